```python
import math
import jax, jax.numpy as jnp
from jax import lax
import numpy as np

D_MODEL = 2048
BATCH = 1
SEQ = 16384
DEPTH = 2

PLE_DIM = 256
D_FF = 5632
DA_HEADS = 8
DA_HEAD_DIM = 64
DA_V_DIM = 2 * DA_HEAD_DIM
RET_HEADS = 8
RET_K_DIM = 128
RET_V_DIM = 128
Q_BLOCK = 128
RET_CHUNK = 128
ROPE_BASE = 10000.0
EPS = 1e-6

DA_QK_W = DA_HEADS * 2 * DA_HEAD_DIM
DA_V_W = DA_HEADS * DA_V_DIM
RET_QK_W = RET_HEADS * RET_K_DIM
RET_V_W = RET_HEADS * RET_V_DIM
IN_SIZES = (DA_QK_W, DA_QK_W, DA_V_W, RET_QK_W, RET_QK_W, RET_V_W, RET_V_W, D_MODEL, D_MODEL)
N_IN = sum(IN_SIZES)

kernel_name = "hybrid_diffattn_retention_gated_macaron"


def rmsnorm(x, g):
    xf = x.astype(jnp.float32)
    y = xf * lax.rsqrt(jnp.mean(xf * xf, axis=-1, keepdims=True) + EPS)
    return (y * g.astype(jnp.float32)).astype(x.dtype)


def swiglu(h, w_gate, w_up, w_down):
    return (jax.nn.silu(h @ w_gate) * (h @ w_up)) @ w_down


def split_in(z):
    offsets = np.cumsum(np.array(IN_SIZES))[:-1].tolist()
    return jnp.split(z, offsets, axis=-1)


def diff_attention(qa, ka, va, q_g, k_g, lq1, lk1, lq2, lk2, sub_g, lambda_init):
    b, s, _ = qa.shape
    q = rmsnorm(qa.reshape(b, s, DA_HEADS, 2, DA_HEAD_DIM), q_g)
    k = rmsnorm(ka.reshape(b, s, DA_HEADS, 2, DA_HEAD_DIM), k_g)
    q = q.transpose(0, 2, 3, 1, 4)
    k = k.transpose(0, 2, 3, 1, 4)
    v = va.reshape(b, s, DA_HEADS, DA_V_DIM).transpose(0, 2, 1, 3)
    lam = (jnp.exp(jnp.sum(lq1.astype(jnp.float32) * lk1.astype(jnp.float32)))
           - jnp.exp(jnp.sum(lq2.astype(jnp.float32) * lk2.astype(jnp.float32)))
           + lambda_init)
    scale = DA_HEAD_DIM ** -0.5
    n_blk = s // Q_BLOCK
    q_blocks = q.reshape(b, DA_HEADS, 2, n_blk, Q_BLOCK, DA_HEAD_DIM).transpose(3, 0, 1, 2, 4, 5)
    key_pos = jnp.arange(s)

    def one_block(args):
        qb, i = args
        sc = jnp.einsum('bhcqd,bhckd->bhcqk', qb, k).astype(jnp.float32) * scale
        q_pos = i * Q_BLOCK + jnp.arange(Q_BLOCK)
        mask = key_pos[None, :] <= q_pos[:, None]
        sc = jnp.where(mask, sc, jnp.finfo(jnp.float32).min)
        a = jax.nn.softmax(sc, axis=-1)
        attn = a[:, :, 0] - lam * a[:, :, 1]
        return jnp.einsum('bhqk,bhkd->bhqd', attn.astype(v.dtype), v)

    o = lax.map(one_block, (q_blocks, jnp.arange(n_blk)))
    o = o.transpose(1, 0, 3, 2, 4).reshape(b, s, DA_HEADS, DA_V_DIM)
    o = rmsnorm(o, sub_g) * (1.0 - lambda_init)
    return o.reshape(b, s, DA_V_W)


def rotary(x, positions):
    d = x.shape[-1]
    inv_freq = ROPE_BASE ** (-jnp.arange(0, d, 2, dtype=jnp.float32) / d)
    ang = positions.astype(jnp.float32)[..., None] * inv_freq
    cos = jnp.cos(ang)[:, :, None, :]
    sin = jnp.sin(ang)[:, :, None, :]
    x1, x2 = x[..., : d // 2], x[..., d // 2:]
    return jnp.concatenate([x1 * cos - x2 * sin, x1 * sin + x2 * cos], axis=-1)


def retention(qr, kr, vr, gr, positions, sub_g):
    b, s, _ = qr.shape
    n, c = s // RET_CHUNK, RET_CHUNK
    q = rotary(qr.astype(jnp.float32).reshape(b, s, RET_HEADS, RET_K_DIM), positions)
    k = rotary(kr.astype(jnp.float32).reshape(b, s, RET_HEADS, RET_K_DIM), positions) * (RET_K_DIM ** -0.5)
    v = vr.astype(jnp.float32).reshape(b, s, RET_HEADS, RET_V_DIM)
    to_chunks = lambda t: t.reshape(b, n, c, RET_HEADS, t.shape[-1]).transpose(0, 3, 1, 2, 4)
    q, k, v = to_chunks(q), to_chunks(k), to_chunks(v)

    log_g = jnp.log(1.0 - 2.0 ** (-5.0 - jnp.arange(RET_HEADS, dtype=jnp.float32)))
    idx = jnp.arange(c, dtype=jnp.float32)
    rel = idx[:, None] - idx[None, :]
    decay_mat = jnp.exp(log_g[:, None, None] * jnp.maximum(rel, 0.0)) * (rel >= 0)
    zeta = jnp.exp(log_g[:, None] * (c - 1 - idx))
    xi = jnp.exp(log_g[:, None] * (idx + 1.0))
    chunk_decay = jnp.exp(log_g * c)

    scores = jnp.einsum('bhncd,bhnmd->bhncm', q, k) * decay_mat[None, :, None]
    o_inner = jnp.einsum('bhncm,bhnme->bhnce', scores, v)
    kv = jnp.einsum('bhnmd,bhnme->bhnde', k * zeta[None, :, None, :, None], v)

    def step(state, kv_n):
        return chunk_decay[None, :, None, None] * state + kv_n, state

    init = jnp.zeros((b, RET_HEADS, RET_K_DIM, RET_V_DIM), jnp.float32)
    _, r_prev = lax.scan(step, init, kv.transpose(2, 0, 1, 3, 4))
    o_cross = jnp.einsum('bhncd,nbhde->bhnce', q * xi[None, :, None, :, None], r_prev)
    o = (o_inner + o_cross).transpose(0, 2, 3, 1, 4).reshape(b, s, RET_HEADS, RET_V_DIM)
    o = rmsnorm(o, sub_g).reshape(b, s, RET_V_W)
    return o * jax.nn.silu(gr.astype(jnp.float32))


def setup_inputs(seed: int = 0) -> dict:
    key = jax.random.key(seed)
    ks = iter(jax.random.split(key, 40))
    L, D, F = DEPTH, D_MODEL, D_FF
    w = lambda shape, fan_in: jax.random.normal(next(ks), shape, jnp.float32) * (fan_in ** -0.5)
    gain = lambda shape: 1.0 + 0.02 * jax.random.normal(next(ks), shape, jnp.float32)
    small = lambda shape: 0.1 * jax.random.normal(next(ks), shape, jnp.float32)
    return {
        "x": jax.random.normal(next(ks), (BATCH, SEQ, D), jnp.float32),
        "p": jax.random.normal(next(ks), (L, BATCH, SEQ, PLE_DIM), jnp.float32),
        "positions": jnp.broadcast_to(jnp.arange(SEQ, dtype=jnp.int32)[None, :], (BATCH, SEQ)),
        "ffn1_norm": gain((L, D)),
        "ffn1_w_gate": w((L, D, F), D),
        "ffn1_w_up": w((L, D, F), D),
        "ffn1_w_down": w((L, F, D), F),
        "mix_norm": gain((L, D)),
        "w_in": w((L, D, N_IN), D),
        "da_q_norm": gain((L, DA_HEAD_DIM)),
        "da_k_norm": gain((L, DA_HEAD_DIM)),
        "da_lambda_q1": small((L, DA_HEAD_DIM)),
        "da_lambda_k1": small((L, DA_HEAD_DIM)),
        "da_lambda_q2": small((L, DA_HEAD_DIM)),
        "da_lambda_k2": small((L, DA_HEAD_DIM)),
        "da_sub_norm": gain((L, DA_V_DIM)),
        "ret_sub_norm": gain((L, RET_V_DIM)),
        "w_up_a": w((L, DA_V_W, D), DA_V_W),
        "w_up_b": w((L, RET_V_W, D), RET_V_W),
        "w_out": w((L, D, D), D),
        "ffn2_norm": gain((L, D)),
        "ffn2_w_gate": w((L, D, F), D),
        "ffn2_w_up": w((L, D, F), D),
        "ffn2_w_down": w((L, F, D), F),
        "ple_norm": gain((L, D)),
        "w_ple_gate": w((L, D, D), D),
        "w_ple_proj": w((L, PLE_DIM, D), PLE_DIM),
    }


def reference(x, p, positions, ffn1_norm, ffn1_w_gate, ffn1_w_up, ffn1_w_down, mix_norm, w_in,
              da_q_norm, da_k_norm, da_lambda_q1, da_lambda_k1, da_lambda_q2, da_lambda_k2,
              da_sub_norm, ret_sub_norm, w_up_a, w_up_b, w_out, ffn2_norm, ffn2_w_gate,
              ffn2_w_up, ffn2_w_down, ple_norm, w_ple_gate, w_ple_proj):
    for i in range(DEPTH):
        lambda_init = 0.8 - 0.6 * math.exp(-0.3 * i)
        x = x + 0.5 * swiglu(rmsnorm(x, ffn1_norm[i]), ffn1_w_gate[i], ffn1_w_up[i], ffn1_w_down[i])
        h = rmsnorm(x, mix_norm[i])
        qa, ka, va, qr, kr, vr, gr, ga, gb = split_in(h @ w_in[i])
        ya = diff_attention(qa, ka, va, da_q_norm[i], da_k_norm[i], da_lambda_q1[i], da_lambda_k1[i],
                            da_lambda_q2[i], da_lambda_k2[i], da_sub_norm[i], lambda_init)
        yb = retention(qr, kr, vr, gr, positions, ret_sub_norm[i]).astype(x.dtype)
        merged = jax.nn.sigmoid(ga) * (ya @ w_up_a[i]) + jax.nn.sigmoid(gb) * (yb @ w_up_b[i])
        x = x + merged @ w_out[i]
        x = x + 0.5 * swiglu(rmsnorm(x, ffn2_norm[i]), ffn2_w_gate[i], ffn2_w_up[i], ffn2_w_down[i])
        gate = jax.nn.sigmoid(rmsnorm(x, ple_norm[i]) @ w_ple_gate[i])
        x = x + gate * (p[i] @ w_ple_proj[i])
    return x
```

```python
import functools
import math

import jax
import jax.numpy as jnp
from jax import lax
from jax.experimental import pallas as pl
from jax.experimental.pallas import tpu as pltpu

F32 = jnp.float32
BF16 = jnp.bfloat16

EPS = 1e-6
ROPE_BASE = 10000.0
DA_HEADS = 8
DA_HEAD_DIM = 64
RET_HEADS = 8
HEAD_W = 128
RET_SCALE = HEAD_W ** -0.5
DA_SCALE = DA_HEAD_DIM ** -0.5
NEG_BIG = -1e30

QA_BLK, KA_BLK, VA_BLK, QR_BLK, KR_BLK, VR_BLK, GR_BLK = 0, 8, 16, 24, 32, 40, 48
GATE_BLK_W = 1024
GA_BLK, GB_BLK = 7, 9

MIB = 1024 * 1024


def _params(sem, vmem_mib):
    return pltpu.CompilerParams(dimension_semantics=sem, vmem_limit_bytes=vmem_mib * MIB)


def _rms(x, g):
    return x * lax.rsqrt(jnp.mean(x * x, axis=-1, keepdims=True) + EPS) * g


def _ffn_kernel(x_ref, g_ref, wg_ref, wu_ref, wd_ref, o_ref, h_ref):
    j = pl.program_id(1)

    @pl.when(j == 0)
    def _():
        x = x_ref[...]
        h_ref[...] = _rms(x, g_ref[...]).astype(BF16)
        o_ref[...] = x

    h = h_ref[...]
    gate = jnp.dot(h, wg_ref[...], preferred_element_type=F32)
    up = jnp.dot(h, wu_ref[...], preferred_element_type=F32)
    act = (gate * jax.nn.sigmoid(gate) * up * 0.5).astype(BF16)
    o_ref[...] += jnp.dot(act, wd_ref[...], preferred_element_type=F32)


def _ffn(x, g, wg, wu, wd, *, tm=512, tf=512):
    s, d = x.shape
    tm = min(tm, s)
    f = wg.shape[1]
    return pl.pallas_call(
        _ffn_kernel,
        out_shape=jax.ShapeDtypeStruct((s, d), F32),
        grid=(s // tm, f // tf),
        in_specs=[
            pl.BlockSpec((tm, d), lambda i, j: (i, 0)),
            pl.BlockSpec((1, d), lambda i, j: (0, 0)),
            pl.BlockSpec((d, tf), lambda i, j: (0, j)),
            pl.BlockSpec((d, tf), lambda i, j: (0, j)),
            pl.BlockSpec((tf, d), lambda i, j: (j, 0)),
        ],
        out_specs=pl.BlockSpec((tm, d), lambda i, j: (i, 0)),
        scratch_shapes=[pltpu.VMEM((tm, d), BF16)],
        compiler_params=_params(("arbitrary", "arbitrary"), 48),
        name="ffn",
    )(x, g, wg, wu, wd)


def _inproj_kernel(x_ref, g_ref, w_ref, z_ref, h_ref):
    @pl.when(pl.program_id(1) == 0)
    def _():
        h_ref[...] = _rms(x_ref[...], g_ref[...]).astype(BF16)

    z_ref[...] = jnp.dot(h_ref[...], w_ref[...], preferred_element_type=F32).astype(BF16)


def _inproj(x, g, w, *, tm=1024, tn=1024):
    s, d = x.shape
    tm = min(tm, s)
    n = w.shape[1]
    return pl.pallas_call(
        _inproj_kernel,
        out_shape=jax.ShapeDtypeStruct((s, n), BF16),
        grid=(s // tm, n // tn),
        in_specs=[
            pl.BlockSpec((tm, d), lambda i, j: (i, 0)),
            pl.BlockSpec((1, d), lambda i, j: (0, 0)),
            pl.BlockSpec((d, tn), lambda i, j: (0, j)),
        ],
        out_specs=pl.BlockSpec((tm, tn), lambda i, j: (i, j)),
        scratch_shapes=[pltpu.VMEM((tm, d), BF16)],
        compiler_params=_params(("arbitrary", "arbitrary"), 48),
        name="in_proj",
    )(x, g, w)


def _half_norm(x, g, lo):
    xx = x * x
    s_lo = jnp.sum(jnp.where(lo, xx, 0.0), axis=-1, keepdims=True)
    s_hi = jnp.sum(jnp.where(lo, 0.0, xx), axis=-1, keepdims=True)
    inv = jnp.where(lo, lax.rsqrt(s_lo * (1.0 / DA_HEAD_DIM) + EPS),
                    lax.rsqrt(s_hi * (1.0 / DA_HEAD_DIM) + EPS))
    return x * inv * g


def _attn_kernel(q_ref, k_ref, v_ref, qg_ref, kg_ref, lq1_ref, lk1_ref, lq2_ref, lk2_ref, sg_ref,
                 o_ref, kn_ref, qs_ref, m_ref, l_ref, acc_ref, *, tq, lambda_init):
    i = pl.program_id(1)
    s_len = k_ref.shape[0]
    lo = lax.broadcasted_iota(jnp.int32, (1, HEAD_W), 1) < DA_HEAD_DIM
    norm_rows = min(1024, s_len)

    @pl.when(i == 0)
    def _():
        def body(c, carry):
            r = pl.multiple_of(c * norm_rows, norm_rows)
            kk = k_ref[pl.ds(r, norm_rows), :].astype(F32)
            kn_ref[pl.ds(r, norm_rows), :] = _half_norm(kk, kg_ref[...], lo).astype(BF16)
            return carry
        lax.fori_loop(0, s_len // norm_rows, body, 0)

    qn = _half_norm(q_ref[...].astype(F32), qg_ref[...], lo) * DA_SCALE
    qs_ref[0:tq, :] = jnp.where(lo, qn, 0.0).astype(BF16)
    qs_ref[tq:2 * tq, :] = jnp.where(lo, 0.0, qn).astype(BF16)
    m_ref[...] = jnp.full(m_ref.shape, NEG_BIG, F32)
    l_ref[...] = jnp.zeros(l_ref.shape, F32)
    acc_ref[...] = jnp.zeros(acc_ref.shape, F32)

    def step(jb, masked):
        r = pl.multiple_of(jb * tq, tq)
        k = kn_ref[pl.ds(r, tq), :]
        v = v_ref[pl.ds(r, tq), :]
        s = lax.dot_general(qs_ref[...], k, (((1,), (1,)), ((), ())), preferred_element_type=F32)
        if masked:
            row = lax.broadcasted_iota(jnp.int32, s.shape, 0)
            col = lax.broadcasted_iota(jnp.int32, s.shape, 1)
            row = jnp.where(row >= tq, row - tq, row)
            s = jnp.where(col <= row, s, NEG_BIG)
        m_prev = m_ref[...]
        m_new = jnp.maximum(m_prev, jnp.max(s, axis=-1, keepdims=True))
        alpha = jnp.exp(m_prev - m_new)
        p = jnp.exp(s - m_new)
        l_ref[...] = alpha * l_ref[...] + jnp.sum(p, axis=-1, keepdims=True)
        acc_ref[...] = alpha * acc_ref[...] + jnp.dot(p.astype(BF16), v, preferred_element_type=F32)
        m_ref[...] = m_new

    def full_step(jb, carry):
        step(jb, False)
        return carry

    lax.fori_loop(0, i, full_step, 0)
    step(i, True)

    o1 = acc_ref[0:tq, :] / l_ref[0:tq, :]
    o2 = acc_ref[tq:2 * tq, :] / l_ref[tq:2 * tq, :]
    lam = (jnp.exp(jnp.sum(lq1_ref[...] * lk1_ref[...], axis=-1, keepdims=True))
           - jnp.exp(jnp.sum(lq2_ref[...] * lk2_ref[...], axis=-1, keepdims=True)) + lambda_init)
    o = o1 - lam * o2
    o = _rms(o, sg_ref[...]) * (1.0 - lambda_init)
    o_ref[...] = o.astype(BF16)


def _attn(z, qg, kg, lq1, lk1, lq2, lk2, sg, *, lambda_init, tq=512):
    s = z.shape[0]
    tq = min(tq, s)
    small = lambda w: pl.BlockSpec((1, w), lambda h, i: (0, 0))
    kern = functools.partial(_attn_kernel, tq=tq, lambda_init=lambda_init)
    return pl.pallas_call(
        kern,
        out_shape=jax.ShapeDtypeStruct((s, DA_HEADS * HEAD_W), BF16),
        grid=(DA_HEADS, s // tq),
        in_specs=[
            pl.BlockSpec((tq, HEAD_W), lambda h, i: (i, QA_BLK + h)),
            pl.BlockSpec((s, HEAD_W), lambda h, i: (0, KA_BLK + h)),
            pl.BlockSpec((s, HEAD_W), lambda h, i: (0, VA_BLK + h)),
            small(HEAD_W), small(HEAD_W),
            small(DA_HEAD_DIM), small(DA_HEAD_DIM), small(DA_HEAD_DIM), small(DA_HEAD_DIM),
            small(HEAD_W),
        ],
        out_specs=pl.BlockSpec((tq, HEAD_W), lambda h, i: (i, h)),
        scratch_shapes=[
            pltpu.VMEM((s, HEAD_W), BF16),
            pltpu.VMEM((2 * tq, HEAD_W), BF16),
            pltpu.VMEM((2 * tq, 1), F32),
            pltpu.VMEM((2 * tq, 1), F32),
            pltpu.VMEM((2 * tq, HEAD_W), F32),
        ],
        compiler_params=_params(("arbitrary", "arbitrary"), 48),
        name="diff_attn",
    )(z, z, z, qg, kg, lq1, lk1, lq2, lk2, sg)


def _rope_kernel(pos_ref, invf_ref, cos_ref, sin_ref):
    ang = pos_ref[...].astype(F32) * invf_ref[...]
    lo = lax.broadcasted_iota(jnp.int32, (1, HEAD_W), 1) < HEAD_W // 2
    cos_ref[...] = jnp.cos(ang)
    sin_ref[...] = jnp.where(lo, -1.0, 1.0) * jnp.sin(ang)


def _rope_tables(pos_col, invf, *, tm=1024):
    s = pos_col.shape[0]
    tm = min(tm, s)
    return pl.pallas_call(
        _rope_kernel,
        out_shape=(jax.ShapeDtypeStruct((s, HEAD_W), F32), jax.ShapeDtypeStruct((s, HEAD_W), F32)),
        grid=(s // tm,),
        in_specs=[pl.BlockSpec((tm, 1), lambda i: (i, 0)), pl.BlockSpec((1, HEAD_W), lambda i: (0, 0))],
        out_specs=(pl.BlockSpec((tm, HEAD_W), lambda i: (i, 0)), pl.BlockSpec((tm, HEAD_W), lambda i: (i, 0))),
        compiler_params=_params(("arbitrary",), 32),
        name="rope_tables",
    )(pos_col, invf)


def _ret_kernel(q_ref, k_ref, v_ref, gr_ref, cos_ref, sin_ref, dec_ref, zeta_ref, xi_ref, cd_ref, sg_ref,
                o_ref, st_ref):
    @pl.when(pl.program_id(1) == 0)
    def _():
        st_ref[...] = jnp.zeros(st_ref.shape, F32)

    cosf = cos_ref[...]
    sinf = sin_ref[...]
    q = q_ref[...].astype(F32)
    k = k_ref[...].astype(F32)
    q = q * cosf + pltpu.roll(q, HEAD_W // 2, 1) * sinf
    k = (k * cosf + pltpu.roll(k, HEAD_W // 2, 1) * sinf) * RET_SCALE
    v = v_ref[...]
    st = st_ref[...]

    sc = lax.dot_general(q.astype(BF16), k.astype(BF16), (((1,), (1,)), ((), ())),
                         preferred_element_type=F32) * dec_ref[...]
    o = jnp.dot(sc.astype(BF16), v, preferred_element_type=F32)
    o = o + jnp.dot((q * xi_ref[...]).astype(BF16), st.astype(BF16), preferred_element_type=F32)
    kz = (k * zeta_ref[...]).astype(BF16)
    kv = lax.dot_general(kz, v, (((0,), (0,)), ((), ())), preferred_element_type=F32)
    st_ref[...] = cd_ref[0:1, :] * st + kv

    g = gr_ref[...].astype(F32)
    o_ref[...] = (_rms(o, sg_ref[...]) * (g * jax.nn.sigmoid(g))).astype(BF16)


def _retention(z, cosf, sinf, dec, zeta, xi, cd, sg, *, chunk):
    s = z.shape[0]
    row = lambda blk: pl.BlockSpec((chunk, HEAD_W), lambda h, i: (i, blk + h))
    tab = pl.BlockSpec((chunk, HEAD_W), lambda h, i: (i, 0))
    per_head = lambda r, c: pl.BlockSpec((None, r, c), lambda h, i: (h, 0, 0))
    return pl.pallas_call(
        _ret_kernel,
        out_shape=jax.ShapeDtypeStruct((s, RET_HEADS * HEAD_W), BF16),
        grid=(RET_HEADS, s // chunk),
        in_specs=[
            row(QR_BLK), row(KR_BLK), row(VR_BLK), row(GR_BLK), tab, tab,
            per_head(chunk, chunk), per_head(chunk, HEAD_W), per_head(chunk, HEAD_W), per_head(8, HEAD_W),
            pl.BlockSpec((1, HEAD_W), lambda h, i: (0, 0)),
        ],
        out_specs=pl.BlockSpec((chunk, HEAD_W), lambda h, i: (i, h)),
        scratch_shapes=[pltpu.VMEM((HEAD_W, HEAD_W), F32)],
        compiler_params=_params(("arbitrary", "arbitrary"), 32),
        name="retention",
    )(z, z, z, z, cosf, sinf, dec, zeta, xi, cd, sg)


def _retention_tables(chunk):
    log_g = jnp.log(1.0 - 2.0 ** (-5.0 - jnp.arange(RET_HEADS, dtype=F32)))
    idx = jnp.arange(chunk, dtype=F32)
    rel = idx[:, None] - idx[None, :]
    dec = jnp.exp(log_g[:, None, None] * jnp.maximum(rel, 0.0)) * (rel >= 0)
    zeta = jnp.exp(log_g[:, None] * (chunk - 1 - idx))
    xi = jnp.exp(log_g[:, None] * (idx + 1.0))
    cd = jnp.exp(log_g * chunk)
    bcast = lambda t: jnp.broadcast_to(t[:, :, None], (RET_HEADS, chunk, HEAD_W))
    return dec, bcast(zeta), bcast(xi), jnp.broadcast_to(cd[:, None, None], (RET_HEADS, 8, HEAD_W))


def _merge_kernel(x_ref, ya_ref, yb_ref, ga0_ref, ga1_ref, gb0_ref, gb1_ref, wa_ref, wb_ref, wo_ref, o_ref):
    ta = jnp.dot(ya_ref[...], wa_ref[...], preferred_element_type=F32)
    tb = jnp.dot(yb_ref[...], wb_ref[...], preferred_element_type=F32)
    ga = jnp.concatenate([ga0_ref[...], ga1_ref[...]], axis=-1).astype(F32)
    gb = jnp.concatenate([gb0_ref[...], gb1_ref[...]], axis=-1).astype(F32)
    merged = jax.nn.sigmoid(ga) * ta + jax.nn.sigmoid(gb) * tb
    o_ref[...] = x_ref[...] + jnp.dot(merged.astype(BF16), wo_ref[...], preferred_element_type=F32)


def _merge(x, ya, yb, z, wa, wb, wo, *, tm=256):
    s, d = x.shape
    gate = lambda blk: pl.BlockSpec((tm, GATE_BLK_W), lambda i: (i, blk))
    resident = lambda r, c: pl.BlockSpec((r, c), lambda i: (0, 0), pipeline_mode=pl.Buffered(1))
    return pl.pallas_call(
        _merge_kernel,
        out_shape=jax.ShapeDtypeStruct((s, d), F32),
        grid=(s // tm,),
        in_specs=[
            pl.BlockSpec((tm, d), lambda i: (i, 0)),
            pl.BlockSpec((tm, ya.shape[1]), lambda i: (i, 0)),
            pl.BlockSpec((tm, yb.shape[1]), lambda i: (i, 0)),
            gate(GA_BLK), gate(GA_BLK + 1), gate(GB_BLK), gate(GB_BLK + 1),
            resident(*wa.shape), resident(*wb.shape), resident(*wo.shape),
        ],
        out_specs=pl.BlockSpec((tm, d), lambda i: (i, 0)),
        compiler_params=_params(("arbitrary",), 48),
        name="merge_out",
    )(x, ya, yb, z, z, z, z, wa, wb, wo)


def _ple_kernel(x_ref, g_ref, p_ref, wg_ref, wp_ref, o_ref):
    x = x_ref[...]
    h = _rms(x, g_ref[...]).astype(BF16)
    gate = jax.nn.sigmoid(jnp.dot(h, wg_ref[...], preferred_element_type=F32))
    proj = jnp.dot(p_ref[...].astype(BF16), wp_ref[...], preferred_element_type=F32)
    o_ref[...] = x + gate * proj


def _ple(x, g, p, wg, wp, *, tm=512):
    s, d = x.shape
    resident = lambda r, c: pl.BlockSpec((r, c), lambda i: (0, 0), pipeline_mode=pl.Buffered(1))
    return pl.pallas_call(
        _ple_kernel,
        out_shape=jax.ShapeDtypeStruct((s, d), F32),
        grid=(s // tm,),
        in_specs=[
            pl.BlockSpec((tm, d), lambda i: (i, 0)),
            pl.BlockSpec((1, d), lambda i: (0, 0)),
            pl.BlockSpec((tm, p.shape[1]), lambda i: (i, 0)),
            resident(*wg.shape), resident(*wp.shape),
        ],
        out_specs=pl.BlockSpec((tm, d), lambda i: (i, 0)),
        compiler_params=_params(("arbitrary",), 48),
        name="ple",
    )(x, g, p, wg, wp)


def kernel(x, p, positions, ffn1_norm, ffn1_w_gate, ffn1_w_up, ffn1_w_down, mix_norm, w_in, da_q_norm, da_k_norm, da_lambda_q1, da_lambda_k1, da_lambda_q2, da_lambda_k2, da_sub_norm, ret_sub_norm, w_up_a, w_up_b, w_out, ffn2_norm, ffn2_w_gate, ffn2_w_up, ffn2_w_down, ple_norm, w_ple_gate, w_ple_proj):
    b, s, d = x.shape
    assert b == 1
    depth = w_in.shape[0]
    chunk = min(512, s)

    row = lambda t: t.reshape(1, -1)
    twice = lambda t: jnp.concatenate([t, t]).reshape(1, -1)
    bf = lambda t: t.astype(BF16)

    inv_freq = ROPE_BASE ** (-jnp.arange(0, HEAD_W, 2, dtype=F32) / HEAD_W)
    cosf, sinf = _rope_tables(positions.reshape(s, 1), twice(inv_freq))
    dec, zeta, xi, cd = _retention_tables(chunk)

    xc = x.reshape(s, d)
    for i in range(depth):
        lambda_init = 0.8 - 0.6 * math.exp(-0.3 * i)
        xc = _ffn(xc, row(ffn1_norm[i]), bf(ffn1_w_gate[i]), bf(ffn1_w_up[i]), bf(ffn1_w_down[i]))
        z = _inproj(xc, row(mix_norm[i]), bf(w_in[i]))
        ya = _attn(z, twice(da_q_norm[i]), twice(da_k_norm[i]), row(da_lambda_q1[i]), row(da_lambda_k1[i]),
                   row(da_lambda_q2[i]), row(da_lambda_k2[i]), row(da_sub_norm[i]), lambda_init=lambda_init)
        yb = _retention(z, cosf, sinf, dec, zeta, xi, cd, row(ret_sub_norm[i]), chunk=chunk)
        xc = _merge(xc, ya, yb, z, bf(w_up_a[i]), bf(w_up_b[i]), bf(w_out[i]))
        xc = _ffn(xc, row(ffn2_norm[i]), bf(ffn2_w_gate[i]), bf(ffn2_w_up[i]), bf(ffn2_w_down[i]))
        xc = _ple(xc, row(ple_norm[i]), p[i].reshape(s, -1), bf(w_ple_gate[i]), bf(w_ple_proj[i]))
    return xc.reshape(b, s, d)
```

```python
import functools
import math

import jax
import jax.numpy as jnp
from jax import lax
from jax.experimental import pallas as pl
from jax.experimental.pallas import tpu as pltpu

F32 = jnp.float32
BF16 = jnp.bfloat16

EPS = 1e-6
ROPE_BASE = 10000.0
DA_HEADS = 8
DA_HEAD_DIM = 64
RET_HEADS = 8
HEAD_W = 128
RET_SCALE = HEAD_W ** -0.5
DA_SCALE = DA_HEAD_DIM ** -0.5
NEG_BIG = -1e30
LOG2E = math.log2(math.e)
ACC_ROWS = HEAD_W + 16

QA_BLK, KA_BLK, VA_BLK, QR_BLK, KR_BLK, VR_BLK, GR_BLK = 0, 8, 16, 24, 32, 40, 48
GATE_BLK_W = 1024
GA_BLK, GB_BLK = 7, 9

MIB = 1024 * 1024


def _params(sem, vmem_mib):
    return pltpu.CompilerParams(dimension_semantics=sem, vmem_limit_bytes=vmem_mib * MIB)


def _rms(x, g):
    return x * lax.rsqrt(jnp.mean(x * x, axis=-1, keepdims=True) + EPS) * g


def _ffn_kernel(x_ref, g_ref, wg_ref, wu_ref, wd_ref, o_ref, h_ref):
    j = pl.program_id(1)

    @pl.when(j == 0)
    def _():
        x = x_ref[...]
        h_ref[...] = _rms(x, g_ref[...]).astype(BF16)
        o_ref[...] = x

    h = h_ref[...]
    gate = jnp.dot(h, wg_ref[...], preferred_element_type=F32)
    up = jnp.dot(h, wu_ref[...], preferred_element_type=F32)
    act = (gate * jax.nn.sigmoid(gate) * up * 0.5).astype(BF16)
    o_ref[...] += jnp.dot(act, wd_ref[...], preferred_element_type=F32)


def _ffn(x, g, wg, wu, wd, *, tm=512, tf=512):
    s, d = x.shape
    tm = min(tm, s)
    f = wg.shape[1]
    return pl.pallas_call(
        _ffn_kernel,
        out_shape=jax.ShapeDtypeStruct((s, d), F32),
        grid=(s // tm, f // tf),
        in_specs=[
            pl.BlockSpec((tm, d), lambda i, j: (i, 0)),
            pl.BlockSpec((1, d), lambda i, j: (0, 0)),
            pl.BlockSpec((d, tf), lambda i, j: (0, j)),
            pl.BlockSpec((d, tf), lambda i, j: (0, j)),
            pl.BlockSpec((tf, d), lambda i, j: (j, 0)),
        ],
        out_specs=pl.BlockSpec((tm, d), lambda i, j: (i, 0)),
        scratch_shapes=[pltpu.VMEM((tm, d), BF16)],
        compiler_params=_params(("arbitrary", "arbitrary"), 48),
        name="ffn",
    )(x, g, wg, wu, wd)


def _inproj_kernel(x_ref, g_ref, w_ref, z_ref, h_ref):
    @pl.when(pl.program_id(1) == 0)
    def _():
        h_ref[...] = _rms(x_ref[...], g_ref[...]).astype(BF16)

    z_ref[...] = jnp.dot(h_ref[...], w_ref[...], preferred_element_type=F32).astype(BF16)


def _inproj(x, g, w, *, tm=1024, tn=1024):
    s, d = x.shape
    tm = min(tm, s)
    n = w.shape[1]
    return pl.pallas_call(
        _inproj_kernel,
        out_shape=jax.ShapeDtypeStruct((s, n), BF16),
        grid=(s // tm, n // tn),
        in_specs=[
            pl.BlockSpec((tm, d), lambda i, j: (i, 0)),
            pl.BlockSpec((1, d), lambda i, j: (0, 0)),
            pl.BlockSpec((d, tn), lambda i, j: (0, j)),
        ],
        out_specs=pl.BlockSpec((tm, tn), lambda i, j: (i, j)),
        scratch_shapes=[pltpu.VMEM((tm, d), BF16)],
        compiler_params=_params(("arbitrary", "arbitrary"), 48),
        name="in_proj",
    )(x, g, w)


def _half_norm(x, g, lo):
    xx = x * x
    s_lo = jnp.sum(jnp.where(lo, xx, 0.0), axis=-1, keepdims=True)
    s_hi = jnp.sum(jnp.where(lo, 0.0, xx), axis=-1, keepdims=True)
    inv = jnp.where(lo, lax.rsqrt(s_lo * (1.0 / DA_HEAD_DIM) + EPS),
                    lax.rsqrt(s_hi * (1.0 / DA_HEAD_DIM) + EPS))
    return x * inv * g


def _attn_kernel(q_ref, k_ref, v_ref, qg_ref, kg_ref, lq1_ref, lk1_ref, lq2_ref, lk2_ref, sg_ref,
                 o_ref, kn_ref, vt_ref, qt_ref, m_ref, acc_ref, *, tq, tk, cw, lambda_init):
    i = pl.program_id(1)
    s_len = k_ref.shape[0]
    lo = lax.broadcasted_iota(jnp.int32, (1, HEAD_W), 1) < DA_HEAD_DIM
    prep_rows = min(512, s_len)

    @pl.when(i == 0)
    def _():
        vt_ref[HEAD_W:ACC_ROWS, :] = jnp.ones((ACC_ROWS - HEAD_W, s_len), BF16)

        def body(c, carry):
            r = pl.multiple_of(c * prep_rows, prep_rows)
            kk = k_ref[pl.ds(r, prep_rows), :].astype(F32)
            kn_ref[pl.ds(r, prep_rows), :] = _half_norm(kk, kg_ref[...], lo).astype(BF16)
            vv = v_ref[pl.ds(r, prep_rows), :].astype(F32)
            vt_ref[0:HEAD_W, pl.ds(r, prep_rows)] = vv.T.astype(BF16)
            return carry
        lax.fori_loop(0, s_len // prep_rows, body, 0)

    qn = _half_norm(q_ref[...].astype(F32), qg_ref[...], lo) * (DA_SCALE * LOG2E)
    qt_ref[:, 0:tq] = jnp.where(lo, qn, 0.0).T.astype(BF16)
    qt_ref[:, tq:2 * tq] = jnp.where(lo, 0.0, qn).T.astype(BF16)
    m_ref[...] = jnp.full(m_ref.shape, NEG_BIG, F32)
    acc_ref[...] = jnp.zeros(acc_ref.shape, F32)

    groups = [(c * cw, (c + 1) * cw) for c in range(2 * tq // cw)]

    def scores(jb):
        r = pl.multiple_of(jb * tk, tk)
        k = kn_ref[pl.ds(r, tk), :]
        out = []
        for a, b in groups:
            s = jnp.dot(k, qt_ref[:, a:b], preferred_element_type=F32)
            out.append((s, jnp.max(s, axis=0, keepdims=True)))
        return tuple(out)

    def update(jb, ss, masked):
        r = pl.multiple_of(jb * tk, tk)
        vt = vt_ref[:, pl.ds(r, tk)]
        for (a, b), (s, s_max) in zip(groups, ss):
            if masked:
                key = r + lax.broadcasted_iota(jnp.int32, s.shape, 0)
                col = a + lax.broadcasted_iota(jnp.int32, s.shape, 1)
                qry = i * tq + jnp.where(col >= tq, col - tq, col)
                s = jnp.where(key <= qry, s, NEG_BIG)
                s_max = jnp.max(s, axis=0, keepdims=True)
            m_prev = m_ref[:, a:b]
            m_new = jnp.maximum(m_prev, s_max)
            alpha = jnp.exp2(m_prev - m_new)
            p = jnp.exp2(s - m_new).astype(BF16)
            acc_ref[:, a:b] = alpha * acc_ref[:, a:b] + jnp.dot(vt, p, preferred_element_type=F32)
            m_ref[:, a:b] = m_new

    def full_step(jb, carry):
        update(jb, scores(jb), False)
        return carry

    n_full = (i * tq) // tk
    lax.fori_loop(0, n_full, full_step, 0)
    update(n_full, scores(n_full), True)

    acc = acc_ref[...]
    o1 = acc[0:HEAD_W, 0:tq] / acc[HEAD_W:HEAD_W + 1, 0:tq]
    o2 = acc[0:HEAD_W, tq:2 * tq] / acc[HEAD_W:HEAD_W + 1, tq:2 * tq]
    lam = (jnp.exp(jnp.sum(lq1_ref[...] * lk1_ref[...], axis=-1, keepdims=True))
           - jnp.exp(jnp.sum(lq2_ref[...] * lk2_ref[...], axis=-1, keepdims=True)) + lambda_init)
    o = (o1 - lam * o2).T
    o = _rms(o, sg_ref[...]) * (1.0 - lambda_init)
    o_ref[...] = o.astype(BF16)


def _attn(z, qg, kg, lq1, lk1, lq2, lk2, sg, *, lambda_init, tq=1024, tk=1024, cw=256):
    s = z.shape[0]
    assert s % tq == 0 and s % tk == 0 and tk % tq == 0
    small = lambda w: pl.BlockSpec((1, w), lambda h, i: (0, 0))
    kern = functools.partial(_attn_kernel, tq=tq, tk=tk, cw=cw, lambda_init=lambda_init)
    return pl.pallas_call(
        kern,
        out_shape=jax.ShapeDtypeStruct((s, DA_HEADS * HEAD_W), BF16),
        grid=(DA_HEADS, s // tq),
        in_specs=[
            pl.BlockSpec((tq, HEAD_W), lambda h, i: (i, QA_BLK + h)),
            pl.BlockSpec((s, HEAD_W), lambda h, i: (0, KA_BLK + h)),
            pl.BlockSpec((s, HEAD_W), lambda h, i: (0, VA_BLK + h)),
            small(HEAD_W), small(HEAD_W),
            small(DA_HEAD_DIM), small(DA_HEAD_DIM), small(DA_HEAD_DIM), small(DA_HEAD_DIM),
            small(HEAD_W),
        ],
        out_specs=pl.BlockSpec((tq, HEAD_W), lambda h, i: (i, h)),
        scratch_shapes=[
            pltpu.VMEM((s, HEAD_W), BF16),
            pltpu.VMEM((ACC_ROWS, s), BF16),
            pltpu.VMEM((HEAD_W, 2 * tq), BF16),
            pltpu.VMEM((1, 2 * tq), F32),
            pltpu.VMEM((ACC_ROWS, 2 * tq), F32),
        ],
        compiler_params=_params(("arbitrary", "arbitrary"), 48),
        name="diff_attn",
    )(z, z, z, qg, kg, lq1, lk1, lq2, lk2, sg)


def _rope_kernel(pos_ref, invf_ref, cos_ref, sin_ref):
    ang = pos_ref[...].astype(F32) * invf_ref[...]
    lo = lax.broadcasted_iota(jnp.int32, (1, HEAD_W), 1) < HEAD_W // 2
    cos_ref[...] = jnp.cos(ang)
    sin_ref[...] = jnp.where(lo, -1.0, 1.0) * jnp.sin(ang)


def _rope_tables(pos_col, invf, *, tm=1024):
    s = pos_col.shape[0]
    tm = min(tm, s)
    return pl.pallas_call(
        _rope_kernel,
        out_shape=(jax.ShapeDtypeStruct((s, HEAD_W), F32), jax.ShapeDtypeStruct((s, HEAD_W), F32)),
        grid=(s // tm,),
        in_specs=[pl.BlockSpec((tm, 1), lambda i: (i, 0)), pl.BlockSpec((1, HEAD_W), lambda i: (0, 0))],
        out_specs=(pl.BlockSpec((tm, HEAD_W), lambda i: (i, 0)), pl.BlockSpec((tm, HEAD_W), lambda i: (i, 0))),
        compiler_params=_params(("arbitrary",), 32),
        name="rope_tables",
    )(pos_col, invf)


def _ret_kernel(q_ref, k_ref, v_ref, gr_ref, cos_ref, sin_ref, dec_ref, zeta_ref, xi_ref, cd_ref, sg_ref,
                o_ref, st_ref):
    @pl.when(pl.program_id(1) == 0)
    def _():
        st_ref[...] = jnp.zeros(st_ref.shape, F32)

    cosf = cos_ref[...]
    sinf = sin_ref[...]
    q = q_ref[...].astype(F32)
    k = k_ref[...].astype(F32)
    q = q * cosf + pltpu.roll(q, HEAD_W // 2, 1) * sinf
    k = (k * cosf + pltpu.roll(k, HEAD_W // 2, 1) * sinf) * RET_SCALE
    v = v_ref[...]
    st = st_ref[...]

    sc = lax.dot_general(q.astype(BF16), k.astype(BF16), (((1,), (1,)), ((), ())),
                         preferred_element_type=F32) * dec_ref[...]
    o = jnp.dot(sc.astype(BF16), v, preferred_element_type=F32)
    o = o + jnp.dot((q * xi_ref[...]).astype(BF16), st.astype(BF16), preferred_element_type=F32)
    kz = (k * zeta_ref[...]).astype(BF16)
    kv = lax.dot_general(kz, v, (((0,), (0,)), ((), ())), preferred_element_type=F32)
    st_ref[...] = cd_ref[0:1, :] * st + kv

    g = gr_ref[...].astype(F32)
    o_ref[...] = (_rms(o, sg_ref[...]) * (g * jax.nn.sigmoid(g))).astype(BF16)


def _retention(z, cosf, sinf, dec, zeta, xi, cd, sg, *, chunk):
    s = z.shape[0]
    row = lambda blk: pl.BlockSpec((chunk, HEAD_W), lambda h, i: (i, blk + h))
    tab = pl.BlockSpec((chunk, HEAD_W), lambda h, i: (i, 0))
    per_head = lambda r, c: pl.BlockSpec((None, r, c), lambda h, i: (h, 0, 0))
    return pl.pallas_call(
        _ret_kernel,
        out_shape=jax.ShapeDtypeStruct((s, RET_HEADS * HEAD_W), BF16),
        grid=(RET_HEADS, s // chunk),
        in_specs=[
            row(QR_BLK), row(KR_BLK), row(VR_BLK), row(GR_BLK), tab, tab,
            per_head(chunk, chunk), per_head(chunk, HEAD_W), per_head(chunk, HEAD_W), per_head(8, HEAD_W),
            pl.BlockSpec((1, HEAD_W), lambda h, i: (0, 0)),
        ],
        out_specs=pl.BlockSpec((chunk, HEAD_W), lambda h, i: (i, h)),
        scratch_shapes=[pltpu.VMEM((HEAD_W, HEAD_W), F32)],
        compiler_params=_params(("arbitrary", "arbitrary"), 32),
        name="retention",
    )(z, z, z, z, cosf, sinf, dec, zeta, xi, cd, sg)


def _retention_tables(chunk):
    log_g = jnp.log(1.0 - 2.0 ** (-5.0 - jnp.arange(RET_HEADS, dtype=F32)))
    idx = jnp.arange(chunk, dtype=F32)
    rel = idx[:, None] - idx[None, :]
    dec = jnp.exp(log_g[:, None, None] * jnp.maximum(rel, 0.0)) * (rel >= 0)
    zeta = jnp.exp(log_g[:, None] * (chunk - 1 - idx))
    xi = jnp.exp(log_g[:, None] * (idx + 1.0))
    cd = jnp.exp(log_g * chunk)
    bcast = lambda t: jnp.broadcast_to(t[:, :, None], (RET_HEADS, chunk, HEAD_W))
    return dec, bcast(zeta), bcast(xi), jnp.broadcast_to(cd[:, None, None], (RET_HEADS, 8, HEAD_W))


def _merge_kernel(x_ref, ya_ref, yb_ref, ga0_ref, ga1_ref, gb0_ref, gb1_ref, wa_ref, wb_ref, wo_ref, o_ref):
    ta = jnp.dot(ya_ref[...], wa_ref[...], preferred_element_type=F32)
    tb = jnp.dot(yb_ref[...], wb_ref[...], preferred_element_type=F32)
    ga = jnp.concatenate([ga0_ref[...], ga1_ref[...]], axis=-1).astype(F32)
    gb = jnp.concatenate([gb0_ref[...], gb1_ref[...]], axis=-1).astype(F32)
    merged = jax.nn.sigmoid(ga) * ta + jax.nn.sigmoid(gb) * tb
    o_ref[...] = x_ref[...] + jnp.dot(merged.astype(BF16), wo_ref[...], preferred_element_type=F32)


def _merge(x, ya, yb, z, wa, wb, wo, *, tm=256):
    s, d = x.shape
    gate = lambda blk: pl.BlockSpec((tm, GATE_BLK_W), lambda i: (i, blk))
    resident = lambda r, c: pl.BlockSpec((r, c), lambda i: (0, 0), pipeline_mode=pl.Buffered(1))
    return pl.pallas_call(
        _merge_kernel,
        out_shape=jax.ShapeDtypeStruct((s, d), F32),
        grid=(s // tm,),
        in_specs=[
            pl.BlockSpec((tm, d), lambda i: (i, 0)),
            pl.BlockSpec((tm, ya.shape[1]), lambda i: (i, 0)),
            pl.BlockSpec((tm, yb.shape[1]), lambda i: (i, 0)),
            gate(GA_BLK), gate(GA_BLK + 1), gate(GB_BLK), gate(GB_BLK + 1),
            resident(*wa.shape), resident(*wb.shape), resident(*wo.shape),
        ],
        out_specs=pl.BlockSpec((tm, d), lambda i: (i, 0)),
        compiler_params=_params(("arbitrary",), 48),
        name="merge_out",
    )(x, ya, yb, z, z, z, z, wa, wb, wo)


def _ple_kernel(x_ref, g_ref, p_ref, wg_ref, wp_ref, o_ref):
    x = x_ref[...]
    h = _rms(x, g_ref[...]).astype(BF16)
    gate = jax.nn.sigmoid(jnp.dot(h, wg_ref[...], preferred_element_type=F32))
    proj = jnp.dot(p_ref[...].astype(BF16), wp_ref[...], preferred_element_type=F32)
    o_ref[...] = x + gate * proj


def _ple(x, g, p, wg, wp, *, tm=512):
    s, d = x.shape
    resident = lambda r, c: pl.BlockSpec((r, c), lambda i: (0, 0), pipeline_mode=pl.Buffered(1))
    return pl.pallas_call(
        _ple_kernel,
        out_shape=jax.ShapeDtypeStruct((s, d), F32),
        grid=(s // tm,),
        in_specs=[
            pl.BlockSpec((tm, d), lambda i: (i, 0)),
            pl.BlockSpec((1, d), lambda i: (0, 0)),
            pl.BlockSpec((tm, p.shape[1]), lambda i: (i, 0)),
            resident(*wg.shape), resident(*wp.shape),
        ],
        out_specs=pl.BlockSpec((tm, d), lambda i: (i, 0)),
        compiler_params=_params(("arbitrary",), 48),
        name="ple",
    )(x, g, p, wg, wp)


def kernel(x, p, positions, ffn1_norm, ffn1_w_gate, ffn1_w_up, ffn1_w_down, mix_norm, w_in, da_q_norm, da_k_norm, da_lambda_q1, da_lambda_k1, da_lambda_q2, da_lambda_k2, da_sub_norm, ret_sub_norm, w_up_a, w_up_b, w_out, ffn2_norm, ffn2_w_gate, ffn2_w_up, ffn2_w_down, ple_norm, w_ple_gate, w_ple_proj):
    b, s, d = x.shape
    assert b == 1
    depth = w_in.shape[0]
    chunk = min(512, s)

    row = lambda t: t.reshape(1, -1)
    twice = lambda t: jnp.concatenate([t, t]).reshape(1, -1)
    bf = lambda t: t.astype(BF16)

    inv_freq = ROPE_BASE ** (-jnp.arange(0, HEAD_W, 2, dtype=F32) / HEAD_W)
    cosf, sinf = _rope_tables(positions.reshape(s, 1), twice(inv_freq))
    dec, zeta, xi, cd = _retention_tables(chunk)

    xc = x.reshape(s, d)
    for i in range(depth):
        lambda_init = 0.8 - 0.6 * math.exp(-0.3 * i)
        xc = _ffn(xc, row(ffn1_norm[i]), bf(ffn1_w_gate[i]), bf(ffn1_w_up[i]), bf(ffn1_w_down[i]))
        z = _inproj(xc, row(mix_norm[i]), bf(w_in[i]))
        ya = _attn(z, twice(da_q_norm[i]), twice(da_k_norm[i]), row(da_lambda_q1[i]), row(da_lambda_k1[i]),
                   row(da_lambda_q2[i]), row(da_lambda_k2[i]), row(da_sub_norm[i]), lambda_init=lambda_init)
        yb = _retention(z, cosf, sinf, dec, zeta, xi, cd, row(ret_sub_norm[i]), chunk=chunk)
        xc = _merge(xc, ya, yb, z, bf(w_up_a[i]), bf(w_up_b[i]), bf(w_out[i]))
        xc = _ffn(xc, row(ffn2_norm[i]), bf(ffn2_w_gate[i]), bf(ffn2_w_up[i]), bf(ffn2_w_down[i]))
        xc = _ple(xc, row(ple_norm[i]), p[i].reshape(s, -1), bf(w_ple_gate[i]), bf(w_ple_proj[i]))
    return xc.reshape(b, s, d)
```

```python
import functools
import math

import jax
import jax.numpy as jnp
from jax import lax
from jax.experimental import pallas as pl
from jax.experimental.pallas import tpu as pltpu

F32 = jnp.float32
BF16 = jnp.bfloat16

EPS = 1e-6
ROPE_BASE = 10000.0
DA_HEADS = 8
DA_HEAD_DIM = 64
RET_HEADS = 8
HEAD_W = 128
RET_SCALE = HEAD_W ** -0.5
DA_SCALE = DA_HEAD_DIM ** -0.5
NEG_BIG = -1e30
LOG2E = math.log2(math.e)
ACC_ROWS = HEAD_W + 16

QA_BLK, KA_BLK, VA_BLK, QR_BLK, KR_BLK, VR_BLK, GR_BLK = 0, 8, 16, 24, 32, 40, 48
GATE_BLK_W = 1024
GA_BLK, GB_BLK = 7, 9

MIB = 1024 * 1024


def _params(sem, vmem_mib):
    return pltpu.CompilerParams(dimension_semantics=sem, vmem_limit_bytes=vmem_mib * MIB)


def _rms(x, g):
    return x * lax.rsqrt(jnp.mean(x * x, axis=-1, keepdims=True) + EPS) * g


def _ffn_kernel(x_ref, g_ref, wg_ref, wu_ref, wd_ref, o_ref, h_ref):
    j = pl.program_id(1)

    @pl.when(j == 0)
    def _():
        x = x_ref[...]
        h_ref[...] = _rms(x, g_ref[...]).astype(BF16)
        o_ref[...] = x

    h = h_ref[...]
    gate = jnp.dot(h, wg_ref[...], preferred_element_type=F32)
    up = jnp.dot(h, wu_ref[...], preferred_element_type=F32)
    act = (gate * jax.nn.sigmoid(gate) * up * 0.5).astype(BF16)
    o_ref[...] += jnp.dot(act, wd_ref[...], preferred_element_type=F32)


def _ffn(x, g, wg, wu, wd, *, tm=512, tf=512):
    s, d = x.shape
    tm = min(tm, s)
    f = wg.shape[1]
    return pl.pallas_call(
        _ffn_kernel,
        out_shape=jax.ShapeDtypeStruct((s, d), F32),
        grid=(s // tm, f // tf),
        in_specs=[
            pl.BlockSpec((tm, d), lambda i, j: (i, 0)),
            pl.BlockSpec((1, d), lambda i, j: (0, 0)),
            pl.BlockSpec((d, tf), lambda i, j: (0, j)),
            pl.BlockSpec((d, tf), lambda i, j: (0, j)),
            pl.BlockSpec((tf, d), lambda i, j: (j, 0)),
        ],
        out_specs=pl.BlockSpec((tm, d), lambda i, j: (i, 0)),
        scratch_shapes=[pltpu.VMEM((tm, d), BF16)],
        compiler_params=_params(("arbitrary", "arbitrary"), 48),
        name="ffn",
    )(x, g, wg, wu, wd)


def _inproj_kernel(x_ref, g_ref, w_ref, z_ref, h_ref):
    @pl.when(pl.program_id(1) == 0)
    def _():
        h_ref[...] = _rms(x_ref[...], g_ref[...]).astype(BF16)

    z_ref[...] = jnp.dot(h_ref[...], w_ref[...], preferred_element_type=F32).astype(BF16)


def _inproj(x, g, w, *, tm=1024, tn=1024):
    s, d = x.shape
    tm = min(tm, s)
    n = w.shape[1]
    return pl.pallas_call(
        _inproj_kernel,
        out_shape=jax.ShapeDtypeStruct((s, n), BF16),
        grid=(s // tm, n // tn),
        in_specs=[
            pl.BlockSpec((tm, d), lambda i, j: (i, 0)),
            pl.BlockSpec((1, d), lambda i, j: (0, 0)),
            pl.BlockSpec((d, tn), lambda i, j: (0, j)),
        ],
        out_specs=pl.BlockSpec((tm, tn), lambda i, j: (i, j)),
        scratch_shapes=[pltpu.VMEM((tm, d), BF16)],
        compiler_params=_params(("arbitrary", "arbitrary"), 48),
        name="in_proj",
    )(x, g, w)


def _half_norm(x, g, lo):
    xx = x * x
    s_lo = jnp.sum(jnp.where(lo, xx, 0.0), axis=-1, keepdims=True)
    s_hi = jnp.sum(jnp.where(lo, 0.0, xx), axis=-1, keepdims=True)
    inv = jnp.where(lo, lax.rsqrt(s_lo * (1.0 / DA_HEAD_DIM) + EPS),
                    lax.rsqrt(s_hi * (1.0 / DA_HEAD_DIM) + EPS))
    return x * inv * g


def _attn_kernel(q_ref, k_ref, v_ref, qg_ref, kg_ref, lq1_ref, lk1_ref, lq2_ref, lk2_ref, sg_ref,
                 o_ref, kn_ref, vt_ref, qt_ref, m_ref, acc_ref, s0_ref, s1_ref, mx0_ref, mx1_ref,
                 *, tq, tk, cw, lambda_init):
    i = pl.program_id(1)
    s_len = k_ref.shape[0]
    lo = lax.broadcasted_iota(jnp.int32, (1, HEAD_W), 1) < DA_HEAD_DIM
    prep_rows = min(512, s_len)

    @pl.when(i == 0)
    def _():
        vt_ref[HEAD_W:ACC_ROWS, :] = jnp.ones((ACC_ROWS - HEAD_W, s_len), BF16)

        def body(c, carry):
            r = pl.multiple_of(c * prep_rows, prep_rows)
            kk = k_ref[pl.ds(r, prep_rows), :].astype(F32)
            kn_ref[pl.ds(r, prep_rows), :] = _half_norm(kk, kg_ref[...], lo).astype(BF16)
            vv = v_ref[pl.ds(r, prep_rows), :].astype(F32)
            vt_ref[0:HEAD_W, pl.ds(r, prep_rows)] = vv.T.astype(BF16)
            return carry
        lax.fori_loop(0, s_len // prep_rows, body, 0)

    qn = _half_norm(q_ref[...].astype(F32), qg_ref[...], lo) * (DA_SCALE * LOG2E)
    qt_ref[:, 0:tq] = jnp.where(lo, qn, 0.0).T.astype(BF16)
    qt_ref[:, tq:2 * tq] = jnp.where(lo, 0.0, qn).T.astype(BF16)
    m_ref[...] = jnp.full(m_ref.shape, NEG_BIG, F32)
    acc_ref[...] = jnp.zeros(acc_ref.shape, F32)

    groups = [(c * cw, (c + 1) * cw) for c in range(2 * tq // cw)]

    def softmax_update(a, b, s, s_max, vt):
        m_prev = m_ref[:, a:b]
        m_new = jnp.maximum(m_prev, s_max)
        alpha = jnp.exp2(m_prev - m_new)
        p = jnp.exp2((s - m_new).astype(BF16))
        acc_ref[:, a:b] = alpha * acc_ref[:, a:b] + jnp.dot(vt, p, preferred_element_type=F32)
        m_ref[:, a:b] = m_new

    def scores(jb, a, b, s_ref, mx_ref):
        r = pl.multiple_of(jb * tk, tk)
        s = jnp.dot(kn_ref[pl.ds(r, tk), :], qt_ref[:, a:b], preferred_element_type=F32)
        s_ref[:, a:b] = s
        mx_ref[:, a:b] = jnp.max(s, axis=0, keepdims=True)

    def update(jb, a, b, s_ref, mx_ref):
        r = pl.multiple_of(jb * tk, tk)
        softmax_update(a, b, s_ref[:, a:b], mx_ref[:, a:b], vt_ref[:, pl.ds(r, tk)])

    n_full = i * (tq // tk)
    for a, b in groups:
        scores(0, a, b, s0_ref, mx0_ref)

    def pair(t, carry):
        jb = 2 * t
        for a, b in groups:
            scores(jb + 1, a, b, s1_ref, mx1_ref)
            update(jb, a, b, s0_ref, mx0_ref)
        for a, b in groups:
            scores(jb + 2, a, b, s0_ref, mx0_ref)
            update(jb + 1, a, b, s1_ref, mx1_ref)
        return carry

    lax.fori_loop(0, n_full // 2, pair, 0)

    for d in range(tq // tk):
        r = pl.multiple_of((n_full + d) * tk, tk)
        for a, b in groups:
            q0 = a % tq
            k0, k1 = d * tk, min((d + 1) * tk, q0 + cw)
            if k1 <= k0:
                continue
            rows = k1 - k0
            if d == 0:
                s = s0_ref[0:rows, a:b]
            else:
                s = jnp.dot(kn_ref[pl.ds(r, rows), :], qt_ref[:, a:b], preferred_element_type=F32)
            if k1 - 1 > q0:
                key = k0 + lax.broadcasted_iota(jnp.int32, s.shape, 0)
                qry = q0 + lax.broadcasted_iota(jnp.int32, s.shape, 1)
                s = jnp.where(key <= qry, s, NEG_BIG)
            softmax_update(a, b, s, jnp.max(s, axis=0, keepdims=True), vt_ref[:, pl.ds(r, rows)])

    acc = acc_ref[...]
    o1 = acc[0:HEAD_W, 0:tq] / acc[HEAD_W:HEAD_W + 1, 0:tq]
    o2 = acc[0:HEAD_W, tq:2 * tq] / acc[HEAD_W:HEAD_W + 1, tq:2 * tq]
    lam = (jnp.exp(jnp.sum(lq1_ref[...] * lk1_ref[...], axis=-1, keepdims=True))
           - jnp.exp(jnp.sum(lq2_ref[...] * lk2_ref[...], axis=-1, keepdims=True)) + lambda_init)
    o = (o1 - lam * o2).T
    o = _rms(o, sg_ref[...]) * (1.0 - lambda_init)
    o_ref[...] = o.astype(BF16)


def _attn(z, qg, kg, lq1, lk1, lq2, lk2, sg, *, lambda_init, tq=1024, tk=512, cw=256):
    s = z.shape[0]
    assert s % tq == 0 and tq % (2 * tk) == 0 and tq % cw == 0
    small = lambda w: pl.BlockSpec((1, w), lambda h, i: (0, 0))
    kern = functools.partial(_attn_kernel, tq=tq, tk=tk, cw=cw, lambda_init=lambda_init)
    return pl.pallas_call(
        kern,
        out_shape=jax.ShapeDtypeStruct((s, DA_HEADS * HEAD_W), BF16),
        grid=(DA_HEADS, s // tq),
        in_specs=[
            pl.BlockSpec((tq, HEAD_W), lambda h, i: (i, QA_BLK + h)),
            pl.BlockSpec((s, HEAD_W), lambda h, i: (0, KA_BLK + h)),
            pl.BlockSpec((s, HEAD_W), lambda h, i: (0, VA_BLK + h)),
            small(HEAD_W), small(HEAD_W),
            small(DA_HEAD_DIM), small(DA_HEAD_DIM), small(DA_HEAD_DIM), small(DA_HEAD_DIM),
            small(HEAD_W),
        ],
        out_specs=pl.BlockSpec((tq, HEAD_W), lambda h, i: (i, h)),
        scratch_shapes=[
            pltpu.VMEM((s, HEAD_W), BF16),
            pltpu.VMEM((ACC_ROWS, s), BF16),
            pltpu.VMEM((HEAD_W, 2 * tq), BF16),
            pltpu.VMEM((1, 2 * tq), F32),
            pltpu.VMEM((ACC_ROWS, 2 * tq), F32),
            pltpu.VMEM((tk, 2 * tq), F32),
            pltpu.VMEM((tk, 2 * tq), F32),
            pltpu.VMEM((1, 2 * tq), F32),
            pltpu.VMEM((1, 2 * tq), F32),
        ],
        compiler_params=_params(("arbitrary", "arbitrary"), 48),
        name="diff_attn",
    )(z, z, z, qg, kg, lq1, lk1, lq2, lk2, sg)


def _rope_kernel(pos_ref, invf_ref, cos_ref, sin_ref):
    ang = pos_ref[...].astype(F32) * invf_ref[...]
    lo = lax.broadcasted_iota(jnp.int32, (1, HEAD_W), 1) < HEAD_W // 2
    cos_ref[...] = jnp.cos(ang)
    sin_ref[...] = jnp.where(lo, -1.0, 1.0) * jnp.sin(ang)


def _rope_tables(pos_col, invf, *, tm=1024):
    s = pos_col.shape[0]
    tm = min(tm, s)
    return pl.pallas_call(
        _rope_kernel,
        out_shape=(jax.ShapeDtypeStruct((s, HEAD_W), F32), jax.ShapeDtypeStruct((s, HEAD_W), F32)),
        grid=(s // tm,),
        in_specs=[pl.BlockSpec((tm, 1), lambda i: (i, 0)), pl.BlockSpec((1, HEAD_W), lambda i: (0, 0))],
        out_specs=(pl.BlockSpec((tm, HEAD_W), lambda i: (i, 0)), pl.BlockSpec((tm, HEAD_W), lambda i: (i, 0))),
        compiler_params=_params(("arbitrary",), 32),
        name="rope_tables",
    )(pos_col, invf)


def _ret_kernel(q_ref, k_ref, v_ref, gr_ref, cos_ref, sin_ref, dec_ref, zeta_ref, xi_ref, cd_ref, sg_ref,
                o_ref, st_ref):
    @pl.when(pl.program_id(1) == 0)
    def _():
        st_ref[...] = jnp.zeros(st_ref.shape, F32)

    cosf = cos_ref[...]
    sinf = sin_ref[...]
    q = q_ref[...].astype(F32)
    k = k_ref[...].astype(F32)
    q = q * cosf + pltpu.roll(q, HEAD_W // 2, 1) * sinf
    k = (k * cosf + pltpu.roll(k, HEAD_W // 2, 1) * sinf) * RET_SCALE
    v = v_ref[...]
    st = st_ref[...]

    sc = lax.dot_general(q.astype(BF16), k.astype(BF16), (((1,), (1,)), ((), ())),
                         preferred_element_type=F32) * dec_ref[...]
    o = jnp.dot(sc.astype(BF16), v, preferred_element_type=F32)
    o = o + jnp.dot((q * xi_ref[...]).astype(BF16), st.astype(BF16), preferred_element_type=F32)
    kz = (k * zeta_ref[...]).astype(BF16)
    kv = lax.dot_general(kz, v, (((0,), (0,)), ((), ())), preferred_element_type=F32)
    st_ref[...] = cd_ref[0:1, :] * st + kv

    g = gr_ref[...].astype(F32)
    o_ref[...] = (_rms(o, sg_ref[...]) * (g * jax.nn.sigmoid(g))).astype(BF16)


def _retention(z, cosf, sinf, dec, zeta, xi, cd, sg, *, chunk):
    s = z.shape[0]
    row = lambda blk: pl.BlockSpec((chunk, HEAD_W), lambda h, i: (i, blk + h))
    tab = pl.BlockSpec((chunk, HEAD_W), lambda h, i: (i, 0))
    per_head = lambda r, c: pl.BlockSpec((None, r, c), lambda h, i: (h, 0, 0))
    return pl.pallas_call(
        _ret_kernel,
        out_shape=jax.ShapeDtypeStruct((s, RET_HEADS * HEAD_W), BF16),
        grid=(RET_HEADS, s // chunk),
        in_specs=[
            row(QR_BLK), row(KR_BLK), row(VR_BLK), row(GR_BLK), tab, tab,
            per_head(chunk, chunk), per_head(chunk, HEAD_W), per_head(chunk, HEAD_W), per_head(8, HEAD_W),
            pl.BlockSpec((1, HEAD_W), lambda h, i: (0, 0)),
        ],
        out_specs=pl.BlockSpec((chunk, HEAD_W), lambda h, i: (i, h)),
        scratch_shapes=[pltpu.VMEM((HEAD_W, HEAD_W), F32)],
        compiler_params=_params(("arbitrary", "arbitrary"), 32),
        name="retention",
    )(z, z, z, z, cosf, sinf, dec, zeta, xi, cd, sg)


def _retention_tables(chunk):
    log_g = jnp.log(1.0 - 2.0 ** (-5.0 - jnp.arange(RET_HEADS, dtype=F32)))
    idx = jnp.arange(chunk, dtype=F32)
    rel = idx[:, None] - idx[None, :]
    dec = jnp.exp(log_g[:, None, None] * jnp.maximum(rel, 0.0)) * (rel >= 0)
    zeta = jnp.exp(log_g[:, None] * (chunk - 1 - idx))
    xi = jnp.exp(log_g[:, None] * (idx + 1.0))
    cd = jnp.exp(log_g * chunk)
    bcast = lambda t: jnp.broadcast_to(t[:, :, None], (RET_HEADS, chunk, HEAD_W))
    return dec, bcast(zeta), bcast(xi), jnp.broadcast_to(cd[:, None, None], (RET_HEADS, 8, HEAD_W))


def _merge_kernel(x_ref, ya_ref, yb_ref, ga0_ref, ga1_ref, gb0_ref, gb1_ref, wa_ref, wb_ref, wo_ref, o_ref):
    ta = jnp.dot(ya_ref[...], wa_ref[...], preferred_element_type=F32)
    tb = jnp.dot(yb_ref[...], wb_ref[...], preferred_element_type=F32)
    ga = jnp.concatenate([ga0_ref[...], ga1_ref[...]], axis=-1).astype(F32)
    gb = jnp.concatenate([gb0_ref[...], gb1_ref[...]], axis=-1).astype(F32)
    merged = jax.nn.sigmoid(ga) * ta + jax.nn.sigmoid(gb) * tb
    o_ref[...] = x_ref[...] + jnp.dot(merged.astype(BF16), wo_ref[...], preferred_element_type=F32)


def _merge(x, ya, yb, z, wa, wb, wo, *, tm=256):
    s, d = x.shape
    gate = lambda blk: pl.BlockSpec((tm, GATE_BLK_W), lambda i: (i, blk))
    resident = lambda r, c: pl.BlockSpec((r, c), lambda i: (0, 0), pipeline_mode=pl.Buffered(1))
    return pl.pallas_call(
        _merge_kernel,
        out_shape=jax.ShapeDtypeStruct((s, d), F32),
        grid=(s // tm,),
        in_specs=[
            pl.BlockSpec((tm, d), lambda i: (i, 0)),
            pl.BlockSpec((tm, ya.shape[1]), lambda i: (i, 0)),
            pl.BlockSpec((tm, yb.shape[1]), lambda i: (i, 0)),
            gate(GA_BLK), gate(GA_BLK + 1), gate(GB_BLK), gate(GB_BLK + 1),
            resident(*wa.shape), resident(*wb.shape), resident(*wo.shape),
        ],
        out_specs=pl.BlockSpec((tm, d), lambda i: (i, 0)),
        compiler_params=_params(("arbitrary",), 48),
        name="merge_out",
    )(x, ya, yb, z, z, z, z, wa, wb, wo)


def _ple_kernel(x_ref, g_ref, p_ref, wg_ref, wp_ref, o_ref):
    x = x_ref[...]
    h = _rms(x, g_ref[...]).astype(BF16)
    gate = jax.nn.sigmoid(jnp.dot(h, wg_ref[...], preferred_element_type=F32))
    proj = jnp.dot(p_ref[...].astype(BF16), wp_ref[...], preferred_element_type=F32)
    o_ref[...] = x + gate * proj


def _ple(x, g, p, wg, wp, *, tm=512):
    s, d = x.shape
    resident = lambda r, c: pl.BlockSpec((r, c), lambda i: (0, 0), pipeline_mode=pl.Buffered(1))
    return pl.pallas_call(
        _ple_kernel,
        out_shape=jax.ShapeDtypeStruct((s, d), F32),
        grid=(s // tm,),
        in_specs=[
            pl.BlockSpec((tm, d), lambda i: (i, 0)),
            pl.BlockSpec((1, d), lambda i: (0, 0)),
            pl.BlockSpec((tm, p.shape[1]), lambda i: (i, 0)),
            resident(*wg.shape), resident(*wp.shape),
        ],
        out_specs=pl.BlockSpec((tm, d), lambda i: (i, 0)),
        compiler_params=_params(("arbitrary",), 48),
        name="ple",
    )(x, g, p, wg, wp)


def kernel(x, p, positions, ffn1_norm, ffn1_w_gate, ffn1_w_up, ffn1_w_down, mix_norm, w_in, da_q_norm, da_k_norm, da_lambda_q1, da_lambda_k1, da_lambda_q2, da_lambda_k2, da_sub_norm, ret_sub_norm, w_up_a, w_up_b, w_out, ffn2_norm, ffn2_w_gate, ffn2_w_up, ffn2_w_down, ple_norm, w_ple_gate, w_ple_proj):
    b, s, d = x.shape
    assert b == 1
    depth = w_in.shape[0]
    chunk = min(512, s)

    row = lambda t: t.reshape(1, -1)
    twice = lambda t: jnp.concatenate([t, t]).reshape(1, -1)
    bf = lambda t: t.astype(BF16)

    inv_freq = ROPE_BASE ** (-jnp.arange(0, HEAD_W, 2, dtype=F32) / HEAD_W)
    cosf, sinf = _rope_tables(positions.reshape(s, 1), twice(inv_freq))
    dec, zeta, xi, cd = _retention_tables(chunk)

    xc = x.reshape(s, d)
    for i in range(depth):
        lambda_init = 0.8 - 0.6 * math.exp(-0.3 * i)
        xc = _ffn(xc, row(ffn1_norm[i]), bf(ffn1_w_gate[i]), bf(ffn1_w_up[i]), bf(ffn1_w_down[i]))
        z = _inproj(xc, row(mix_norm[i]), bf(w_in[i]))
        ya = _attn(z, twice(da_q_norm[i]), twice(da_k_norm[i]), row(da_lambda_q1[i]), row(da_lambda_k1[i]),
                   row(da_lambda_q2[i]), row(da_lambda_k2[i]), row(da_sub_norm[i]), lambda_init=lambda_init)
        yb = _retention(z, cosf, sinf, dec, zeta, xi, cd, row(ret_sub_norm[i]), chunk=chunk)
        xc = _merge(xc, ya, yb, z, bf(w_up_a[i]), bf(w_up_b[i]), bf(w_out[i]))
        xc = _ffn(xc, row(ffn2_norm[i]), bf(ffn2_w_gate[i]), bf(ffn2_w_up[i]), bf(ffn2_w_down[i]))
        xc = _ple(xc, row(ple_norm[i]), p[i].reshape(s, -1), bf(w_ple_gate[i]), bf(w_ple_proj[i]))
    return xc.reshape(b, s, d)
```

```python
import functools
import math

import jax
import jax.numpy as jnp
from jax import lax
from jax.experimental import pallas as pl
from jax.experimental.pallas import tpu as pltpu

F32 = jnp.float32
BF16 = jnp.bfloat16

EPS = 1e-6
ROPE_BASE = 10000.0
DA_HEADS = 8
DA_HEAD_DIM = 64
RET_HEADS = 8
HEAD_W = 128
RET_SCALE = HEAD_W ** -0.5
DA_SCALE = DA_HEAD_DIM ** -0.5
NEG_BIG = -1e30
LOG2E = math.log2(math.e)
ACC_ROWS = HEAD_W + 16

QA_BLK, KA_BLK, VA_BLK, QR_BLK, KR_BLK, VR_BLK, GR_BLK = 0, 8, 16, 24, 32, 40, 48
GATE_BLK_W = 1024
GA_BLK, GB_BLK = 7, 9

MIB = 1024 * 1024


def _params(sem, vmem_mib):
    return pltpu.CompilerParams(dimension_semantics=sem, vmem_limit_bytes=vmem_mib * MIB)


def _rms(x, g):
    return x * lax.rsqrt(jnp.mean(x * x, axis=-1, keepdims=True) + EPS) * g


def _ffn_kernel(x_ref, g_ref, wg_ref, wu_ref, wd_ref, o_ref, h_ref):
    j = pl.program_id(1)

    @pl.when(j == 0)
    def _():
        x = x_ref[...]
        h_ref[...] = _rms(x, g_ref[...]).astype(BF16)
        o_ref[...] = x

    h = h_ref[...]
    gate = jnp.dot(h, wg_ref[...], preferred_element_type=F32)
    up = jnp.dot(h, wu_ref[...], preferred_element_type=F32)
    act = (gate * jax.nn.sigmoid(gate) * up * 0.5).astype(BF16)
    o_ref[...] += jnp.dot(act, wd_ref[...], preferred_element_type=F32)


def _ffn(x, g, wg, wu, wd, *, tm=1024, tf=512):
    s, d = x.shape
    tm = min(tm, s)
    f = wg.shape[1]
    return pl.pallas_call(
        _ffn_kernel,
        out_shape=jax.ShapeDtypeStruct((s, d), F32),
        grid=(s // tm, f // tf),
        in_specs=[
            pl.BlockSpec((tm, d), lambda i, j: (i, 0)),
            pl.BlockSpec((1, d), lambda i, j: (0, 0)),
            pl.BlockSpec((d, tf), lambda i, j: (0, j)),
            pl.BlockSpec((d, tf), lambda i, j: (0, j)),
            pl.BlockSpec((tf, d), lambda i, j: (j, 0)),
        ],
        out_specs=pl.BlockSpec((tm, d), lambda i, j: (i, 0)),
        scratch_shapes=[pltpu.VMEM((tm, d), BF16)],
        compiler_params=_params(("arbitrary", "arbitrary"), 58),
        name="ffn",
    )(x, g, wg, wu, wd)


def _inproj_kernel(x_ref, g_ref, w_ref, z_ref, h_ref):
    @pl.when(pl.program_id(1) == 0)
    def _():
        h_ref[...] = _rms(x_ref[...], g_ref[...]).astype(BF16)

    z_ref[...] = jnp.dot(h_ref[...], w_ref[...], preferred_element_type=F32).astype(BF16)


def _inproj(x, g, w, *, tm=1024, tn=1024):
    s, d = x.shape
    tm = min(tm, s)
    n = w.shape[1]
    return pl.pallas_call(
        _inproj_kernel,
        out_shape=jax.ShapeDtypeStruct((s, n), BF16),
        grid=(s // tm, n // tn),
        in_specs=[
            pl.BlockSpec((tm, d), lambda i, j: (i, 0)),
            pl.BlockSpec((1, d), lambda i, j: (0, 0)),
            pl.BlockSpec((d, tn), lambda i, j: (0, j)),
        ],
        out_specs=pl.BlockSpec((tm, tn), lambda i, j: (i, j)),
        scratch_shapes=[pltpu.VMEM((tm, d), BF16)],
        compiler_params=_params(("arbitrary", "arbitrary"), 48),
        name="in_proj",
    )(x, g, w)


def _half_norm(x, g, lo):
    xx = x * x
    s_lo = jnp.sum(jnp.where(lo, xx, 0.0), axis=-1, keepdims=True)
    s_hi = jnp.sum(jnp.where(lo, 0.0, xx), axis=-1, keepdims=True)
    inv = jnp.where(lo, lax.rsqrt(s_lo * (1.0 / DA_HEAD_DIM) + EPS),
                    lax.rsqrt(s_hi * (1.0 / DA_HEAD_DIM) + EPS))
    return x * inv * g


def _attn_kernel(q_ref, k_ref, v_ref, qg_ref, kg_ref, lq1_ref, lk1_ref, lq2_ref, lk2_ref, sg_ref,
                 o_ref, kn_ref, vt_ref, qt_ref, m_ref, acc_ref, s0_ref, s1_ref, mx0_ref, mx1_ref,
                 *, tq, tk, cw, lambda_init):
    i = pl.program_id(1)
    s_len = k_ref.shape[0]
    lo = lax.broadcasted_iota(jnp.int32, (1, HEAD_W), 1) < DA_HEAD_DIM
    prep_rows = min(512, s_len)

    @pl.when(i == 0)
    def _():
        vt_ref[HEAD_W:ACC_ROWS, :] = jnp.ones((ACC_ROWS - HEAD_W, s_len), BF16)

        def body(c, carry):
            r = pl.multiple_of(c * prep_rows, prep_rows)
            kk = k_ref[pl.ds(r, prep_rows), :].astype(F32)
            kn_ref[pl.ds(r, prep_rows), :] = _half_norm(kk, kg_ref[...], lo).astype(BF16)
            vv = v_ref[pl.ds(r, prep_rows), :].astype(F32)
            vt_ref[0:HEAD_W, pl.ds(r, prep_rows)] = vv.T.astype(BF16)
            return carry
        lax.fori_loop(0, s_len // prep_rows, body, 0)

    dh = DA_HEAD_DIM
    qt = q_ref[...].astype(F32).T
    qq = qt * qt
    qscale = DA_SCALE * LOG2E
    inv1 = lax.rsqrt(jnp.sum(qq[0:dh], axis=0, keepdims=True) * (1.0 / dh) + EPS) * qscale
    inv2 = lax.rsqrt(jnp.sum(qq[dh:HEAD_W], axis=0, keepdims=True) * (1.0 / dh) + EPS) * qscale
    zeros = jnp.zeros((dh, tq), BF16)
    qt_ref[0:dh, 0:tq] = (qt[0:dh] * inv1 * qg_ref[0:dh, :]).astype(BF16)
    qt_ref[dh:HEAD_W, 0:tq] = zeros
    qt_ref[0:dh, tq:2 * tq] = zeros
    qt_ref[dh:HEAD_W, tq:2 * tq] = (qt[dh:HEAD_W] * inv2 * qg_ref[dh:HEAD_W, :]).astype(BF16)
    m_ref[...] = jnp.full(m_ref.shape, NEG_BIG, F32)
    acc_ref[...] = jnp.zeros(acc_ref.shape, F32)

    groups = [(c * cw, (c + 1) * cw) for c in range(2 * tq // cw)]

    def softmax_update(a, b, s, s_max, vt):
        m_prev = m_ref[:, a:b]
        m_new = jnp.maximum(m_prev, s_max)
        alpha = jnp.exp2(m_prev - m_new)
        p = jnp.exp2((s - m_new).astype(BF16))
        acc_ref[:, a:b] = alpha * acc_ref[:, a:b] + jnp.dot(vt, p, preferred_element_type=F32)
        m_ref[:, a:b] = m_new

    def scores(jb, a, b, s_ref, mx_ref):
        r = pl.multiple_of(jb * tk, tk)
        s = jnp.dot(kn_ref[pl.ds(r, tk), :], qt_ref[:, a:b], preferred_element_type=F32)
        s_ref[a // cw] = s
        mx_ref[:, a:b] = jnp.max(s, axis=0, keepdims=True)

    def update(jb, a, b, s_ref, mx_ref):
        r = pl.multiple_of(jb * tk, tk)
        softmax_update(a, b, s_ref[a // cw], mx_ref[:, a:b], vt_ref[:, pl.ds(r, tk)])

    n_full = i * (tq // tk)
    for a, b in groups:
        scores(0, a, b, s0_ref, mx0_ref)

    def pair(t, carry):
        jb = 2 * t
        for a, b in groups:
            scores(jb + 1, a, b, s1_ref, mx1_ref)
            update(jb, a, b, s0_ref, mx0_ref)
        for a, b in groups:
            scores(jb + 2, a, b, s0_ref, mx0_ref)
            update(jb + 1, a, b, s1_ref, mx1_ref)
        return carry

    lax.fori_loop(0, n_full // 2, pair, 0)

    for d in range(tq // tk):
        r = pl.multiple_of((n_full + d) * tk, tk)
        for a, b in groups:
            q0 = a % tq
            k0, k1 = d * tk, min((d + 1) * tk, q0 + cw)
            if k1 <= k0:
                continue
            rows = k1 - k0
            if d == 0:
                s = s0_ref[a // cw, 0:rows, :]
            else:
                s = jnp.dot(kn_ref[pl.ds(r, rows), :], qt_ref[:, a:b], preferred_element_type=F32)
            if k1 - 1 > q0:
                key = k0 + lax.broadcasted_iota(jnp.int32, s.shape, 0)
                qry = q0 + lax.broadcasted_iota(jnp.int32, s.shape, 1)
                s = jnp.where(key <= qry, s, NEG_BIG)
            softmax_update(a, b, s, jnp.max(s, axis=0, keepdims=True), vt_ref[:, pl.ds(r, rows)])

    o1 = acc_ref[0:HEAD_W, 0:tq] * (1.0 / acc_ref[HEAD_W:HEAD_W + 1, 0:tq])
    o2 = acc_ref[0:HEAD_W, tq:2 * tq] * (1.0 / acc_ref[HEAD_W:HEAD_W + 1, tq:2 * tq])
    lam = (jnp.exp(jnp.sum(lq1_ref[...] * lk1_ref[...], axis=-1, keepdims=True))
           - jnp.exp(jnp.sum(lq2_ref[...] * lk2_ref[...], axis=-1, keepdims=True)) + lambda_init)
    o = o1 - lam * o2
    inv = lax.rsqrt(jnp.mean(o * o, axis=0, keepdims=True) + EPS) * (1.0 - lambda_init)
    o_ref[...] = (o * inv * sg_ref[...]).T.astype(BF16)


def _attn(z, qg, kg, lq1, lk1, lq2, lk2, sg, *, lambda_init, tq=1024, tk=512, cw=256):
    s = z.shape[0]
    assert s % tq == 0 and tq % (2 * tk) == 0 and tq % cw == 0
    small = lambda w: pl.BlockSpec((1, w), lambda h, i: (0, 0))
    column = pl.BlockSpec((HEAD_W, 1), lambda h, i: (0, 0))
    kern = functools.partial(_attn_kernel, tq=tq, tk=tk, cw=cw, lambda_init=lambda_init)
    return pl.pallas_call(
        kern,
        out_shape=jax.ShapeDtypeStruct((s, DA_HEADS * HEAD_W), BF16),
        grid=(DA_HEADS, s // tq),
        in_specs=[
            pl.BlockSpec((tq, HEAD_W), lambda h, i: (i, QA_BLK + h)),
            pl.BlockSpec((s, HEAD_W), lambda h, i: (0, KA_BLK + h)),
            pl.BlockSpec((s, HEAD_W), lambda h, i: (0, VA_BLK + h)),
            column, small(HEAD_W),
            small(DA_HEAD_DIM), small(DA_HEAD_DIM), small(DA_HEAD_DIM), small(DA_HEAD_DIM),
            column,
        ],
        out_specs=pl.BlockSpec((tq, HEAD_W), lambda h, i: (i, h)),
        scratch_shapes=[
            pltpu.VMEM((s, HEAD_W), BF16),
            pltpu.VMEM((ACC_ROWS, s), BF16),
            pltpu.VMEM((HEAD_W, 2 * tq), BF16),
            pltpu.VMEM((1, 2 * tq), F32),
            pltpu.VMEM((ACC_ROWS, 2 * tq), F32),
            pltpu.VMEM((2 * tq // cw, tk, cw), F32),
            pltpu.VMEM((2 * tq // cw, tk, cw), F32),
            pltpu.VMEM((1, 2 * tq), F32),
            pltpu.VMEM((1, 2 * tq), F32),
        ],
        compiler_params=_params(("arbitrary", "arbitrary"), 48),
        name="diff_attn",
    )(z, z, z, qg, kg, lq1, lk1, lq2, lk2, sg)


def _rope_kernel(pos_ref, invf_ref, cos_ref, sin_ref):
    ang = pos_ref[...].astype(F32) * invf_ref[...]
    lo = lax.broadcasted_iota(jnp.int32, (1, HEAD_W), 1) < HEAD_W // 2
    cos_ref[...] = jnp.cos(ang)
    sin_ref[...] = jnp.where(lo, -1.0, 1.0) * jnp.sin(ang)


def _rope_tables(pos_col, invf, *, tm=1024):
    s = pos_col.shape[0]
    tm = min(tm, s)
    return pl.pallas_call(
        _rope_kernel,
        out_shape=(jax.ShapeDtypeStruct((s, HEAD_W), F32), jax.ShapeDtypeStruct((s, HEAD_W), F32)),
        grid=(s // tm,),
        in_specs=[pl.BlockSpec((tm, 1), lambda i: (i, 0)), pl.BlockSpec((1, HEAD_W), lambda i: (0, 0))],
        out_specs=(pl.BlockSpec((tm, HEAD_W), lambda i: (i, 0)), pl.BlockSpec((tm, HEAD_W), lambda i: (i, 0))),
        compiler_params=_params(("arbitrary",), 32),
        name="rope_tables",
    )(pos_col, invf)


def _ret_kernel(q_ref, k_ref, v_ref, gr_ref, cos_ref, sin_ref, dec_ref, zeta_ref, xi_ref, cd_ref, sg_ref,
                o_ref, st_ref):
    @pl.when(pl.program_id(1) == 0)
    def _():
        st_ref[...] = jnp.zeros(st_ref.shape, F32)

    cosf = cos_ref[...]
    sinf = sin_ref[...]
    q = q_ref[...].astype(F32)
    k = k_ref[...].astype(F32)
    q = q * cosf + pltpu.roll(q, HEAD_W // 2, 1) * sinf
    k = (k * cosf + pltpu.roll(k, HEAD_W // 2, 1) * sinf) * RET_SCALE
    v = v_ref[...]
    st = st_ref[...]

    sc = lax.dot_general(q.astype(BF16), k.astype(BF16), (((1,), (1,)), ((), ())),
                         preferred_element_type=F32) * dec_ref[...]
    o = jnp.dot(sc.astype(BF16), v, preferred_element_type=F32)
    o = o + jnp.dot((q * xi_ref[...]).astype(BF16), st.astype(BF16), preferred_element_type=F32)
    kz = (k * zeta_ref[...]).astype(BF16)
    kv = lax.dot_general(kz, v, (((0,), (0,)), ((), ())), preferred_element_type=F32)
    st_ref[...] = cd_ref[0:1, :] * st + kv

    g = gr_ref[...].astype(F32)
    o_ref[...] = (_rms(o, sg_ref[...]) * (g * jax.nn.sigmoid(g))).astype(BF16)


def _retention(z, cosf, sinf, dec, zeta, xi, cd, sg, *, chunk):
    s = z.shape[0]
    row = lambda blk: pl.BlockSpec((chunk, HEAD_W), lambda h, i: (i, blk + h))
    tab = pl.BlockSpec((chunk, HEAD_W), lambda h, i: (i, 0))
    per_head = lambda r, c: pl.BlockSpec((None, r, c), lambda h, i: (h, 0, 0))
    return pl.pallas_call(
        _ret_kernel,
        out_shape=jax.ShapeDtypeStruct((s, RET_HEADS * HEAD_W), BF16),
        grid=(RET_HEADS, s // chunk),
        in_specs=[
            row(QR_BLK), row(KR_BLK), row(VR_BLK), row(GR_BLK), tab, tab,
            per_head(chunk, chunk), per_head(chunk, HEAD_W), per_head(chunk, HEAD_W), per_head(8, HEAD_W),
            pl.BlockSpec((1, HEAD_W), lambda h, i: (0, 0)),
        ],
        out_specs=pl.BlockSpec((chunk, HEAD_W), lambda h, i: (i, h)),
        scratch_shapes=[pltpu.VMEM((HEAD_W, HEAD_W), F32)],
        compiler_params=_params(("arbitrary", "arbitrary"), 32),
        name="retention",
    )(z, z, z, z, cosf, sinf, dec, zeta, xi, cd, sg)


def _retention_tables(chunk):
    log_g = jnp.log(1.0 - 2.0 ** (-5.0 - jnp.arange(RET_HEADS, dtype=F32)))
    idx = jnp.arange(chunk, dtype=F32)
    rel = idx[:, None] - idx[None, :]
    dec = jnp.exp(log_g[:, None, None] * jnp.maximum(rel, 0.0)) * (rel >= 0)
    zeta = jnp.exp(log_g[:, None] * (chunk - 1 - idx))
    xi = jnp.exp(log_g[:, None] * (idx + 1.0))
    cd = jnp.exp(log_g * chunk)
    bcast = lambda t: jnp.broadcast_to(t[:, :, None], (RET_HEADS, chunk, HEAD_W))
    return dec, bcast(zeta), bcast(xi), jnp.broadcast_to(cd[:, None, None], (RET_HEADS, 8, HEAD_W))


def _merge_kernel(x_ref, ya_ref, yb_ref, ga0_ref, ga1_ref, gb0_ref, gb1_ref, wa_ref, wb_ref, wo_ref, o_ref):
    ta = jnp.dot(ya_ref[...], wa_ref[...], preferred_element_type=F32)
    tb = jnp.dot(yb_ref[...], wb_ref[...], preferred_element_type=F32)
    ga = jnp.concatenate([ga0_ref[...], ga1_ref[...]], axis=-1).astype(F32)
    gb = jnp.concatenate([gb0_ref[...], gb1_ref[...]], axis=-1).astype(F32)
    merged = jax.nn.sigmoid(ga) * ta + jax.nn.sigmoid(gb) * tb
    o_ref[...] = x_ref[...] + jnp.dot(merged.astype(BF16), wo_ref[...], preferred_element_type=F32)


def _merge(x, ya, yb, z, wa, wb, wo, *, tm=256):
    s, d = x.shape
    gate = lambda blk: pl.BlockSpec((tm, GATE_BLK_W), lambda i: (i, blk))
    resident = lambda r, c: pl.BlockSpec((r, c), lambda i: (0, 0), pipeline_mode=pl.Buffered(1))
    return pl.pallas_call(
        _merge_kernel,
        out_shape=jax.ShapeDtypeStruct((s, d), F32),
        grid=(s // tm,),
        in_specs=[
            pl.BlockSpec((tm, d), lambda i: (i, 0)),
            pl.BlockSpec((tm, ya.shape[1]), lambda i: (i, 0)),
            pl.BlockSpec((tm, yb.shape[1]), lambda i: (i, 0)),
            gate(GA_BLK), gate(GA_BLK + 1), gate(GB_BLK), gate(GB_BLK + 1),
            resident(*wa.shape), resident(*wb.shape), resident(*wo.shape),
        ],
        out_specs=pl.BlockSpec((tm, d), lambda i: (i, 0)),
        compiler_params=_params(("arbitrary",), 48),
        name="merge_out",
    )(x, ya, yb, z, z, z, z, wa, wb, wo)


def _ple_kernel(x_ref, g_ref, p_ref, wg_ref, wp_ref, o_ref):
    x = x_ref[...]
    h = _rms(x, g_ref[...]).astype(BF16)
    gate = jax.nn.sigmoid(jnp.dot(h, wg_ref[...], preferred_element_type=F32))
    proj = jnp.dot(p_ref[...].astype(BF16), wp_ref[...], preferred_element_type=F32)
    o_ref[...] = x + gate * proj


def _ple(x, g, p, wg, wp, *, tm=512):
    s, d = x.shape
    resident = lambda r, c: pl.BlockSpec((r, c), lambda i: (0, 0), pipeline_mode=pl.Buffered(1))
    return pl.pallas_call(
        _ple_kernel,
        out_shape=jax.ShapeDtypeStruct((s, d), F32),
        grid=(s // tm,),
        in_specs=[
            pl.BlockSpec((tm, d), lambda i: (i, 0)),
            pl.BlockSpec((1, d), lambda i: (0, 0)),
            pl.BlockSpec((tm, p.shape[1]), lambda i: (i, 0)),
            resident(*wg.shape), resident(*wp.shape),
        ],
        out_specs=pl.BlockSpec((tm, d), lambda i: (i, 0)),
        compiler_params=_params(("arbitrary",), 48),
        name="ple",
    )(x, g, p, wg, wp)


def kernel(x, p, positions, ffn1_norm, ffn1_w_gate, ffn1_w_up, ffn1_w_down, mix_norm, w_in, da_q_norm, da_k_norm, da_lambda_q1, da_lambda_k1, da_lambda_q2, da_lambda_k2, da_sub_norm, ret_sub_norm, w_up_a, w_up_b, w_out, ffn2_norm, ffn2_w_gate, ffn2_w_up, ffn2_w_down, ple_norm, w_ple_gate, w_ple_proj):
    b, s, d = x.shape
    assert b == 1
    depth = w_in.shape[0]
    chunk = min(512, s)

    row = lambda t: t.reshape(1, -1)
    twice = lambda t: jnp.concatenate([t, t]).reshape(1, -1)
    bf = lambda t: t.astype(BF16)

    inv_freq = ROPE_BASE ** (-jnp.arange(0, HEAD_W, 2, dtype=F32) / HEAD_W)
    cosf, sinf = _rope_tables(positions.reshape(s, 1), twice(inv_freq))
    dec, zeta, xi, cd = _retention_tables(chunk)

    xc = x.reshape(s, d)
    for i in range(depth):
        lambda_init = 0.8 - 0.6 * math.exp(-0.3 * i)
        xc = _ffn(xc, row(ffn1_norm[i]), bf(ffn1_w_gate[i]), bf(ffn1_w_up[i]), bf(ffn1_w_down[i]))
        z = _inproj(xc, row(mix_norm[i]), bf(w_in[i]))
        ya = _attn(z, twice(da_q_norm[i]).reshape(-1, 1), twice(da_k_norm[i]), row(da_lambda_q1[i]),
                   row(da_lambda_k1[i]), row(da_lambda_q2[i]), row(da_lambda_k2[i]),
                   da_sub_norm[i].reshape(-1, 1), lambda_init=lambda_init)
        yb = _retention(z, cosf, sinf, dec, zeta, xi, cd, row(ret_sub_norm[i]), chunk=chunk)
        xc = _merge(xc, ya, yb, z, bf(w_up_a[i]), bf(w_up_b[i]), bf(w_out[i]))
        xc = _ffn(xc, row(ffn2_norm[i]), bf(ffn2_w_gate[i]), bf(ffn2_w_up[i]), bf(ffn2_w_down[i]))
        xc = _ple(xc, row(ple_norm[i]), p[i].reshape(s, -1), bf(w_ple_gate[i]), bf(w_ple_proj[i]))
    return xc.reshape(b, s, d)
```

```python
import functools
import math

import jax
import jax.numpy as jnp
from jax import lax
from jax.experimental import pallas as pl
from jax.experimental.pallas import tpu as pltpu

F32 = jnp.float32
BF16 = jnp.bfloat16

EPS = 1e-6
ROPE_BASE = 10000.0
DA_HEADS = 8
DA_HEAD_DIM = 64
RET_HEADS = 8
HEAD_W = 128
RET_SCALE = HEAD_W ** -0.5
DA_SCALE = DA_HEAD_DIM ** -0.5
NEG_BIG = -1e30
LOG2E = math.log2(math.e)
ACC_ROWS = HEAD_W + 16

QA_BLK, KA_BLK, VA_BLK, QR_BLK, KR_BLK, VR_BLK, GR_BLK = 0, 8, 16, 24, 32, 40, 48
GATE_BLK_W = 1024
GA_BLK, GB_BLK = 7, 9

MIB = 1024 * 1024


def _params(sem, vmem_mib):
    return pltpu.CompilerParams(dimension_semantics=sem, vmem_limit_bytes=vmem_mib * MIB)


def _rms(x, g):
    return x * lax.rsqrt(jnp.mean(x * x, axis=-1, keepdims=True) + EPS) * g


def _ffn_kernel(x_ref, g_ref, wg_ref, wu_ref, wd_ref, o_ref, h_ref):
    j = pl.program_id(1)

    @pl.when(j == 0)
    def _():
        x = x_ref[...]
        h_ref[...] = _rms(x, g_ref[...]).astype(BF16)
        o_ref[...] = x

    h = h_ref[...]
    gate = jnp.dot(h, wg_ref[...], preferred_element_type=F32)
    up = jnp.dot(h, wu_ref[...], preferred_element_type=F32)
    act = (gate * jax.nn.sigmoid(gate) * up * 0.5).astype(BF16)
    o_ref[...] += jnp.dot(act, wd_ref[...], preferred_element_type=F32)


def _ffn(x, g, wg, wu, wd, layer, *, tm=1024, tf=512):
    s, d = x.shape
    tm = min(tm, s)
    f = wg.shape[2]
    return pl.pallas_call(
        _ffn_kernel,
        out_shape=jax.ShapeDtypeStruct((s, d), F32),
        grid=(s // tm, f // tf),
        in_specs=[
            pl.BlockSpec((tm, d), lambda i, j: (i, 0)),
            pl.BlockSpec((1, d), lambda i, j: (0, 0)),
            pl.BlockSpec((None, d, tf), lambda i, j: (layer, 0, j)),
            pl.BlockSpec((None, d, tf), lambda i, j: (layer, 0, j)),
            pl.BlockSpec((None, tf, d), lambda i, j: (layer, j, 0)),
        ],
        out_specs=pl.BlockSpec((tm, d), lambda i, j: (i, 0)),
        scratch_shapes=[pltpu.VMEM((tm, d), BF16)],
        compiler_params=_params(("arbitrary", "arbitrary"), 58),
        name="ffn",
    )(x, g, wg, wu, wd)


def _inproj_kernel(x_ref, g_ref, w_ref, z_ref, h_ref):
    @pl.when(pl.program_id(1) == 0)
    def _():
        h_ref[...] = _rms(x_ref[...], g_ref[...]).astype(BF16)

    z_ref[...] = jnp.dot(h_ref[...], w_ref[...], preferred_element_type=F32).astype(BF16)


def _inproj(x, g, w, layer, *, tm=1024, tn=1024):
    s, d = x.shape
    tm = min(tm, s)
    n = w.shape[2]
    return pl.pallas_call(
        _inproj_kernel,
        out_shape=jax.ShapeDtypeStruct((s, n), BF16),
        grid=(s // tm, n // tn),
        in_specs=[
            pl.BlockSpec((tm, d), lambda i, j: (i, 0)),
            pl.BlockSpec((1, d), lambda i, j: (0, 0)),
            pl.BlockSpec((None, d, tn), lambda i, j: (layer, 0, j)),
        ],
        out_specs=pl.BlockSpec((tm, tn), lambda i, j: (i, j)),
        scratch_shapes=[pltpu.VMEM((tm, d), BF16)],
        compiler_params=_params(("arbitrary", "arbitrary"), 48),
        name="in_proj",
    )(x, g, w)


def _half_norm(x, g, lo):
    xx = x * x
    s_lo = jnp.sum(jnp.where(lo, xx, 0.0), axis=-1, keepdims=True)
    s_hi = jnp.sum(jnp.where(lo, 0.0, xx), axis=-1, keepdims=True)
    inv = jnp.where(lo, lax.rsqrt(s_lo * (1.0 / DA_HEAD_DIM) + EPS),
                    lax.rsqrt(s_hi * (1.0 / DA_HEAD_DIM) + EPS))
    return x * inv * g


def _attn_kernel(q_ref, k_ref, v_ref, qg_ref, kg_ref, lq1_ref, lk1_ref, lq2_ref, lk2_ref, sg_ref,
                 o_ref, kn_ref, vt_ref, qt_ref, m_ref, acc_ref, s0_ref, s1_ref, mx0_ref, mx1_ref,
                 *, tq, tk, cw, unroll, lambda_init):
    i = pl.program_id(1)
    s_len = k_ref.shape[0]
    lo = lax.broadcasted_iota(jnp.int32, (1, HEAD_W), 1) < DA_HEAD_DIM
    prep_rows = min(512, s_len)

    @pl.when(i == 0)
    def _():
        vt_ref[HEAD_W:ACC_ROWS, :] = jnp.ones((ACC_ROWS - HEAD_W, s_len), BF16)

        def body(c, carry):
            r = pl.multiple_of(c * prep_rows, prep_rows)
            kk = k_ref[pl.ds(r, prep_rows), :].astype(F32)
            kn_ref[pl.ds(r, prep_rows), :] = _half_norm(kk, kg_ref[...], lo).astype(BF16)
            vv = v_ref[pl.ds(r, prep_rows), :].astype(F32)
            vt_ref[0:HEAD_W, pl.ds(r, prep_rows)] = vv.T.astype(BF16)
            return carry
        lax.fori_loop(0, s_len // prep_rows, body, 0)

    dh = DA_HEAD_DIM
    qt = q_ref[...].astype(F32).T
    qq = qt * qt
    qscale = DA_SCALE * LOG2E
    inv1 = lax.rsqrt(jnp.sum(qq[0:dh], axis=0, keepdims=True) * (1.0 / dh) + EPS) * qscale
    inv2 = lax.rsqrt(jnp.sum(qq[dh:HEAD_W], axis=0, keepdims=True) * (1.0 / dh) + EPS) * qscale
    zeros = jnp.zeros((dh, tq), BF16)
    qt_ref[0:dh, 0:tq] = (qt[0:dh] * inv1 * qg_ref[0:dh, :]).astype(BF16)
    qt_ref[dh:HEAD_W, 0:tq] = zeros
    qt_ref[0:dh, tq:2 * tq] = zeros
    qt_ref[dh:HEAD_W, tq:2 * tq] = (qt[dh:HEAD_W] * inv2 * qg_ref[dh:HEAD_W, :]).astype(BF16)
    m_ref[...] = jnp.full(m_ref.shape, NEG_BIG, F32)
    acc_ref[...] = jnp.zeros(acc_ref.shape, F32)

    groups = [(c * cw, (c + 1) * cw) for c in range(2 * tq // cw)]

    def softmax_update(a, b, s, s_max, vt):
        m_prev = m_ref[:, a:b]
        m_new = jnp.maximum(m_prev, s_max)
        alpha = jnp.exp2(m_prev - m_new)
        p = jnp.exp2((s - m_new).astype(BF16))
        acc_ref[:, a:b] = alpha * acc_ref[:, a:b] + jnp.dot(vt, p, preferred_element_type=F32)
        m_ref[:, a:b] = m_new

    def scores(jb, a, b, s_ref, mx_ref):
        r = pl.multiple_of(jb * tk, tk)
        s = jnp.dot(kn_ref[pl.ds(r, tk), :], qt_ref[:, a:b], preferred_element_type=F32)
        s_ref[a // cw] = s
        mx_ref[:, a:b] = jnp.max(s, axis=0, keepdims=True)

    def update(jb, a, b, s_ref, mx_ref):
        r = pl.multiple_of(jb * tk, tk)
        softmax_update(a, b, s_ref[a // cw], mx_ref[:, a:b], vt_ref[:, pl.ds(r, tk)])

    n_full = i * (tq // tk)
    for a, b in groups:
        scores(0, a, b, s0_ref, mx0_ref)

    def pair(jb):
        for a, b in groups:
            scores(jb + 1, a, b, s1_ref, mx1_ref)
            update(jb, a, b, s0_ref, mx0_ref)
        for a, b in groups:
            scores(jb + 2, a, b, s0_ref, mx0_ref)
            update(jb + 1, a, b, s1_ref, mx1_ref)

    def unrolled_pairs(t, carry):
        for u in range(unroll):
            pair(2 * unroll * t + 2 * u)
        return carry

    def one_pair(t, carry):
        pair(n_full - n_full % (2 * unroll) + 2 * t)
        return carry

    lax.fori_loop(0, n_full // (2 * unroll), unrolled_pairs, 0)
    lax.fori_loop(0, (n_full % (2 * unroll)) // 2, one_pair, 0)

    for d in range(tq // tk):
        r = pl.multiple_of((n_full + d) * tk, tk)
        for a, b in groups:
            q0 = a % tq
            k0, k1 = d * tk, min((d + 1) * tk, q0 + cw)
            if k1 <= k0:
                continue
            rows = k1 - k0
            if d == 0:
                s = s0_ref[a // cw, 0:rows, :]
            else:
                s = jnp.dot(kn_ref[pl.ds(r, rows), :], qt_ref[:, a:b], preferred_element_type=F32)
            if k1 - 1 > q0:
                key = k0 + lax.broadcasted_iota(jnp.int32, s.shape, 0)
                qry = q0 + lax.broadcasted_iota(jnp.int32, s.shape, 1)
                s = jnp.where(key <= qry, s, NEG_BIG)
            softmax_update(a, b, s, jnp.max(s, axis=0, keepdims=True), vt_ref[:, pl.ds(r, rows)])

    o1 = acc_ref[0:HEAD_W, 0:tq] * (1.0 / acc_ref[HEAD_W:HEAD_W + 1, 0:tq])
    o2 = acc_ref[0:HEAD_W, tq:2 * tq] * (1.0 / acc_ref[HEAD_W:HEAD_W + 1, tq:2 * tq])
    lam = (jnp.exp(jnp.sum(lq1_ref[...] * lk1_ref[...], axis=-1, keepdims=True))
           - jnp.exp(jnp.sum(lq2_ref[...] * lk2_ref[...], axis=-1, keepdims=True)) + lambda_init)
    o = o1 - lam * o2
    inv = lax.rsqrt(jnp.mean(o * o, axis=0, keepdims=True) + EPS) * (1.0 - lambda_init)
    o_ref[...] = (o * inv * sg_ref[...]).T.astype(BF16)


def _attn(z, qg, kg, lq1, lk1, lq2, lk2, sg, *, lambda_init, tq=1024, tk=512, cw=256, unroll=4):
    s = z.shape[0]
    assert s % tq == 0 and tq % (2 * tk) == 0 and tq % cw == 0
    small = lambda w: pl.BlockSpec((1, w), lambda h, i: (0, 0))
    column = pl.BlockSpec((HEAD_W, 1), lambda h, i: (0, 0))
    kern = functools.partial(_attn_kernel, tq=tq, tk=tk, cw=cw, unroll=unroll, lambda_init=lambda_init)
    return pl.pallas_call(
        kern,
        out_shape=jax.ShapeDtypeStruct((s, DA_HEADS * HEAD_W), BF16),
        grid=(DA_HEADS, s // tq),
        in_specs=[
            pl.BlockSpec((tq, HEAD_W), lambda h, i: (i, QA_BLK + h)),
            pl.BlockSpec((s, HEAD_W), lambda h, i: (0, KA_BLK + h)),
            pl.BlockSpec((s, HEAD_W), lambda h, i: (0, VA_BLK + h)),
            column, small(HEAD_W),
            small(DA_HEAD_DIM), small(DA_HEAD_DIM), small(DA_HEAD_DIM), small(DA_HEAD_DIM),
            column,
        ],
        out_specs=pl.BlockSpec((tq, HEAD_W), lambda h, i: (i, h)),
        scratch_shapes=[
            pltpu.VMEM((s, HEAD_W), BF16),
            pltpu.VMEM((ACC_ROWS, s), BF16),
            pltpu.VMEM((HEAD_W, 2 * tq), BF16),
            pltpu.VMEM((1, 2 * tq), F32),
            pltpu.VMEM((ACC_ROWS, 2 * tq), F32),
            pltpu.VMEM((2 * tq // cw, tk, cw), F32),
            pltpu.VMEM((2 * tq // cw, tk, cw), F32),
            pltpu.VMEM((1, 2 * tq), F32),
            pltpu.VMEM((1, 2 * tq), F32),
        ],
        compiler_params=_params(("arbitrary", "arbitrary"), 48),
        name="diff_attn",
    )(z, z, z, qg, kg, lq1, lk1, lq2, lk2, sg)


def _rope_kernel(pos_ref, invf_ref, cos_ref, sin_ref):
    ang = pos_ref[...].astype(F32) * invf_ref[...]
    lo = lax.broadcasted_iota(jnp.int32, (1, HEAD_W), 1) < HEAD_W // 2
    cos_ref[...] = jnp.cos(ang)
    sin_ref[...] = jnp.where(lo, -1.0, 1.0) * jnp.sin(ang)


def _rope_tables(pos_col, invf, *, tm=1024):
    s = pos_col.shape[0]
    tm = min(tm, s)
    return pl.pallas_call(
        _rope_kernel,
        out_shape=(jax.ShapeDtypeStruct((s, HEAD_W), F32), jax.ShapeDtypeStruct((s, HEAD_W), F32)),
        grid=(s // tm,),
        in_specs=[pl.BlockSpec((tm, 1), lambda i: (i, 0)), pl.BlockSpec((1, HEAD_W), lambda i: (0, 0))],
        out_specs=(pl.BlockSpec((tm, HEAD_W), lambda i: (i, 0)), pl.BlockSpec((tm, HEAD_W), lambda i: (i, 0))),
        compiler_params=_params(("arbitrary",), 32),
        name="rope_tables",
    )(pos_col, invf)


def _ret_kernel(q_ref, k_ref, v_ref, gr_ref, cos_ref, sin_ref, dec_ref, zeta_ref, xi_ref, cd_ref, sg_ref,
                o_ref, st_ref):
    @pl.when(pl.program_id(1) == 0)
    def _():
        st_ref[...] = jnp.zeros(st_ref.shape, F32)

    cosf = cos_ref[...]
    sinf = sin_ref[...]
    q = q_ref[...].astype(F32)
    k = k_ref[...].astype(F32)
    q = q * cosf + pltpu.roll(q, HEAD_W // 2, 1) * sinf
    k = (k * cosf + pltpu.roll(k, HEAD_W // 2, 1) * sinf) * RET_SCALE
    v = v_ref[...]
    st = st_ref[...]

    sc = lax.dot_general(q.astype(BF16), k.astype(BF16), (((1,), (1,)), ((), ())),
                         preferred_element_type=F32) * dec_ref[...]
    o = jnp.dot(sc.astype(BF16), v, preferred_element_type=F32)
    o = o + jnp.dot((q * xi_ref[...]).astype(BF16), st.astype(BF16), preferred_element_type=F32)
    kz = (k * zeta_ref[...]).astype(BF16)
    kv = lax.dot_general(kz, v, (((0,), (0,)), ((), ())), preferred_element_type=F32)
    st_ref[...] = cd_ref[0:1, :] * st + kv

    g = gr_ref[...].astype(F32)
    o_ref[...] = (_rms(o, sg_ref[...]) * (g * jax.nn.sigmoid(g))).astype(BF16)


def _retention(z, cosf, sinf, dec, zeta, xi, cd, sg, *, chunk):
    s = z.shape[0]
    row = lambda blk: pl.BlockSpec((chunk, HEAD_W), lambda h, i: (i, blk + h))
    tab = pl.BlockSpec((chunk, HEAD_W), lambda h, i: (i, 0))
    per_head = lambda r, c: pl.BlockSpec((None, r, c), lambda h, i: (h, 0, 0))
    return pl.pallas_call(
        _ret_kernel,
        out_shape=jax.ShapeDtypeStruct((s, RET_HEADS * HEAD_W), BF16),
        grid=(RET_HEADS, s // chunk),
        in_specs=[
            row(QR_BLK), row(KR_BLK), row(VR_BLK), row(GR_BLK), tab, tab,
            per_head(chunk, chunk), per_head(chunk, HEAD_W), per_head(chunk, HEAD_W), per_head(8, HEAD_W),
            pl.BlockSpec((1, HEAD_W), lambda h, i: (0, 0)),
        ],
        out_specs=pl.BlockSpec((chunk, HEAD_W), lambda h, i: (i, h)),
        scratch_shapes=[pltpu.VMEM((HEAD_W, HEAD_W), F32)],
        compiler_params=_params(("arbitrary", "arbitrary"), 32),
        name="retention",
    )(z, z, z, z, cosf, sinf, dec, zeta, xi, cd, sg)


def _retention_tables(chunk):
    log_g = jnp.log(1.0 - 2.0 ** (-5.0 - jnp.arange(RET_HEADS, dtype=F32)))
    idx = jnp.arange(chunk, dtype=F32)
    rel = idx[:, None] - idx[None, :]
    dec = jnp.exp(log_g[:, None, None] * jnp.maximum(rel, 0.0)) * (rel >= 0)
    zeta = jnp.exp(log_g[:, None] * (chunk - 1 - idx))
    xi = jnp.exp(log_g[:, None] * (idx + 1.0))
    cd = jnp.exp(log_g * chunk)
    bcast = lambda t: jnp.broadcast_to(t[:, :, None], (RET_HEADS, chunk, HEAD_W))
    return dec, bcast(zeta), bcast(xi), jnp.broadcast_to(cd[:, None, None], (RET_HEADS, 8, HEAD_W))


def _merge_kernel(x_ref, ya_ref, yb_ref, ga0_ref, ga1_ref, gb0_ref, gb1_ref, wa_ref, wb_ref, wo_ref, o_ref):
    ta = jnp.dot(ya_ref[...], wa_ref[...], preferred_element_type=F32)
    tb = jnp.dot(yb_ref[...], wb_ref[...], preferred_element_type=F32)
    ga = jnp.concatenate([ga0_ref[...], ga1_ref[...]], axis=-1).astype(F32)
    gb = jnp.concatenate([gb0_ref[...], gb1_ref[...]], axis=-1).astype(F32)
    merged = jax.nn.sigmoid(ga) * ta + jax.nn.sigmoid(gb) * tb
    o_ref[...] = x_ref[...] + jnp.dot(merged.astype(BF16), wo_ref[...], preferred_element_type=F32)


def _merge(x, ya, yb, z, wa, wb, wo, layer, *, tm=256):
    s, d = x.shape
    gate = lambda blk: pl.BlockSpec((tm, GATE_BLK_W), lambda i: (i, blk))
    resident = lambda w: pl.BlockSpec((None,) + w.shape[1:], lambda i: (layer, 0, 0), pipeline_mode=pl.Buffered(1))
    return pl.pallas_call(
        _merge_kernel,
        out_shape=jax.ShapeDtypeStruct((s, d), F32),
        grid=(s // tm,),
        in_specs=[
            pl.BlockSpec((tm, d), lambda i: (i, 0)),
            pl.BlockSpec((tm, ya.shape[1]), lambda i: (i, 0)),
            pl.BlockSpec((tm, yb.shape[1]), lambda i: (i, 0)),
            gate(GA_BLK), gate(GA_BLK + 1), gate(GB_BLK), gate(GB_BLK + 1),
            resident(wa), resident(wb), resident(wo),
        ],
        out_specs=pl.BlockSpec((tm, d), lambda i: (i, 0)),
        compiler_params=_params(("arbitrary",), 48),
        name="merge_out",
    )(x, ya, yb, z, z, z, z, wa, wb, wo)


def _ple_kernel(x_ref, g_ref, p_ref, wg_ref, wp_ref, o_ref):
    x = x_ref[...]
    h = _rms(x, g_ref[...]).astype(BF16)
    gate = jax.nn.sigmoid(jnp.dot(h, wg_ref[...], preferred_element_type=F32))
    proj = jnp.dot(p_ref[...].astype(BF16), wp_ref[...], preferred_element_type=F32)
    o_ref[...] = x + gate * proj


def _ple(x, g, p, wg, wp, layer, *, tm=512):
    s, d = x.shape
    resident = lambda w: pl.BlockSpec((None,) + w.shape[1:], lambda i: (layer, 0, 0), pipeline_mode=pl.Buffered(1))
    return pl.pallas_call(
        _ple_kernel,
        out_shape=jax.ShapeDtypeStruct((s, d), F32),
        grid=(s // tm,),
        in_specs=[
            pl.BlockSpec((tm, d), lambda i: (i, 0)),
            pl.BlockSpec((1, d), lambda i: (0, 0)),
            pl.BlockSpec((None, tm, p.shape[2]), lambda i: (layer, i, 0)),
            resident(wg), resident(wp),
        ],
        out_specs=pl.BlockSpec((tm, d), lambda i: (i, 0)),
        compiler_params=_params(("arbitrary",), 48),
        name="ple",
    )(x, g, p, wg, wp)


def kernel(x, p, positions, ffn1_norm, ffn1_w_gate, ffn1_w_up, ffn1_w_down, mix_norm, w_in, da_q_norm, da_k_norm, da_lambda_q1, da_lambda_k1, da_lambda_q2, da_lambda_k2, da_sub_norm, ret_sub_norm, w_up_a, w_up_b, w_out, ffn2_norm, ffn2_w_gate, ffn2_w_up, ffn2_w_down, ple_norm, w_ple_gate, w_ple_proj):
    b, s, d = x.shape
    assert b == 1
    depth = w_in.shape[0]
    chunk = min(512, s)

    row = lambda t: t.reshape(1, -1)
    twice = lambda t: jnp.concatenate([t, t]).reshape(1, -1)
    bf = lambda t: t.astype(BF16)

    inv_freq = ROPE_BASE ** (-jnp.arange(0, HEAD_W, 2, dtype=F32) / HEAD_W)
    cosf, sinf = _rope_tables(positions.reshape(s, 1), twice(inv_freq))
    dec, zeta, xi, cd = _retention_tables(chunk)

    f1g, f1u, f1d = bf(ffn1_w_gate), bf(ffn1_w_up), bf(ffn1_w_down)
    f2g, f2u, f2d = bf(ffn2_w_gate), bf(ffn2_w_up), bf(ffn2_w_down)
    win, wua, wub, wo = bf(w_in), bf(w_up_a), bf(w_up_b), bf(w_out)
    wpg, wpp = bf(w_ple_gate), bf(w_ple_proj)
    p3 = p.reshape(depth, s, -1)

    xc = x.reshape(s, d)
    for i in range(depth):
        lambda_init = 0.8 - 0.6 * math.exp(-0.3 * i)
        xc = _ffn(xc, row(ffn1_norm[i]), f1g, f1u, f1d, i)
        z = _inproj(xc, row(mix_norm[i]), win, i)
        ya = _attn(z, twice(da_q_norm[i]).reshape(-1, 1), twice(da_k_norm[i]), row(da_lambda_q1[i]),
                   row(da_lambda_k1[i]), row(da_lambda_q2[i]), row(da_lambda_k2[i]),
                   da_sub_norm[i].reshape(-1, 1), lambda_init=lambda_init)
        yb = _retention(z, cosf, sinf, dec, zeta, xi, cd, row(ret_sub_norm[i]), chunk=chunk)
        xc = _merge(xc, ya, yb, z, wua, wub, wo, i)
        xc = _ffn(xc, row(ffn2_norm[i]), f2g, f2u, f2d, i)
        xc = _ple(xc, row(ple_norm[i]), p3, wpg, wpp, i)
    return xc.reshape(b, s, d)
```

```python
import functools
import math

import jax
import jax.numpy as jnp
from jax import lax
from jax.experimental import pallas as pl
from jax.experimental.pallas import tpu as pltpu

F32 = jnp.float32
BF16 = jnp.bfloat16

EPS = 1e-6
ROPE_BASE = 10000.0
DA_HEADS = 8
DA_HEAD_DIM = 64
RET_HEADS = 8
HEAD_W = 128
RET_SCALE = HEAD_W ** -0.5
DA_SCALE = DA_HEAD_DIM ** -0.5
NEG_BIG = -1e30
LOG2E = math.log2(math.e)
ACC_ROWS = HEAD_W + 16

QA_BLK, KA_BLK, VA_BLK, QR_BLK, KR_BLK, VR_BLK, GR_BLK = 0, 8, 16, 24, 32, 40, 48
GATE_BLK_W = 1024
GA_BLK, GB_BLK = 7, 9

MIB = 1024 * 1024


def _params(sem, vmem_mib):
    return pltpu.CompilerParams(dimension_semantics=sem, vmem_limit_bytes=vmem_mib * MIB)


def _rms(x, g):
    return x * lax.rsqrt(jnp.mean(x * x, axis=-1, keepdims=True) + EPS) * g


def _ffn_kernel(x_ref, g_ref, wg_ref, wu_ref, wd_ref, o_ref, h_ref):
    j = pl.program_id(1)

    @pl.when(j == 0)
    def _():
        x = x_ref[...]
        h_ref[...] = _rms(x, g_ref[...]).astype(BF16)
        o_ref[...] = x

    h = h_ref[...]
    gate = jnp.dot(h, wg_ref[...], preferred_element_type=F32)
    up = jnp.dot(h, wu_ref[...], preferred_element_type=F32)
    act = (gate * jax.nn.sigmoid(gate) * up * 0.5).astype(BF16)
    o_ref[...] += jnp.dot(act, wd_ref[...], preferred_element_type=F32)


def _ffn(x, g, wg, wu, wd, layer, *, tm=1024, tf=512):
    s, d = x.shape
    tm = min(tm, s)
    f = wg.shape[2]
    return pl.pallas_call(
        _ffn_kernel,
        out_shape=jax.ShapeDtypeStruct((s, d), F32),
        grid=(s // tm, f // tf),
        in_specs=[
            pl.BlockSpec((tm, d), lambda i, j: (i, 0)),
            pl.BlockSpec((1, d), lambda i, j: (0, 0)),
            pl.BlockSpec((None, d, tf), lambda i, j: (layer, 0, j)),
            pl.BlockSpec((None, d, tf), lambda i, j: (layer, 0, j)),
            pl.BlockSpec((None, tf, d), lambda i, j: (layer, j, 0)),
        ],
        out_specs=pl.BlockSpec((tm, d), lambda i, j: (i, 0)),
        scratch_shapes=[pltpu.VMEM((tm, d), BF16)],
        compiler_params=_params(("arbitrary", "arbitrary"), 58),
        name="ffn",
    )(x, g, wg, wu, wd)


def _inproj_kernel(x_ref, g_ref, w_ref, z_ref, h_ref):
    @pl.when(pl.program_id(1) == 0)
    def _():
        h_ref[...] = _rms(x_ref[...], g_ref[...]).astype(BF16)

    z_ref[...] = jnp.dot(h_ref[...], w_ref[...], preferred_element_type=F32).astype(BF16)


def _inproj(x, g, w, layer, *, tm=1024, tn=1024):
    s, d = x.shape
    tm = min(tm, s)
    n = w.shape[2]
    return pl.pallas_call(
        _inproj_kernel,
        out_shape=jax.ShapeDtypeStruct((s, n), BF16),
        grid=(s // tm, n // tn),
        in_specs=[
            pl.BlockSpec((tm, d), lambda i, j: (i, 0)),
            pl.BlockSpec((1, d), lambda i, j: (0, 0)),
            pl.BlockSpec((None, d, tn), lambda i, j: (layer, 0, j)),
        ],
        out_specs=pl.BlockSpec((tm, tn), lambda i, j: (i, j)),
        scratch_shapes=[pltpu.VMEM((tm, d), BF16)],
        compiler_params=_params(("arbitrary", "arbitrary"), 48),
        name="in_proj",
    )(x, g, w)


def _half_norm(x, g, lo):
    xx = x * x
    s_lo = jnp.sum(jnp.where(lo, xx, 0.0), axis=-1, keepdims=True)
    s_hi = jnp.sum(jnp.where(lo, 0.0, xx), axis=-1, keepdims=True)
    inv = jnp.where(lo, lax.rsqrt(s_lo * (1.0 / DA_HEAD_DIM) + EPS),
                    lax.rsqrt(s_hi * (1.0 / DA_HEAD_DIM) + EPS))
    return x * inv * g


def _attn_kernel(q_ref, k_ref, v_ref, qg_ref, kg_ref, lq1_ref, lk1_ref, lq2_ref, lk2_ref, sg_ref,
                 o_ref, kn_ref, vt_ref, qt_ref, m_ref, acc_ref, s0_ref, s1_ref, mx0_ref, mx1_ref,
                 *, tq, tk, cw, unroll, lambda_init):
    i = pl.program_id(1)
    s_len = k_ref.shape[0]
    lo = lax.broadcasted_iota(jnp.int32, (1, HEAD_W), 1) < DA_HEAD_DIM
    prep_rows = min(512, s_len)

    @pl.when(i == 0)
    def _():
        vt_ref[HEAD_W:ACC_ROWS, :] = jnp.ones((ACC_ROWS - HEAD_W, s_len), BF16)

        def body(c, carry):
            r = pl.multiple_of(c * prep_rows, prep_rows)
            kk = k_ref[pl.ds(r, prep_rows), :].astype(F32)
            kn_ref[pl.ds(r, prep_rows), :] = _half_norm(kk, kg_ref[...], lo).astype(BF16)
            vv = v_ref[pl.ds(r, prep_rows), :].astype(F32)
            vt_ref[0:HEAD_W, pl.ds(r, prep_rows)] = vv.T.astype(BF16)
            return carry
        lax.fori_loop(0, s_len // prep_rows, body, 0)

    dh = DA_HEAD_DIM
    qt = q_ref[...].astype(F32).T
    qq = qt * qt
    qscale = DA_SCALE * LOG2E
    inv1 = lax.rsqrt(jnp.sum(qq[0:dh], axis=0, keepdims=True) * (1.0 / dh) + EPS) * qscale
    inv2 = lax.rsqrt(jnp.sum(qq[dh:HEAD_W], axis=0, keepdims=True) * (1.0 / dh) + EPS) * qscale
    zeros = jnp.zeros((dh, tq), BF16)
    qt_ref[0:dh, 0:tq] = (qt[0:dh] * inv1 * qg_ref[0:dh, :]).astype(BF16)
    qt_ref[dh:HEAD_W, 0:tq] = zeros
    qt_ref[0:dh, tq:2 * tq] = zeros
    qt_ref[dh:HEAD_W, tq:2 * tq] = (qt[dh:HEAD_W] * inv2 * qg_ref[dh:HEAD_W, :]).astype(BF16)
    m_ref[...] = jnp.full(m_ref.shape, NEG_BIG, F32)
    acc_ref[...] = jnp.zeros(acc_ref.shape, F32)

    groups = [(c * cw, (c + 1) * cw) for c in range(2 * tq // cw)]

    def softmax_update(a, b, s, s_max, vt):
        m_prev = m_ref[:, a:b]
        m_new = jnp.maximum(m_prev, s_max)
        alpha = jnp.exp2(m_prev - m_new)
        p = jnp.exp2((s - m_new).astype(BF16))
        acc_ref[:, a:b] = alpha * acc_ref[:, a:b] + jnp.dot(vt, p, preferred_element_type=F32)
        m_ref[:, a:b] = m_new

    def scores(jb, a, b, s_ref, mx_ref):
        r = pl.multiple_of(jb * tk, tk)
        s = jnp.dot(kn_ref[pl.ds(r, tk), :], qt_ref[:, a:b], preferred_element_type=F32)
        s_ref[a // cw] = s
        mx_ref[:, a:b] = jnp.max(s, axis=0, keepdims=True)

    def update(jb, a, b, s_ref, mx_ref):
        r = pl.multiple_of(jb * tk, tk)
        softmax_update(a, b, s_ref[a // cw], mx_ref[:, a:b], vt_ref[:, pl.ds(r, tk)])

    n_full = i * (tq // tk)
    for a, b in groups:
        scores(0, a, b, s0_ref, mx0_ref)

    def pair(jb):
        for a, b in groups:
            scores(jb + 1, a, b, s1_ref, mx1_ref)
            update(jb, a, b, s0_ref, mx0_ref)
        for a, b in groups:
            scores(jb + 2, a, b, s0_ref, mx0_ref)
            update(jb + 1, a, b, s1_ref, mx1_ref)

    def unrolled_pairs(t, carry):
        for u in range(unroll):
            pair(2 * unroll * t + 2 * u)
        return carry

    def one_pair(t, carry):
        pair(n_full - n_full % (2 * unroll) + 2 * t)
        return carry

    lax.fori_loop(0, n_full // (2 * unroll), unrolled_pairs, 0)
    lax.fori_loop(0, (n_full % (2 * unroll)) // 2, one_pair, 0)

    for d in range(tq // tk):
        r = pl.multiple_of((n_full + d) * tk, tk)
        for a, b in groups:
            q0 = a % tq
            k0, k1 = d * tk, min((d + 1) * tk, q0 + cw)
            if k1 <= k0:
                continue
            rows = k1 - k0
            if d == 0:
                s = s0_ref[a // cw, 0:rows, :]
            else:
                s = jnp.dot(kn_ref[pl.ds(r, rows), :], qt_ref[:, a:b], preferred_element_type=F32)
            if k1 - 1 > q0:
                key = k0 + lax.broadcasted_iota(jnp.int32, s.shape, 0)
                qry = q0 + lax.broadcasted_iota(jnp.int32, s.shape, 1)
                s = jnp.where(key <= qry, s, NEG_BIG)
            softmax_update(a, b, s, jnp.max(s, axis=0, keepdims=True), vt_ref[:, pl.ds(r, rows)])

    o1 = acc_ref[0:HEAD_W, 0:tq] * (1.0 / acc_ref[HEAD_W:HEAD_W + 1, 0:tq])
    o2 = acc_ref[0:HEAD_W, tq:2 * tq] * (1.0 / acc_ref[HEAD_W:HEAD_W + 1, tq:2 * tq])
    lam = (jnp.exp(jnp.sum(lq1_ref[...] * lk1_ref[...], axis=-1, keepdims=True))
           - jnp.exp(jnp.sum(lq2_ref[...] * lk2_ref[...], axis=-1, keepdims=True)) + lambda_init)
    o = o1 - lam * o2
    inv = lax.rsqrt(jnp.mean(o * o, axis=0, keepdims=True) + EPS) * (1.0 - lambda_init)
    o_ref[...] = (o * inv * sg_ref[...]).T.astype(BF16)


def _attn(z, qg, kg, lq1, lk1, lq2, lk2, sg, *, lambda_init, tq=1024, tk=512, cw=256, unroll=4):
    s = z.shape[0]
    assert s % tq == 0 and tq % (2 * tk) == 0 and tq % cw == 0
    small = lambda w: pl.BlockSpec((1, w), lambda h, i: (0, 0))
    column = pl.BlockSpec((HEAD_W, 1), lambda h, i: (0, 0))
    kern = functools.partial(_attn_kernel, tq=tq, tk=tk, cw=cw, unroll=unroll, lambda_init=lambda_init)
    return pl.pallas_call(
        kern,
        out_shape=jax.ShapeDtypeStruct((s, DA_HEADS * HEAD_W), BF16),
        grid=(DA_HEADS, s // tq),
        in_specs=[
            pl.BlockSpec((tq, HEAD_W), lambda h, i: (i, QA_BLK + h)),
            pl.BlockSpec((s, HEAD_W), lambda h, i: (0, KA_BLK + h)),
            pl.BlockSpec((s, HEAD_W), lambda h, i: (0, VA_BLK + h)),
            column, small(HEAD_W),
            small(DA_HEAD_DIM), small(DA_HEAD_DIM), small(DA_HEAD_DIM), small(DA_HEAD_DIM),
            column,
        ],
        out_specs=pl.BlockSpec((tq, HEAD_W), lambda h, i: (i, h)),
        scratch_shapes=[
            pltpu.VMEM((s, HEAD_W), BF16),
            pltpu.VMEM((ACC_ROWS, s), BF16),
            pltpu.VMEM((HEAD_W, 2 * tq), BF16),
            pltpu.VMEM((1, 2 * tq), F32),
            pltpu.VMEM((ACC_ROWS, 2 * tq), F32),
            pltpu.VMEM((2 * tq // cw, tk, cw), F32),
            pltpu.VMEM((2 * tq // cw, tk, cw), F32),
            pltpu.VMEM((1, 2 * tq), F32),
            pltpu.VMEM((1, 2 * tq), F32),
        ],
        compiler_params=_params(("arbitrary", "arbitrary"), 48),
        name="diff_attn",
    )(z, z, z, qg, kg, lq1, lk1, lq2, lk2, sg)


def _rope_kernel(pos_ref, invf_ref, cos_ref, sin_ref):
    ang = pos_ref[...].astype(F32) * invf_ref[...]
    lo = lax.broadcasted_iota(jnp.int32, (1, HEAD_W), 1) < HEAD_W // 2
    cos_ref[...] = jnp.cos(ang)
    sin_ref[...] = jnp.where(lo, -1.0, 1.0) * jnp.sin(ang)


def _rope_tables(pos_col, invf, *, tm=1024):
    s = pos_col.shape[0]
    tm = min(tm, s)
    return pl.pallas_call(
        _rope_kernel,
        out_shape=(jax.ShapeDtypeStruct((s, HEAD_W), F32), jax.ShapeDtypeStruct((s, HEAD_W), F32)),
        grid=(s // tm,),
        in_specs=[pl.BlockSpec((tm, 1), lambda i: (i, 0)), pl.BlockSpec((1, HEAD_W), lambda i: (0, 0))],
        out_specs=(pl.BlockSpec((tm, HEAD_W), lambda i: (i, 0)), pl.BlockSpec((tm, HEAD_W), lambda i: (i, 0))),
        compiler_params=_params(("arbitrary",), 32),
        name="rope_tables",
    )(pos_col, invf)


def _ret_kernel(q_ref, k_ref, v_ref, gr_ref, cos_ref, sin_ref, dec_ref, zeta_ref, xi_ref, cd_ref, sg_ref,
                o_ref, st_ref, *, heads):
    @pl.when(pl.program_id(1) == 0)
    def _():
        st_ref[...] = jnp.zeros(st_ref.shape, F32)

    cosf = cos_ref[...]
    sinf = sin_ref[...]
    for hh in range(heads):
        lanes = slice(hh * HEAD_W, (hh + 1) * HEAD_W)
        q = q_ref[:, lanes].astype(F32)
        k = k_ref[:, lanes].astype(F32)
        q = q * cosf + pltpu.roll(q, HEAD_W // 2, 1) * sinf
        k = (k * cosf + pltpu.roll(k, HEAD_W // 2, 1) * sinf) * RET_SCALE
        v = v_ref[:, lanes]
        st = st_ref[hh]

        sc = lax.dot_general(q.astype(BF16), k.astype(BF16), (((1,), (1,)), ((), ())),
                             preferred_element_type=F32) * dec_ref[hh]
        o = jnp.dot(sc.astype(BF16), v, preferred_element_type=F32)
        o = o + jnp.dot((q * xi_ref[hh]).astype(BF16), st.astype(BF16), preferred_element_type=F32)
        kz = (k * zeta_ref[hh]).astype(BF16)
        kv = lax.dot_general(kz, v, (((0,), (0,)), ((), ())), preferred_element_type=F32)
        st_ref[hh] = cd_ref[hh, 0:1, :] * st + kv

        g = gr_ref[:, lanes].astype(F32)
        o_ref[:, lanes] = (_rms(o, sg_ref[...]) * (g * jax.nn.sigmoid(g))).astype(BF16)


def _retention(z, cosf, sinf, dec, zeta, xi, cd, sg, *, chunk, heads=8):
    s = z.shape[0]
    w = heads * HEAD_W
    row = lambda blk: pl.BlockSpec((chunk, w), lambda h, i: (i, blk // heads + h))
    tab = pl.BlockSpec((chunk, HEAD_W), lambda h, i: (i, 0))
    per_head = lambda r, c: pl.BlockSpec((heads, r, c), lambda h, i: (h, 0, 0))
    return pl.pallas_call(
        functools.partial(_ret_kernel, heads=heads),
        out_shape=jax.ShapeDtypeStruct((s, RET_HEADS * HEAD_W), BF16),
        grid=(RET_HEADS // heads, s // chunk),
        in_specs=[
            row(QR_BLK), row(KR_BLK), row(VR_BLK), row(GR_BLK), tab, tab,
            per_head(chunk, chunk), per_head(chunk, HEAD_W), per_head(chunk, HEAD_W), per_head(8, HEAD_W),
            pl.BlockSpec((1, HEAD_W), lambda h, i: (0, 0)),
        ],
        out_specs=pl.BlockSpec((chunk, w), lambda h, i: (i, h)),
        scratch_shapes=[pltpu.VMEM((heads, HEAD_W, HEAD_W), F32)],
        compiler_params=_params(("arbitrary", "arbitrary"), 40),
        name="retention",
    )(z, z, z, z, cosf, sinf, dec, zeta, xi, cd, sg)


def _retention_tables(chunk):
    log_g = jnp.log(1.0 - 2.0 ** (-5.0 - jnp.arange(RET_HEADS, dtype=F32)))
    idx = jnp.arange(chunk, dtype=F32)
    rel = idx[:, None] - idx[None, :]
    dec = jnp.exp(log_g[:, None, None] * jnp.maximum(rel, 0.0)) * (rel >= 0)
    zeta = jnp.exp(log_g[:, None] * (chunk - 1 - idx))
    xi = jnp.exp(log_g[:, None] * (idx + 1.0))
    cd = jnp.exp(log_g * chunk)
    bcast = lambda t: jnp.broadcast_to(t[:, :, None], (RET_HEADS, chunk, HEAD_W))
    return dec, bcast(zeta), bcast(xi), jnp.broadcast_to(cd[:, None, None], (RET_HEADS, 8, HEAD_W))


def _merge_kernel(x_ref, ya_ref, yb_ref, ga0_ref, ga1_ref, gb0_ref, gb1_ref, wa_ref, wb_ref, wo_ref, o_ref):
    ta = jnp.dot(ya_ref[...], wa_ref[...], preferred_element_type=F32)
    tb = jnp.dot(yb_ref[...], wb_ref[...], preferred_element_type=F32)
    ga = jnp.concatenate([ga0_ref[...], ga1_ref[...]], axis=-1).astype(F32)
    gb = jnp.concatenate([gb0_ref[...], gb1_ref[...]], axis=-1).astype(F32)
    merged = jax.nn.sigmoid(ga) * ta + jax.nn.sigmoid(gb) * tb
    o_ref[...] = x_ref[...] + jnp.dot(merged.astype(BF16), wo_ref[...], preferred_element_type=F32)


def _merge(x, ya, yb, z, wa, wb, wo, layer, *, tm=256):
    s, d = x.shape
    gate = lambda blk: pl.BlockSpec((tm, GATE_BLK_W), lambda i: (i, blk))
    resident = lambda w: pl.BlockSpec((None,) + w.shape[1:], lambda i: (layer, 0, 0), pipeline_mode=pl.Buffered(1))
    return pl.pallas_call(
        _merge_kernel,
        out_shape=jax.ShapeDtypeStruct((s, d), F32),
        grid=(s // tm,),
        in_specs=[
            pl.BlockSpec((tm, d), lambda i: (i, 0)),
            pl.BlockSpec((tm, ya.shape[1]), lambda i: (i, 0)),
            pl.BlockSpec((tm, yb.shape[1]), lambda i: (i, 0)),
            gate(GA_BLK), gate(GA_BLK + 1), gate(GB_BLK), gate(GB_BLK + 1),
            resident(wa), resident(wb), resident(wo),
        ],
        out_specs=pl.BlockSpec((tm, d), lambda i: (i, 0)),
        compiler_params=_params(("arbitrary",), 48),
        name="merge_out",
    )(x, ya, yb, z, z, z, z, wa, wb, wo)


def _ple_kernel(x_ref, g_ref, p_ref, wg_ref, wp_ref, o_ref):
    x = x_ref[...]
    h = _rms(x, g_ref[...]).astype(BF16)
    gate = jax.nn.sigmoid(jnp.dot(h, wg_ref[...], preferred_element_type=F32))
    proj = jnp.dot(p_ref[...].astype(BF16), wp_ref[...], preferred_element_type=F32)
    o_ref[...] = x + gate * proj


def _ple(x, g, p, wg, wp, layer, *, tm=512):
    s, d = x.shape
    resident = lambda w: pl.BlockSpec((None,) + w.shape[1:], lambda i: (layer, 0, 0), pipeline_mode=pl.Buffered(1))
    return pl.pallas_call(
        _ple_kernel,
        out_shape=jax.ShapeDtypeStruct((s, d), F32),
        grid=(s // tm,),
        in_specs=[
            pl.BlockSpec((tm, d), lambda i: (i, 0)),
            pl.BlockSpec((1, d), lambda i: (0, 0)),
            pl.BlockSpec((None, tm, p.shape[2]), lambda i: (layer, i, 0)),
            resident(wg), resident(wp),
        ],
        out_specs=pl.BlockSpec((tm, d), lambda i: (i, 0)),
        compiler_params=_params(("arbitrary",), 48),
        name="ple",
    )(x, g, p, wg, wp)


def kernel(x, p, positions, ffn1_norm, ffn1_w_gate, ffn1_w_up, ffn1_w_down, mix_norm, w_in, da_q_norm, da_k_norm, da_lambda_q1, da_lambda_k1, da_lambda_q2, da_lambda_k2, da_sub_norm, ret_sub_norm, w_up_a, w_up_b, w_out, ffn2_norm, ffn2_w_gate, ffn2_w_up, ffn2_w_down, ple_norm, w_ple_gate, w_ple_proj):
    b, s, d = x.shape
    assert b == 1
    depth = w_in.shape[0]
    chunk = min(256, s)

    row = lambda t: t.reshape(1, -1)
    twice = lambda t: jnp.concatenate([t, t]).reshape(1, -1)
    bf = lambda t: t.astype(BF16)

    inv_freq = ROPE_BASE ** (-jnp.arange(0, HEAD_W, 2, dtype=F32) / HEAD_W)
    cosf, sinf = _rope_tables(positions.reshape(s, 1), twice(inv_freq))
    dec, zeta, xi, cd = _retention_tables(chunk)

    f1g, f1u, f1d = bf(ffn1_w_gate), bf(ffn1_w_up), bf(ffn1_w_down)
    f2g, f2u, f2d = bf(ffn2_w_gate), bf(ffn2_w_up), bf(ffn2_w_down)
    win, wua, wub, wo = bf(w_in), bf(w_up_a), bf(w_up_b), bf(w_out)
    wpg, wpp = bf(w_ple_gate), bf(w_ple_proj)
    p3 = p.reshape(depth, s, -1)

    xc = x.reshape(s, d)
    for i in range(depth):
        lambda_init = 0.8 - 0.6 * math.exp(-0.3 * i)
        xc = _ffn(xc, row(ffn1_norm[i]), f1g, f1u, f1d, i)
        z = _inproj(xc, row(mix_norm[i]), win, i)
        ya = _attn(z, twice(da_q_norm[i]).reshape(-1, 1), twice(da_k_norm[i]), row(da_lambda_q1[i]),
                   row(da_lambda_k1[i]), row(da_lambda_q2[i]), row(da_lambda_k2[i]),
                   da_sub_norm[i].reshape(-1, 1), lambda_init=lambda_init)
        yb = _retention(z, cosf, sinf, dec, zeta, xi, cd, row(ret_sub_norm[i]), chunk=chunk)
        xc = _merge(xc, ya, yb, z, wua, wub, wo, i)
        xc = _ffn(xc, row(ffn2_norm[i]), f2g, f2u, f2d, i)
        xc = _ple(xc, row(ple_norm[i]), p3, wpg, wpp, i)
    return xc.reshape(b, s, d)
```

```python
import functools
import math

import jax
import jax.numpy as jnp
from jax import lax
from jax.experimental import pallas as pl
from jax.experimental.pallas import tpu as pltpu

F32 = jnp.float32
BF16 = jnp.bfloat16

EPS = 1e-6
ROPE_BASE = 10000.0
DA_HEADS = 8
DA_HEAD_DIM = 64
RET_HEADS = 8
HEAD_W = 128
RET_SCALE = HEAD_W ** -0.5
DA_SCALE = DA_HEAD_DIM ** -0.5
NEG_BIG = -1e30
LOG2E = math.log2(math.e)
ACC_ROWS = HEAD_W + 16

QA_BLK, KA_BLK, VA_BLK, QR_BLK, KR_BLK, VR_BLK, GR_BLK = 0, 8, 16, 24, 32, 40, 48
GATE_BLK_W = 1024
GA_BLK, GB_BLK = 7, 9

MIB = 1024 * 1024


def _params(sem, vmem_mib):
    return pltpu.CompilerParams(dimension_semantics=sem, vmem_limit_bytes=vmem_mib * MIB)


def _rms(x, g):
    return x * lax.rsqrt(jnp.mean(x * x, axis=-1, keepdims=True) + EPS) * g


def _ffn_kernel(x_ref, g_ref, wg_ref, wu_ref, wd_ref, o_ref, h_ref):
    j = pl.program_id(1)

    @pl.when(j == 0)
    def _():
        x = x_ref[...]
        h_ref[...] = _rms(x, g_ref[...]).astype(BF16)
        o_ref[...] = x

    h = h_ref[...]
    gate = jnp.dot(h, wg_ref[...], preferred_element_type=F32)
    up = jnp.dot(h, wu_ref[...], preferred_element_type=F32)
    act = (gate * jax.nn.sigmoid(gate) * up * 0.5).astype(BF16)
    o_ref[...] += jnp.dot(act, wd_ref[...], preferred_element_type=F32)


def _ffn(x, g, wg, wu, wd, layer, *, tm=1024, tf=512):
    s, d = x.shape
    tm = min(tm, s)
    f = wg.shape[2]
    return pl.pallas_call(
        _ffn_kernel,
        out_shape=jax.ShapeDtypeStruct((s, d), F32),
        grid=(s // tm, f // tf),
        in_specs=[
            pl.BlockSpec((tm, d), lambda i, j: (i, 0)),
            pl.BlockSpec((1, d), lambda i, j: (0, 0)),
            pl.BlockSpec((None, d, tf), lambda i, j: (layer, 0, j)),
            pl.BlockSpec((None, d, tf), lambda i, j: (layer, 0, j)),
            pl.BlockSpec((None, tf, d), lambda i, j: (layer, j, 0)),
        ],
        out_specs=pl.BlockSpec((tm, d), lambda i, j: (i, 0)),
        scratch_shapes=[pltpu.VMEM((tm, d), BF16)],
        compiler_params=_params(("arbitrary", "arbitrary"), 58),
        name="ffn",
    )(x, g, wg, wu, wd)


def _inproj_kernel(x_ref, g_ref, w_ref, z_ref, h_ref):
    @pl.when(pl.program_id(1) == 0)
    def _():
        h_ref[...] = _rms(x_ref[...], g_ref[...]).astype(BF16)

    z_ref[...] = jnp.dot(h_ref[...], w_ref[...], preferred_element_type=F32).astype(BF16)


def _inproj(x, g, w, layer, *, tm=1024, tn=1024):
    s, d = x.shape
    tm = min(tm, s)
    n = w.shape[2]
    return pl.pallas_call(
        _inproj_kernel,
        out_shape=jax.ShapeDtypeStruct((s, n), BF16),
        grid=(s // tm, n // tn),
        in_specs=[
            pl.BlockSpec((tm, d), lambda i, j: (i, 0)),
            pl.BlockSpec((1, d), lambda i, j: (0, 0)),
            pl.BlockSpec((None, d, tn), lambda i, j: (layer, 0, j)),
        ],
        out_specs=pl.BlockSpec((tm, tn), lambda i, j: (i, j)),
        scratch_shapes=[pltpu.VMEM((tm, d), BF16)],
        compiler_params=_params(("arbitrary", "arbitrary"), 48),
        name="in_proj",
    )(x, g, w)


def _half_norm(x, g, lo):
    xx = x * x
    s_lo = jnp.sum(jnp.where(lo, xx, 0.0), axis=-1, keepdims=True)
    s_hi = jnp.sum(jnp.where(lo, 0.0, xx), axis=-1, keepdims=True)
    inv = jnp.where(lo, lax.rsqrt(s_lo * (1.0 / DA_HEAD_DIM) + EPS),
                    lax.rsqrt(s_hi * (1.0 / DA_HEAD_DIM) + EPS))
    return x * inv * g


def _attn_kernel(*refs, n_cast, tq, tk, cw, unroll, lambda_init):
    q_ref, k_ref, v_ref, qg_ref, kg_ref, lq1_ref, lk1_ref, lq2_ref, lk2_ref, sg_ref = refs[:10]
    w_refs = refs[10:10 + n_cast]
    o_ref = refs[10 + n_cast]
    c_refs = refs[11 + n_cast:11 + 2 * n_cast]
    kn_ref, vt_ref, qt_ref, m_ref, acc_ref, s0_ref, s1_ref, mx0_ref, mx1_ref = refs[11 + 2 * n_cast:]

    for w_ref, c_ref in zip(w_refs, c_refs):
        c_ref[...] = w_ref[...].astype(BF16)

    i = pl.program_id(1)
    s_len = k_ref.shape[0]
    lo = lax.broadcasted_iota(jnp.int32, (1, HEAD_W), 1) < DA_HEAD_DIM
    prep_rows = min(512, s_len)

    @pl.when(i == 0)
    def _():
        vt_ref[HEAD_W:ACC_ROWS, :] = jnp.ones((ACC_ROWS - HEAD_W, s_len), BF16)

        def body(c, carry):
            r = pl.multiple_of(c * prep_rows, prep_rows)
            kk = k_ref[pl.ds(r, prep_rows), :].astype(F32)
            kn_ref[pl.ds(r, prep_rows), :] = _half_norm(kk, kg_ref[...], lo).astype(BF16)
            vv = v_ref[pl.ds(r, prep_rows), :].astype(F32)
            vt_ref[0:HEAD_W, pl.ds(r, prep_rows)] = vv.T.astype(BF16)
            return carry
        lax.fori_loop(0, s_len // prep_rows, body, 0)

    dh = DA_HEAD_DIM
    qt = q_ref[...].astype(F32).T
    qq = qt * qt
    qscale = DA_SCALE * LOG2E
    inv1 = lax.rsqrt(jnp.sum(qq[0:dh], axis=0, keepdims=True) * (1.0 / dh) + EPS) * qscale
    inv2 = lax.rsqrt(jnp.sum(qq[dh:HEAD_W], axis=0, keepdims=True) * (1.0 / dh) + EPS) * qscale
    zeros = jnp.zeros((dh, tq), BF16)
    qt_ref[0:dh, 0:tq] = (qt[0:dh] * inv1 * qg_ref[0:dh, :]).astype(BF16)
    qt_ref[dh:HEAD_W, 0:tq] = zeros
    qt_ref[0:dh, tq:2 * tq] = zeros
    qt_ref[dh:HEAD_W, tq:2 * tq] = (qt[dh:HEAD_W] * inv2 * qg_ref[dh:HEAD_W, :]).astype(BF16)
    m_ref[...] = jnp.full(m_ref.shape, NEG_BIG, F32)
    acc_ref[...] = jnp.zeros(acc_ref.shape, F32)

    groups = [(c * cw, (c + 1) * cw) for c in range(2 * tq // cw)]

    def softmax_update(a, b, s, s_max, vt):
        m_prev = m_ref[:, a:b]
        m_new = jnp.maximum(m_prev, s_max)
        alpha = jnp.exp2(m_prev - m_new)
        p = jnp.exp2((s - m_new).astype(BF16))
        acc_ref[:, a:b] = alpha * acc_ref[:, a:b] + jnp.dot(vt, p, preferred_element_type=F32)
        m_ref[:, a:b] = m_new

    def scores(jb, a, b, s_ref, mx_ref):
        r = pl.multiple_of(jb * tk, tk)
        s = jnp.dot(kn_ref[pl.ds(r, tk), :], qt_ref[:, a:b], preferred_element_type=F32)
        s_ref[a // cw] = s
        mx_ref[:, a:b] = jnp.max(s, axis=0, keepdims=True)

    def update(jb, a, b, s_ref, mx_ref):
        r = pl.multiple_of(jb * tk, tk)
        softmax_update(a, b, s_ref[a // cw], mx_ref[:, a:b], vt_ref[:, pl.ds(r, tk)])

    n_full = i * (tq // tk)
    for a, b in groups:
        scores(0, a, b, s0_ref, mx0_ref)

    def pair(jb):
        for a, b in groups:
            scores(jb + 1, a, b, s1_ref, mx1_ref)
            update(jb, a, b, s0_ref, mx0_ref)
        for a, b in groups:
            scores(jb + 2, a, b, s0_ref, mx0_ref)
            update(jb + 1, a, b, s1_ref, mx1_ref)

    def unrolled_pairs(t, carry):
        for u in range(unroll):
            pair(2 * unroll * t + 2 * u)
        return carry

    def one_pair(t, carry):
        pair(n_full - n_full % (2 * unroll) + 2 * t)
        return carry

    lax.fori_loop(0, n_full // (2 * unroll), unrolled_pairs, 0)
    lax.fori_loop(0, (n_full % (2 * unroll)) // 2, one_pair, 0)

    for d in range(tq // tk):
        r = pl.multiple_of((n_full + d) * tk, tk)
        for a, b in groups:
            q0 = a % tq
            k0, k1 = d * tk, min((d + 1) * tk, q0 + cw)
            if k1 <= k0:
                continue
            rows = k1 - k0
            if d == 0:
                s = s0_ref[a // cw, 0:rows, :]
            else:
                s = jnp.dot(kn_ref[pl.ds(r, rows), :], qt_ref[:, a:b], preferred_element_type=F32)
            if k1 - 1 > q0:
                key = k0 + lax.broadcasted_iota(jnp.int32, s.shape, 0)
                qry = q0 + lax.broadcasted_iota(jnp.int32, s.shape, 1)
                s = jnp.where(key <= qry, s, NEG_BIG)
            softmax_update(a, b, s, jnp.max(s, axis=0, keepdims=True), vt_ref[:, pl.ds(r, rows)])

    o1 = acc_ref[0:HEAD_W, 0:tq] * (1.0 / acc_ref[HEAD_W:HEAD_W + 1, 0:tq])
    o2 = acc_ref[0:HEAD_W, tq:2 * tq] * (1.0 / acc_ref[HEAD_W:HEAD_W + 1, tq:2 * tq])
    lam = (jnp.exp(jnp.sum(lq1_ref[...] * lk1_ref[...], axis=-1, keepdims=True))
           - jnp.exp(jnp.sum(lq2_ref[...] * lk2_ref[...], axis=-1, keepdims=True)) + lambda_init)
    o = o1 - lam * o2
    inv = lax.rsqrt(jnp.mean(o * o, axis=0, keepdims=True) + EPS) * (1.0 - lambda_init)
    o_ref[...] = (o * inv * sg_ref[...]).T.astype(BF16)


def _cast_specs(w, first, n_steps, n_inner):
    nl, r, c = w.shape[0] - first, w.shape[1], w.shape[2]
    per_layer = n_steps // nl
    assert per_layer * nl == n_steps
    for nb_c in (1, 2, 4, 8, 16):
        nb_r = per_layer // nb_c
        if per_layer % nb_c or c % nb_c or r % nb_r:
            continue
        br, bc = r // nb_r, c // nb_c
        if br % 16 == 0 and bc % HEAD_W == 0:
            break
    else:
        raise ValueError(f"no slab split for {w.shape}")

    def index(offset):
        def index_map(h, i):
            t = h * n_inner + i
            rem = t % per_layer
            return (t // per_layer + offset, rem // nb_c, rem % nb_c)
        return index_map

    return (pl.BlockSpec((None, br, bc), index(first)), pl.BlockSpec((None, br, bc), index(0)),
            jax.ShapeDtypeStruct((nl, r, c), BF16))


def _attn(z, qg, kg, lq1, lk1, lq2, lk2, sg, cast=(), *, lambda_init, tq=1024, tk=512, cw=256, unroll=4):
    s = z.shape[0]
    assert s % tq == 0 and tq % (2 * tk) == 0 and tq % cw == 0
    small = lambda w: pl.BlockSpec((1, w), lambda h, i: (0, 0))
    column = pl.BlockSpec((HEAD_W, 1), lambda h, i: (0, 0))
    n_inner = s // tq
    cast_specs = [_cast_specs(w, first, DA_HEADS * n_inner, n_inner) for w, first in cast]
    kern = functools.partial(_attn_kernel, n_cast=len(cast), tq=tq, tk=tk, cw=cw, unroll=unroll,
                             lambda_init=lambda_init)
    out = pl.pallas_call(
        kern,
        out_shape=[jax.ShapeDtypeStruct((s, DA_HEADS * HEAD_W), BF16)] + [c[2] for c in cast_specs],
        grid=(DA_HEADS, n_inner),
        in_specs=[
            pl.BlockSpec((tq, HEAD_W), lambda h, i: (i, QA_BLK + h)),
            pl.BlockSpec((s, HEAD_W), lambda h, i: (0, KA_BLK + h)),
            pl.BlockSpec((s, HEAD_W), lambda h, i: (0, VA_BLK + h)),
            column, small(HEAD_W),
            small(DA_HEAD_DIM), small(DA_HEAD_DIM), small(DA_HEAD_DIM), small(DA_HEAD_DIM),
            column,
        ] + [c[0] for c in cast_specs],
        out_specs=[pl.BlockSpec((tq, HEAD_W), lambda h, i: (i, h))] + [c[1] for c in cast_specs],
        scratch_shapes=[
            pltpu.VMEM((s, HEAD_W), BF16),
            pltpu.VMEM((ACC_ROWS, s), BF16),
            pltpu.VMEM((HEAD_W, 2 * tq), BF16),
            pltpu.VMEM((1, 2 * tq), F32),
            pltpu.VMEM((ACC_ROWS, 2 * tq), F32),
            pltpu.VMEM((2 * tq // cw, tk, cw), F32),
            pltpu.VMEM((2 * tq // cw, tk, cw), F32),
            pltpu.VMEM((1, 2 * tq), F32),
            pltpu.VMEM((1, 2 * tq), F32),
        ],
        compiler_params=_params(("arbitrary", "arbitrary"), 58),
        name="diff_attn",
    )(z, z, z, qg, kg, lq1, lk1, lq2, lk2, sg, *[w for w, _ in cast])
    return out[0], out[1:]


def _rope_kernel(pos_ref, invf_ref, cos_ref, sin_ref):
    ang = pos_ref[...].astype(F32) * invf_ref[...]
    lo = lax.broadcasted_iota(jnp.int32, (1, HEAD_W), 1) < HEAD_W // 2
    cos_ref[...] = jnp.cos(ang)
    sin_ref[...] = jnp.where(lo, -1.0, 1.0) * jnp.sin(ang)


def _rope_tables(pos_col, invf, *, tm=1024):
    s = pos_col.shape[0]
    tm = min(tm, s)
    return pl.pallas_call(
        _rope_kernel,
        out_shape=(jax.ShapeDtypeStruct((s, HEAD_W), F32), jax.ShapeDtypeStruct((s, HEAD_W), F32)),
        grid=(s // tm,),
        in_specs=[pl.BlockSpec((tm, 1), lambda i: (i, 0)), pl.BlockSpec((1, HEAD_W), lambda i: (0, 0))],
        out_specs=(pl.BlockSpec((tm, HEAD_W), lambda i: (i, 0)), pl.BlockSpec((tm, HEAD_W), lambda i: (i, 0))),
        compiler_params=_params(("arbitrary",), 32),
        name="rope_tables",
    )(pos_col, invf)


def _ret_kernel(q_ref, k_ref, v_ref, gr_ref, cos_ref, sin_ref, dec_ref, zeta_ref, xi_ref, cd_ref, sg_ref,
                o_ref, st_ref, *, heads):
    @pl.when(pl.program_id(1) == 0)
    def _():
        st_ref[...] = jnp.zeros(st_ref.shape, F32)

    cosf = cos_ref[...]
    sinf = sin_ref[...]
    for hh in range(heads):
        lanes = slice(hh * HEAD_W, (hh + 1) * HEAD_W)
        q = q_ref[:, lanes].astype(F32)
        k = k_ref[:, lanes].astype(F32)
        q = q * cosf + pltpu.roll(q, HEAD_W // 2, 1) * sinf
        k = (k * cosf + pltpu.roll(k, HEAD_W // 2, 1) * sinf) * RET_SCALE
        v = v_ref[:, lanes]
        st = st_ref[hh]

        sc = lax.dot_general(q.astype(BF16), k.astype(BF16), (((1,), (1,)), ((), ())),
                             preferred_element_type=F32) * dec_ref[hh]
        o = jnp.dot(sc.astype(BF16), v, preferred_element_type=F32)
        o = o + jnp.dot((q * xi_ref[hh]).astype(BF16), st.astype(BF16), preferred_element_type=F32)
        kz = (k * zeta_ref[hh]).astype(BF16)
        kv = lax.dot_general(kz, v, (((0,), (0,)), ((), ())), preferred_element_type=F32)
        st_ref[hh] = cd_ref[hh, 0:1, :] * st + kv

        g = gr_ref[:, lanes].astype(F32)
        o_ref[:, lanes] = (_rms(o, sg_ref[...]) * (g * jax.nn.sigmoid(g))).astype(BF16)


def _retention(z, cosf, sinf, dec, zeta, xi, cd, sg, *, chunk, heads=8):
    s = z.shape[0]
    w = heads * HEAD_W
    row = lambda blk: pl.BlockSpec((chunk, w), lambda h, i: (i, blk // heads + h))
    tab = pl.BlockSpec((chunk, HEAD_W), lambda h, i: (i, 0))
    per_head = lambda r, c: pl.BlockSpec((heads, r, c), lambda h, i: (h, 0, 0))
    return pl.pallas_call(
        functools.partial(_ret_kernel, heads=heads),
        out_shape=jax.ShapeDtypeStruct((s, RET_HEADS * HEAD_W), BF16),
        grid=(RET_HEADS // heads, s // chunk),
        in_specs=[
            row(QR_BLK), row(KR_BLK), row(VR_BLK), row(GR_BLK), tab, tab,
            per_head(chunk, chunk), per_head(chunk, HEAD_W), per_head(chunk, HEAD_W), per_head(8, HEAD_W),
            pl.BlockSpec((1, HEAD_W), lambda h, i: (0, 0)),
        ],
        out_specs=pl.BlockSpec((chunk, w), lambda h, i: (i, h)),
        scratch_shapes=[pltpu.VMEM((heads, HEAD_W, HEAD_W), F32)],
        compiler_params=_params(("arbitrary", "arbitrary"), 40),
        name="retention",
    )(z, z, z, z, cosf, sinf, dec, zeta, xi, cd, sg)


def _retention_tables(chunk):
    log_g = jnp.log(1.0 - 2.0 ** (-5.0 - jnp.arange(RET_HEADS, dtype=F32)))
    idx = jnp.arange(chunk, dtype=F32)
    rel = idx[:, None] - idx[None, :]
    dec = jnp.exp(log_g[:, None, None] * jnp.maximum(rel, 0.0)) * (rel >= 0)
    zeta = jnp.exp(log_g[:, None] * (chunk - 1 - idx))
    xi = jnp.exp(log_g[:, None] * (idx + 1.0))
    cd = jnp.exp(log_g * chunk)
    bcast = lambda t: jnp.broadcast_to(t[:, :, None], (RET_HEADS, chunk, HEAD_W))
    return dec, bcast(zeta), bcast(xi), jnp.broadcast_to(cd[:, None, None], (RET_HEADS, 8, HEAD_W))


def _merge_kernel(x_ref, ya_ref, yb_ref, ga0_ref, ga1_ref, gb0_ref, gb1_ref, wa_ref, wb_ref, wo_ref, o_ref):
    ta = jnp.dot(ya_ref[...], wa_ref[...], preferred_element_type=F32)
    tb = jnp.dot(yb_ref[...], wb_ref[...], preferred_element_type=F32)
    ga = jnp.concatenate([ga0_ref[...], ga1_ref[...]], axis=-1).astype(F32)
    gb = jnp.concatenate([gb0_ref[...], gb1_ref[...]], axis=-1).astype(F32)
    merged = jax.nn.sigmoid(ga) * ta + jax.nn.sigmoid(gb) * tb
    o_ref[...] = x_ref[...] + jnp.dot(merged.astype(BF16), wo_ref[...], preferred_element_type=F32)


def _merge(x, ya, yb, z, wa, wb, wo, layer, *, tm=256):
    s, d = x.shape
    gate = lambda blk: pl.BlockSpec((tm, GATE_BLK_W), lambda i: (i, blk))
    resident = lambda w: pl.BlockSpec((None,) + w.shape[1:], lambda i: (layer, 0, 0), pipeline_mode=pl.Buffered(1))
    return pl.pallas_call(
        _merge_kernel,
        out_shape=jax.ShapeDtypeStruct((s, d), F32),
        grid=(s // tm,),
        in_specs=[
            pl.BlockSpec((tm, d), lambda i: (i, 0)),
            pl.BlockSpec((tm, ya.shape[1]), lambda i: (i, 0)),
            pl.BlockSpec((tm, yb.shape[1]), lambda i: (i, 0)),
            gate(GA_BLK), gate(GA_BLK + 1), gate(GB_BLK), gate(GB_BLK + 1),
            resident(wa), resident(wb), resident(wo),
        ],
        out_specs=pl.BlockSpec((tm, d), lambda i: (i, 0)),
        compiler_params=_params(("arbitrary",), 48),
        name="merge_out",
    )(x, ya, yb, z, z, z, z, wa, wb, wo)


def _ple_kernel(x_ref, g_ref, p_ref, wg_ref, wp_ref, o_ref):
    x = x_ref[...]
    h = _rms(x, g_ref[...]).astype(BF16)
    gate = jax.nn.sigmoid(jnp.dot(h, wg_ref[...], preferred_element_type=F32))
    proj = jnp.dot(p_ref[...].astype(BF16), wp_ref[...], preferred_element_type=F32)
    o_ref[...] = x + gate * proj


def _ple(x, g, p, wg, wp, layer, *, tm=512):
    s, d = x.shape
    resident = lambda w: pl.BlockSpec((None,) + w.shape[1:], lambda i: (layer, 0, 0), pipeline_mode=pl.Buffered(1))
    return pl.pallas_call(
        _ple_kernel,
        out_shape=jax.ShapeDtypeStruct((s, d), F32),
        grid=(s // tm,),
        in_specs=[
            pl.BlockSpec((tm, d), lambda i: (i, 0)),
            pl.BlockSpec((1, d), lambda i: (0, 0)),
            pl.BlockSpec((None, tm, p.shape[2]), lambda i: (layer, i, 0)),
            resident(wg), resident(wp),
        ],
        out_specs=pl.BlockSpec((tm, d), lambda i: (i, 0)),
        compiler_params=_params(("arbitrary",), 48),
        name="ple",
    )(x, g, p, wg, wp)


def kernel(x, p, positions, ffn1_norm, ffn1_w_gate, ffn1_w_up, ffn1_w_down, mix_norm, w_in, da_q_norm, da_k_norm, da_lambda_q1, da_lambda_k1, da_lambda_q2, da_lambda_k2, da_sub_norm, ret_sub_norm, w_up_a, w_up_b, w_out, ffn2_norm, ffn2_w_gate, ffn2_w_up, ffn2_w_down, ple_norm, w_ple_gate, w_ple_proj):
    b, s, d = x.shape
    assert b == 1
    depth = w_in.shape[0]
    chunk = min(256, s)

    row = lambda t: t.reshape(1, -1)
    twice = lambda t: jnp.concatenate([t, t]).reshape(1, -1)
    bf = lambda t: t.astype(BF16)

    inv_freq = ROPE_BASE ** (-jnp.arange(0, HEAD_W, 2, dtype=F32) / HEAD_W)
    cosf, sinf = _rope_tables(positions.reshape(s, 1), twice(inv_freq))
    dec, zeta, xi, cd = _retention_tables(chunk)

    early = [bf(w[0:1]) for w in (ffn1_w_gate, ffn1_w_up, ffn1_w_down, w_in)]
    wpp = bf(w_ple_proj)
    late = [(w, 1) for w in (ffn1_w_gate, ffn1_w_up, ffn1_w_down, w_in)] if depth > 1 else []
    rest = [(w, 0) for w in (ffn2_w_gate, ffn2_w_up, ffn2_w_down, w_up_a, w_up_b, w_out, w_ple_gate)]
    p3 = p.reshape(depth, s, -1)

    xc = x.reshape(s, d)
    for i in range(depth):
        lambda_init = 0.8 - 0.6 * math.exp(-0.3 * i)
        f1g, f1u, f1d, win = early if i == 0 else cast_late
        first = 0 if i == 0 else i - 1
        xc = _ffn(xc, row(ffn1_norm[i]), f1g, f1u, f1d, first)
        z = _inproj(xc, row(mix_norm[i]), win, first)
        ya, cast_out = _attn(z, twice(da_q_norm[i]).reshape(-1, 1), twice(da_k_norm[i]), row(da_lambda_q1[i]),
                             row(da_lambda_k1[i]), row(da_lambda_q2[i]), row(da_lambda_k2[i]),
                             da_sub_norm[i].reshape(-1, 1), (late + rest) if i == 0 else (),
                             lambda_init=lambda_init)
        if i == 0:
            cast_late = cast_out[:len(late)]
            f2g, f2u, f2d, wua, wub, wo, wpg = cast_out[len(late):]
        yb = _retention(z, cosf, sinf, dec, zeta, xi, cd, row(ret_sub_norm[i]), chunk=chunk)
        xc = _merge(xc, ya, yb, z, wua, wub, wo, i)
        xc = _ffn(xc, row(ffn2_norm[i]), f2g, f2u, f2d, i)
        xc = _ple(xc, row(ple_norm[i]), p3, wpg, wpp, i)
    return xc.reshape(b, s, d)
```

```python
import functools
import math

import jax
import jax.numpy as jnp
from jax import lax
from jax.experimental import pallas as pl
from jax.experimental.pallas import tpu as pltpu

F32 = jnp.float32
BF16 = jnp.bfloat16

EPS = 1e-6
ROPE_BASE = 10000.0
DA_HEADS = 8
DA_HEAD_DIM = 64
RET_HEADS = 8
HEAD_W = 128
RET_SCALE = HEAD_W ** -0.5
DA_SCALE = DA_HEAD_DIM ** -0.5
NEG_BIG = -1e30
LOG2E = math.log2(math.e)
ACC_ROWS = HEAD_W + 16

QA_BLK, KA_BLK, VA_BLK, QR_BLK, KR_BLK, VR_BLK, GR_BLK = 0, 8, 16, 24, 32, 40, 48
GATE_BLK_W = 1024
GA_BLK, GB_BLK = 7, 9

MIB = 1024 * 1024


def _params(sem, vmem_mib):
    return pltpu.CompilerParams(dimension_semantics=sem, vmem_limit_bytes=vmem_mib * MIB)


def _rms(x, g):
    return x * lax.rsqrt(jnp.mean(x * x, axis=-1, keepdims=True) + EPS) * g


def _ffn_kernel(x_ref, g_ref, wg_ref, wu_ref, wd_ref, o_ref, h_ref):
    j = pl.program_id(1)

    @pl.when(j == 0)
    def _():
        x = x_ref[...]
        h_ref[...] = _rms(x, g_ref[...]).astype(BF16)
        o_ref[...] = x

    h = h_ref[...]
    gate = jnp.dot(h, wg_ref[...], preferred_element_type=F32)
    up = jnp.dot(h, wu_ref[...], preferred_element_type=F32)
    act = (gate * jax.nn.sigmoid(gate) * up * 0.5).astype(BF16)
    o_ref[...] += jnp.dot(act, wd_ref[...], preferred_element_type=F32)


def _ffn(x, g, wg, wu, wd, layer, *, tm=1024, tf=512):
    s, d = x.shape
    tm = min(tm, s)
    f = wg.shape[2]
    return pl.pallas_call(
        _ffn_kernel,
        out_shape=jax.ShapeDtypeStruct((s, d), F32),
        grid=(s // tm, f // tf),
        in_specs=[
            pl.BlockSpec((tm, d), lambda i, j: (i, 0)),
            pl.BlockSpec((1, d), lambda i, j: (0, 0)),
            pl.BlockSpec((None, d, tf), lambda i, j: (layer, 0, j)),
            pl.BlockSpec((None, d, tf), lambda i, j: (layer, 0, j)),
            pl.BlockSpec((None, tf, d), lambda i, j: (layer, j, 0)),
        ],
        out_specs=pl.BlockSpec((tm, d), lambda i, j: (i, 0)),
        scratch_shapes=[pltpu.VMEM((tm, d), BF16)],
        compiler_params=_params(("arbitrary", "arbitrary"), 58),
        name="ffn",
    )(x, g, wg, wu, wd)


def _inproj_kernel(x_ref, g_ref, w_ref, z_ref, h_ref):
    @pl.when(pl.program_id(1) == 0)
    def _():
        h_ref[...] = _rms(x_ref[...], g_ref[...]).astype(BF16)

    z_ref[...] = jnp.dot(h_ref[...], w_ref[...], preferred_element_type=F32).astype(BF16)


def _inproj(x, g, w, layer, *, tm=1024, tn=1024):
    s, d = x.shape
    tm = min(tm, s)
    n = w.shape[2]
    return pl.pallas_call(
        _inproj_kernel,
        out_shape=jax.ShapeDtypeStruct((s, n), BF16),
        grid=(s // tm, n // tn),
        in_specs=[
            pl.BlockSpec((tm, d), lambda i, j: (i, 0)),
            pl.BlockSpec((1, d), lambda i, j: (0, 0)),
            pl.BlockSpec((None, d, tn), lambda i, j: (layer, 0, j)),
        ],
        out_specs=pl.BlockSpec((tm, tn), lambda i, j: (i, j)),
        scratch_shapes=[pltpu.VMEM((tm, d), BF16)],
        compiler_params=_params(("arbitrary", "arbitrary"), 48),
        name="in_proj",
    )(x, g, w)


def _half_norm(x, g, lo):
    xx = x * x
    s_lo = jnp.sum(jnp.where(lo, xx, 0.0), axis=-1, keepdims=True)
    s_hi = jnp.sum(jnp.where(lo, 0.0, xx), axis=-1, keepdims=True)
    inv = jnp.where(lo, lax.rsqrt(s_lo * (1.0 / DA_HEAD_DIM) + EPS),
                    lax.rsqrt(s_hi * (1.0 / DA_HEAD_DIM) + EPS))
    return x * inv * g


def _attn_kernel(*refs, n_cast, tq, tk, cw, unroll, lambda_init):
    q_ref, k_ref, v_ref, qg_ref, kg_ref, lq1_ref, lk1_ref, lq2_ref, lk2_ref, sg_ref = refs[:10]
    w_refs = refs[10:10 + n_cast]
    o_ref = refs[10 + n_cast]
    c_refs = refs[11 + n_cast:11 + 2 * n_cast]
    kn_ref, vt_ref, qt_ref, m_ref, acc_ref, s0_ref, s1_ref, mx0_ref, mx1_ref = refs[11 + 2 * n_cast:]

    for w_ref, c_ref in zip(w_refs, c_refs):
        c_ref[...] = w_ref[...].astype(BF16)

    i = pl.program_id(1)
    s_len = k_ref.shape[0]
    lo = lax.broadcasted_iota(jnp.int32, (1, HEAD_W), 1) < DA_HEAD_DIM
    prep_rows = min(512, s_len)

    @pl.when(i == 0)
    def _():
        vt_ref[HEAD_W:ACC_ROWS, :] = jnp.ones((ACC_ROWS - HEAD_W, s_len), BF16)

        def body(c, carry):
            r = pl.multiple_of(c * prep_rows, prep_rows)
            kk = k_ref[pl.ds(r, prep_rows), :].astype(F32)
            kn_ref[pl.ds(r, prep_rows), :] = _half_norm(kk, kg_ref[...], lo).astype(BF16)
            vv = v_ref[pl.ds(r, prep_rows), :].astype(F32)
            vt_ref[0:HEAD_W, pl.ds(r, prep_rows)] = vv.T.astype(BF16)
            return carry
        lax.fori_loop(0, s_len // prep_rows, body, 0)

    dh = DA_HEAD_DIM
    qt = q_ref[...].astype(F32).T
    qq = qt * qt
    qscale = DA_SCALE * LOG2E
    inv1 = lax.rsqrt(jnp.sum(qq[0:dh], axis=0, keepdims=True) * (1.0 / dh) + EPS) * qscale
    inv2 = lax.rsqrt(jnp.sum(qq[dh:HEAD_W], axis=0, keepdims=True) * (1.0 / dh) + EPS) * qscale
    zeros = jnp.zeros((dh, tq), BF16)
    qt_ref[0:dh, 0:tq] = (qt[0:dh] * inv1 * qg_ref[0:dh, :]).astype(BF16)
    qt_ref[dh:HEAD_W, 0:tq] = zeros
    qt_ref[0:dh, tq:2 * tq] = zeros
    qt_ref[dh:HEAD_W, tq:2 * tq] = (qt[dh:HEAD_W] * inv2 * qg_ref[dh:HEAD_W, :]).astype(BF16)
    m_ref[...] = jnp.full(m_ref.shape, NEG_BIG, F32)
    acc_ref[...] = jnp.zeros(acc_ref.shape, F32)

    groups = [(c * cw, (c + 1) * cw) for c in range(2 * tq // cw)]

    def softmax_update(a, b, s, s_max, vt):
        m_prev = m_ref[:, a:b]
        m_new = jnp.maximum(m_prev, s_max)
        alpha = jnp.exp2(m_prev - m_new)
        p = jnp.exp2((s - m_new).astype(BF16))
        acc_ref[:, a:b] = alpha * acc_ref[:, a:b] + jnp.dot(vt, p, preferred_element_type=F32)
        m_ref[:, a:b] = m_new

    def scores(jb, a, b, s_ref, mx_ref):
        r = pl.multiple_of(jb * tk, tk)
        s = jnp.dot(kn_ref[pl.ds(r, tk), :], qt_ref[:, a:b], preferred_element_type=F32)
        s_ref[a // cw] = s
        mx_ref[:, a:b] = jnp.max(s, axis=0, keepdims=True)

    def update(jb, a, b, s_ref, mx_ref):
        r = pl.multiple_of(jb * tk, tk)
        softmax_update(a, b, s_ref[a // cw], mx_ref[:, a:b], vt_ref[:, pl.ds(r, tk)])

    n_full = i * (tq // tk)
    for a, b in groups:
        scores(0, a, b, s0_ref, mx0_ref)

    def pair(jb):
        for a, b in groups:
            scores(jb + 1, a, b, s1_ref, mx1_ref)
            update(jb, a, b, s0_ref, mx0_ref)
        for a, b in groups:
            scores(jb + 2, a, b, s0_ref, mx0_ref)
            update(jb + 1, a, b, s1_ref, mx1_ref)

    def unrolled_pairs(t, carry):
        for u in range(unroll):
            pair(2 * unroll * t + 2 * u)
        return carry

    def one_pair(t, carry):
        pair(n_full - n_full % (2 * unroll) + 2 * t)
        return carry

    lax.fori_loop(0, n_full // (2 * unroll), unrolled_pairs, 0)
    lax.fori_loop(0, (n_full % (2 * unroll)) // 2, one_pair, 0)

    def key_rows(d, a):
        q0 = a % tq
        return d * tk, min((d + 1) * tk, q0 + cw), q0

    slots = (s0_ref, s1_ref)
    for d in range(tq // tk):
        r = pl.multiple_of((n_full + d) * tk, tk)
        r_next = pl.multiple_of((n_full + d + 1) * tk, tk)
        for a, b in groups:
            k0, k1, _ = key_rows(d + 1, a)
            if d + 1 < tq // tk and k1 > k0:
                slots[(d + 1) % 2][a // cw, 0:k1 - k0, :] = jnp.dot(
                    kn_ref[pl.ds(r_next, k1 - k0), :], qt_ref[:, a:b], preferred_element_type=F32)
            k0, k1, q0 = key_rows(d, a)
            if k1 <= k0:
                continue
            s = slots[d % 2][a // cw, 0:k1 - k0, :]
            if k1 - 1 > q0:
                key = k0 + lax.broadcasted_iota(jnp.int32, s.shape, 0)
                qry = q0 + lax.broadcasted_iota(jnp.int32, s.shape, 1)
                s = jnp.where(key <= qry, s, NEG_BIG)
            softmax_update(a, b, s, jnp.max(s, axis=0, keepdims=True), vt_ref[:, pl.ds(r, k1 - k0)])

    o1 = acc_ref[0:HEAD_W, 0:tq] * (1.0 / acc_ref[HEAD_W:HEAD_W + 1, 0:tq])
    o2 = acc_ref[0:HEAD_W, tq:2 * tq] * (1.0 / acc_ref[HEAD_W:HEAD_W + 1, tq:2 * tq])
    lam = (jnp.exp(jnp.sum(lq1_ref[...] * lk1_ref[...], axis=-1, keepdims=True))
           - jnp.exp(jnp.sum(lq2_ref[...] * lk2_ref[...], axis=-1, keepdims=True)) + lambda_init)
    o = o1 - lam * o2
    inv = lax.rsqrt(jnp.mean(o * o, axis=0, keepdims=True) + EPS) * (1.0 - lambda_init)
    o_ref[...] = (o * inv * sg_ref[...]).T.astype(BF16)


def _cast_specs(w, first, n_steps, n_inner):
    nl, r, c = w.shape[0] - first, w.shape[1], w.shape[2]
    per_layer = n_steps // nl
    assert per_layer * nl == n_steps
    for nb_c in (1, 2, 4, 8, 16):
        nb_r = per_layer // nb_c
        if per_layer % nb_c or c % nb_c or r % nb_r:
            continue
        br, bc = r // nb_r, c // nb_c
        if br % 16 == 0 and bc % HEAD_W == 0:
            break
    else:
        raise ValueError(f"no slab split for {w.shape}")

    def index(offset):
        def index_map(h, i):
            t = h * n_inner + i
            rem = t % per_layer
            return (t // per_layer + offset, rem // nb_c, rem % nb_c)
        return index_map

    return (pl.BlockSpec((None, br, bc), index(first)), pl.BlockSpec((None, br, bc), index(0)),
            jax.ShapeDtypeStruct((nl, r, c), BF16))


def _attn(z, qg, kg, lq1, lk1, lq2, lk2, sg, cast=(), *, lambda_init, tq=1024, tk=512, cw=256, unroll=4):
    s = z.shape[0]
    assert s % tq == 0 and tq % (2 * tk) == 0 and tq % cw == 0
    small = lambda w: pl.BlockSpec((1, w), lambda h, i: (0, 0))
    column = pl.BlockSpec((HEAD_W, 1), lambda h, i: (0, 0))
    n_inner = s // tq
    cast_specs = [_cast_specs(w, first, DA_HEADS * n_inner, n_inner) for w, first in cast]
    kern = functools.partial(_attn_kernel, n_cast=len(cast), tq=tq, tk=tk, cw=cw, unroll=unroll,
                             lambda_init=lambda_init)
    out = pl.pallas_call(
        kern,
        out_shape=[jax.ShapeDtypeStruct((s, DA_HEADS * HEAD_W), BF16)] + [c[2] for c in cast_specs],
        grid=(DA_HEADS, n_inner),
        in_specs=[
            pl.BlockSpec((tq, HEAD_W), lambda h, i: (i, QA_BLK + h)),
            pl.BlockSpec((s, HEAD_W), lambda h, i: (0, KA_BLK + h)),
            pl.BlockSpec((s, HEAD_W), lambda h, i: (0, VA_BLK + h)),
            column, small(HEAD_W),
            small(DA_HEAD_DIM), small(DA_HEAD_DIM), small(DA_HEAD_DIM), small(DA_HEAD_DIM),
            column,
        ] + [c[0] for c in cast_specs],
        out_specs=[pl.BlockSpec((tq, HEAD_W), lambda h, i: (i, h))] + [c[1] for c in cast_specs],
        scratch_shapes=[
            pltpu.VMEM((s, HEAD_W), BF16),
            pltpu.VMEM((ACC_ROWS, s), BF16),
            pltpu.VMEM((HEAD_W, 2 * tq), BF16),
            pltpu.VMEM((1, 2 * tq), F32),
            pltpu.VMEM((ACC_ROWS, 2 * tq), F32),
            pltpu.VMEM((2 * tq // cw, tk, cw), F32),
            pltpu.VMEM((2 * tq // cw, tk, cw), F32),
            pltpu.VMEM((1, 2 * tq), F32),
            pltpu.VMEM((1, 2 * tq), F32),
        ],
        compiler_params=_params(("arbitrary", "arbitrary"), 58),
        name="diff_attn",
    )(z, z, z, qg, kg, lq1, lk1, lq2, lk2, sg, *[w for w, _ in cast])
    return out[0], out[1:]


def _rope_kernel(pos_ref, invf_ref, cos_ref, sin_ref):
    ang = pos_ref[...].astype(F32) * invf_ref[...]
    lo = lax.broadcasted_iota(jnp.int32, (1, HEAD_W), 1) < HEAD_W // 2
    cos_ref[...] = jnp.cos(ang)
    sin_ref[...] = jnp.where(lo, -1.0, 1.0) * jnp.sin(ang)


def _rope_tables(pos_col, invf, *, tm=1024):
    s = pos_col.shape[0]
    tm = min(tm, s)
    return pl.pallas_call(
        _rope_kernel,
        out_shape=(jax.ShapeDtypeStruct((s, HEAD_W), F32), jax.ShapeDtypeStruct((s, HEAD_W), F32)),
        grid=(s // tm,),
        in_specs=[pl.BlockSpec((tm, 1), lambda i: (i, 0)), pl.BlockSpec((1, HEAD_W), lambda i: (0, 0))],
        out_specs=(pl.BlockSpec((tm, HEAD_W), lambda i: (i, 0)), pl.BlockSpec((tm, HEAD_W), lambda i: (i, 0))),
        compiler_params=_params(("arbitrary",), 32),
        name="rope_tables",
    )(pos_col, invf)


def _ret_kernel(q_ref, k_ref, v_ref, gr_ref, cos_ref, sin_ref, dec_ref, zeta_ref, xi_ref, cd_ref, sg_ref,
                o_ref, st_ref, *, heads):
    @pl.when(pl.program_id(1) == 0)
    def _():
        st_ref[...] = jnp.zeros(st_ref.shape, F32)

    cosf = cos_ref[...]
    sinf = sin_ref[...]
    for hh in range(heads):
        lanes = slice(hh * HEAD_W, (hh + 1) * HEAD_W)
        q = q_ref[:, lanes].astype(F32)
        k = k_ref[:, lanes].astype(F32)
        q = q * cosf + pltpu.roll(q, HEAD_W // 2, 1) * sinf
        k = (k * cosf + pltpu.roll(k, HEAD_W // 2, 1) * sinf) * RET_SCALE
        v = v_ref[:, lanes]
        st = st_ref[hh]

        sc = lax.dot_general(q.astype(BF16), k.astype(BF16), (((1,), (1,)), ((), ())),
                             preferred_element_type=F32) * dec_ref[hh]
        o = jnp.dot(sc.astype(BF16), v, preferred_element_type=F32)
        o = o + jnp.dot((q * xi_ref[hh]).astype(BF16), st.astype(BF16), preferred_element_type=F32)
        kz = (k * zeta_ref[hh]).astype(BF16)
        kv = lax.dot_general(kz, v, (((0,), (0,)), ((), ())), preferred_element_type=F32)
        st_ref[hh] = cd_ref[hh, 0:1, :] * st + kv

        g = gr_ref[:, lanes].astype(F32)
        o_ref[:, lanes] = (_rms(o, sg_ref[...]) * (g * jax.nn.sigmoid(g))).astype(BF16)


def _retention(z, cosf, sinf, dec, zeta, xi, cd, sg, *, chunk, heads=8):
    s = z.shape[0]
    w = heads * HEAD_W
    row = lambda blk: pl.BlockSpec((chunk, w), lambda h, i: (i, blk // heads + h))
    tab = pl.BlockSpec((chunk, HEAD_W), lambda h, i: (i, 0))
    per_head = lambda r, c: pl.BlockSpec((heads, r, c), lambda h, i: (h, 0, 0))
    return pl.pallas_call(
        functools.partial(_ret_kernel, heads=heads),
        out_shape=jax.ShapeDtypeStruct((s, RET_HEADS * HEAD_W), BF16),
        grid=(RET_HEADS // heads, s // chunk),
        in_specs=[
            row(QR_BLK), row(KR_BLK), row(VR_BLK), row(GR_BLK), tab, tab,
            per_head(chunk, chunk), per_head(chunk, HEAD_W), per_head(chunk, HEAD_W), per_head(8, HEAD_W),
            pl.BlockSpec((1, HEAD_W), lambda h, i: (0, 0)),
        ],
        out_specs=pl.BlockSpec((chunk, w), lambda h, i: (i, h)),
        scratch_shapes=[pltpu.VMEM((heads, HEAD_W, HEAD_W), F32)],
        compiler_params=_params(("arbitrary", "arbitrary"), 40),
        name="retention",
    )(z, z, z, z, cosf, sinf, dec, zeta, xi, cd, sg)


def _retention_tables(chunk):
    log_g = jnp.log(1.0 - 2.0 ** (-5.0 - jnp.arange(RET_HEADS, dtype=F32)))
    idx = jnp.arange(chunk, dtype=F32)
    rel = idx[:, None] - idx[None, :]
    dec = jnp.exp(log_g[:, None, None] * jnp.maximum(rel, 0.0)) * (rel >= 0)
    zeta = jnp.exp(log_g[:, None] * (chunk - 1 - idx))
    xi = jnp.exp(log_g[:, None] * (idx + 1.0))
    cd = jnp.exp(log_g * chunk)
    bcast = lambda t: jnp.broadcast_to(t[:, :, None], (RET_HEADS, chunk, HEAD_W))
    return dec, bcast(zeta), bcast(xi), jnp.broadcast_to(cd[:, None, None], (RET_HEADS, 8, HEAD_W))


def _merge_kernel(x_ref, ya_ref, yb_ref, ga0_ref, ga1_ref, gb0_ref, gb1_ref, wa_ref, wb_ref, wo_ref, o_ref):
    ta = jnp.dot(ya_ref[...], wa_ref[...], preferred_element_type=F32)
    tb = jnp.dot(yb_ref[...], wb_ref[...], preferred_element_type=F32)
    ga = jnp.concatenate([ga0_ref[...], ga1_ref[...]], axis=-1).astype(F32)
    gb = jnp.concatenate([gb0_ref[...], gb1_ref[...]], axis=-1).astype(F32)
    merged = jax.nn.sigmoid(ga) * ta + jax.nn.sigmoid(gb) * tb
    o_ref[...] = x_ref[...] + jnp.dot(merged.astype(BF16), wo_ref[...], preferred_element_type=F32)


def _merge(x, ya, yb, z, wa, wb, wo, layer, *, tm=256):
    s, d = x.shape
    gate = lambda blk: pl.BlockSpec((tm, GATE_BLK_W), lambda i: (i, blk))
    resident = lambda w: pl.BlockSpec((None,) + w.shape[1:], lambda i: (layer, 0, 0), pipeline_mode=pl.Buffered(1))
    return pl.pallas_call(
        _merge_kernel,
        out_shape=jax.ShapeDtypeStruct((s, d), F32),
        grid=(s // tm,),
        in_specs=[
            pl.BlockSpec((tm, d), lambda i: (i, 0)),
            pl.BlockSpec((tm, ya.shape[1]), lambda i: (i, 0)),
            pl.BlockSpec((tm, yb.shape[1]), lambda i: (i, 0)),
            gate(GA_BLK), gate(GA_BLK + 1), gate(GB_BLK), gate(GB_BLK + 1),
            resident(wa), resident(wb), resident(wo),
        ],
        out_specs=pl.BlockSpec((tm, d), lambda i: (i, 0)),
        compiler_params=_params(("arbitrary",), 48),
        name="merge_out",
    )(x, ya, yb, z, z, z, z, wa, wb, wo)


def _ple_kernel(x_ref, g_ref, p_ref, wg_ref, wp_ref, o_ref):
    x = x_ref[...]
    h = _rms(x, g_ref[...]).astype(BF16)
    gate = jax.nn.sigmoid(jnp.dot(h, wg_ref[...], preferred_element_type=F32))
    proj = jnp.dot(p_ref[...].astype(BF16), wp_ref[...], preferred_element_type=F32)
    o_ref[...] = x + gate * proj


def _ple(x, g, p, wg, wp, layer, *, tm=512):
    s, d = x.shape
    resident = lambda w: pl.BlockSpec((None,) + w.shape[1:], lambda i: (layer, 0, 0), pipeline_mode=pl.Buffered(1))
    return pl.pallas_call(
        _ple_kernel,
        out_shape=jax.ShapeDtypeStruct((s, d), F32),
        grid=(s // tm,),
        in_specs=[
            pl.BlockSpec((tm, d), lambda i: (i, 0)),
            pl.BlockSpec((1, d), lambda i: (0, 0)),
            pl.BlockSpec((None, tm, p.shape[2]), lambda i: (layer, i, 0)),
            resident(wg), resident(wp),
        ],
        out_specs=pl.BlockSpec((tm, d), lambda i: (i, 0)),
        compiler_params=_params(("arbitrary",), 48),
        name="ple",
    )(x, g, p, wg, wp)


def kernel(x, p, positions, ffn1_norm, ffn1_w_gate, ffn1_w_up, ffn1_w_down, mix_norm, w_in, da_q_norm, da_k_norm, da_lambda_q1, da_lambda_k1, da_lambda_q2, da_lambda_k2, da_sub_norm, ret_sub_norm, w_up_a, w_up_b, w_out, ffn2_norm, ffn2_w_gate, ffn2_w_up, ffn2_w_down, ple_norm, w_ple_gate, w_ple_proj):
    b, s, d = x.shape
    assert b == 1
    depth = w_in.shape[0]
    chunk = min(256, s)

    row = lambda t: t.reshape(1, -1)
    twice = lambda t: jnp.concatenate([t, t]).reshape(1, -1)
    bf = lambda t: t.astype(BF16)

    inv_freq = ROPE_BASE ** (-jnp.arange(0, HEAD_W, 2, dtype=F32) / HEAD_W)
    cosf, sinf = _rope_tables(positions.reshape(s, 1), twice(inv_freq))
    dec, zeta, xi, cd = _retention_tables(chunk)

    early = [bf(w[0:1]) for w in (ffn1_w_gate, ffn1_w_up, ffn1_w_down, w_in)]
    wpp = bf(w_ple_proj)
    late = [(w, 1) for w in (ffn1_w_gate, ffn1_w_up, ffn1_w_down, w_in)] if depth > 1 else []
    rest = [(w, 0) for w in (ffn2_w_gate, ffn2_w_up, ffn2_w_down, w_up_a, w_up_b, w_out, w_ple_gate)]
    p3 = p.reshape(depth, s, -1)

    xc = x.reshape(s, d)
    for i in range(depth):
        lambda_init = 0.8 - 0.6 * math.exp(-0.3 * i)
        f1g, f1u, f1d, win = early if i == 0 else cast_late
        first = 0 if i == 0 else i - 1
        xc = _ffn(xc, row(ffn1_norm[i]), f1g, f1u, f1d, first)
        z = _inproj(xc, row(mix_norm[i]), win, first)
        ya, cast_out = _attn(z, twice(da_q_norm[i]).reshape(-1, 1), twice(da_k_norm[i]), row(da_lambda_q1[i]),
                             row(da_lambda_k1[i]), row(da_lambda_q2[i]), row(da_lambda_k2[i]),
                             da_sub_norm[i].reshape(-1, 1), (late + rest) if i == 0 else (),
                             lambda_init=lambda_init)
        if i == 0:
            cast_late = cast_out[:len(late)]
            f2g, f2u, f2d, wua, wub, wo, wpg = cast_out[len(late):]
        yb = _retention(z, cosf, sinf, dec, zeta, xi, cd, row(ret_sub_norm[i]), chunk=chunk)
        xc = _merge(xc, ya, yb, z, wua, wub, wo, i)
        xc = _ffn(xc, row(ffn2_norm[i]), f2g, f2u, f2d, i)
        xc = _ple(xc, row(ple_norm[i]), p3, wpg, wpp, i)
    return xc.reshape(b, s, d)
```

```python
import functools
import math

import jax
import jax.numpy as jnp
from jax import lax
from jax.experimental import pallas as pl
from jax.experimental.pallas import tpu as pltpu

F32 = jnp.float32
BF16 = jnp.bfloat16

EPS = 1e-6
ROPE_BASE = 10000.0
DA_HEADS = 8
DA_HEAD_DIM = 64
RET_HEADS = 8
HEAD_W = 128
RET_SCALE = HEAD_W ** -0.5
DA_SCALE = DA_HEAD_DIM ** -0.5
NEG_BIG = -1e30
LOG2E = math.log2(math.e)
ACC_ROWS = HEAD_W + 16

QA_BLK, KA_BLK, VA_BLK, QR_BLK, KR_BLK, VR_BLK, GR_BLK = 0, 8, 16, 24, 32, 40, 48
GATE_BLK_W = 1024
GA_BLK, GB_BLK = 7, 9

MIB = 1024 * 1024


def _params(sem, vmem_mib, flags=None):
    return pltpu.CompilerParams(dimension_semantics=sem, vmem_limit_bytes=vmem_mib * MIB, flags=flags)


def _rms(x, g):
    return x * lax.rsqrt(jnp.mean(x * x, axis=-1, keepdims=True) + EPS) * g


def _ffn_kernel(x_ref, g_ref, wg_ref, wu_ref, wd_ref, o_ref, h_ref):
    j = pl.program_id(1)

    @pl.when(j == 0)
    def _():
        x = x_ref[...]
        h_ref[...] = _rms(x, g_ref[...]).astype(BF16)
        o_ref[...] = x

    h = h_ref[...]
    gate = jnp.dot(h, wg_ref[...], preferred_element_type=F32)
    up = jnp.dot(h, wu_ref[...], preferred_element_type=F32)
    act = (gate * jax.nn.sigmoid(gate) * up * 0.5).astype(BF16)
    o_ref[...] += jnp.dot(act, wd_ref[...], preferred_element_type=F32)


def _ffn(x, g, wg, wu, wd, layer, *, tm=1024, tf=512):
    s, d = x.shape
    tm = min(tm, s)
    f = wg.shape[2]
    return pl.pallas_call(
        _ffn_kernel,
        out_shape=jax.ShapeDtypeStruct((s, d), F32),
        grid=(s // tm, f // tf),
        in_specs=[
            pl.BlockSpec((tm, d), lambda i, j: (i, 0)),
            pl.BlockSpec((1, d), lambda i, j: (0, 0)),
            pl.BlockSpec((None, d, tf), lambda i, j: (layer, 0, j)),
            pl.BlockSpec((None, d, tf), lambda i, j: (layer, 0, j)),
            pl.BlockSpec((None, tf, d), lambda i, j: (layer, j, 0)),
        ],
        out_specs=pl.BlockSpec((tm, d), lambda i, j: (i, 0)),
        scratch_shapes=[pltpu.VMEM((tm, d), BF16)],
        compiler_params=_params(("arbitrary", "arbitrary"), 58),
        name="ffn",
    )(x, g, wg, wu, wd)


def _inproj_kernel(x_ref, g_ref, w_ref, z_ref, h_ref):
    @pl.when(pl.program_id(1) == 0)
    def _():
        h_ref[...] = _rms(x_ref[...], g_ref[...]).astype(BF16)

    z_ref[...] = jnp.dot(h_ref[...], w_ref[...], preferred_element_type=F32).astype(BF16)


def _inproj(x, g, w, layer, *, tm=1024, tn=1024):
    s, d = x.shape
    tm = min(tm, s)
    n = w.shape[2]
    return pl.pallas_call(
        _inproj_kernel,
        out_shape=jax.ShapeDtypeStruct((s, n), BF16),
        grid=(s // tm, n // tn),
        in_specs=[
            pl.BlockSpec((tm, d), lambda i, j: (i, 0)),
            pl.BlockSpec((1, d), lambda i, j: (0, 0)),
            pl.BlockSpec((None, d, tn), lambda i, j: (layer, 0, j)),
        ],
        out_specs=pl.BlockSpec((tm, tn), lambda i, j: (i, j)),
        scratch_shapes=[pltpu.VMEM((tm, d), BF16)],
        compiler_params=_params(("arbitrary", "arbitrary"), 48),
        name="in_proj",
    )(x, g, w)


def _half_norm(x, g, lo):
    xx = x * x
    s_lo = jnp.sum(jnp.where(lo, xx, 0.0), axis=-1, keepdims=True)
    s_hi = jnp.sum(jnp.where(lo, 0.0, xx), axis=-1, keepdims=True)
    inv = jnp.where(lo, lax.rsqrt(s_lo * (1.0 / DA_HEAD_DIM) + EPS),
                    lax.rsqrt(s_hi * (1.0 / DA_HEAD_DIM) + EPS))
    return x * inv * g


def _attn_kernel(*refs, n_cast, tq, tk, cw, unroll, lambda_init):
    q_ref, k_ref, v_ref, qg_ref, kg_ref, lq1_ref, lk1_ref, lq2_ref, lk2_ref, sg_ref = refs[:10]
    w_refs = refs[10:10 + n_cast]
    o_ref = refs[10 + n_cast]
    c_refs = refs[11 + n_cast:11 + 2 * n_cast]
    kn_ref, vt_ref, qt_ref, m_ref, acc_ref, s0_ref, s1_ref, mx0_ref, mx1_ref = refs[11 + 2 * n_cast:]

    for w_ref, c_ref in zip(w_refs, c_refs):
        c_ref[...] = w_ref[...].astype(BF16)

    i = pl.program_id(1)
    s_len = k_ref.shape[0]
    lo = lax.broadcasted_iota(jnp.int32, (1, HEAD_W), 1) < DA_HEAD_DIM
    prep_rows = min(512, s_len)

    @pl.when(i == 0)
    def _():
        vt_ref[HEAD_W:ACC_ROWS, :] = jnp.ones((ACC_ROWS - HEAD_W, s_len), BF16)

        def body(c, carry):
            r = pl.multiple_of(c * prep_rows, prep_rows)
            kk = k_ref[pl.ds(r, prep_rows), :].astype(F32)
            kn_ref[pl.ds(r, prep_rows), :] = _half_norm(kk, kg_ref[...], lo).astype(BF16)
            vv = v_ref[pl.ds(r, prep_rows), :].astype(F32)
            vt_ref[0:HEAD_W, pl.ds(r, prep_rows)] = vv.T.astype(BF16)
            return carry
        lax.fori_loop(0, s_len // prep_rows, body, 0)

    dh = DA_HEAD_DIM
    qt = q_ref[...].astype(F32).T
    qq = qt * qt
    qscale = DA_SCALE * LOG2E
    inv1 = lax.rsqrt(jnp.sum(qq[0:dh], axis=0, keepdims=True) * (1.0 / dh) + EPS) * qscale
    inv2 = lax.rsqrt(jnp.sum(qq[dh:HEAD_W], axis=0, keepdims=True) * (1.0 / dh) + EPS) * qscale
    zeros = jnp.zeros((dh, tq), BF16)
    qt_ref[0:dh, 0:tq] = (qt[0:dh] * inv1 * qg_ref[0:dh, :]).astype(BF16)
    qt_ref[dh:HEAD_W, 0:tq] = zeros
    qt_ref[0:dh, tq:2 * tq] = zeros
    qt_ref[dh:HEAD_W, tq:2 * tq] = (qt[dh:HEAD_W] * inv2 * qg_ref[dh:HEAD_W, :]).astype(BF16)
    m_ref[...] = jnp.full(m_ref.shape, NEG_BIG, F32)
    acc_ref[...] = jnp.zeros(acc_ref.shape, F32)

    groups = [(c * cw, (c + 1) * cw) for c in range(2 * tq // cw)]

    def softmax_update(a, b, s, s_max, vt):
        m_prev = m_ref[:, a:b]
        m_new = jnp.maximum(m_prev, s_max)
        alpha = jnp.exp2(m_prev - m_new)
        if s.dtype == BF16:
            p = jnp.exp2(s - m_new.astype(BF16))
        else:
            p = jnp.exp2((s - m_new).astype(BF16))
        acc_ref[:, a:b] = alpha * acc_ref[:, a:b] + jnp.dot(vt, p, preferred_element_type=F32)
        m_ref[:, a:b] = m_new

    def scores(jb, a, b, s_ref, mx_ref):
        r = pl.multiple_of(jb * tk, tk)
        s = jnp.dot(kn_ref[pl.ds(r, tk), :], qt_ref[:, a:b], preferred_element_type=F32)
        s_ref[a // cw] = s.astype(BF16)
        mx_ref[:, a:b] = jnp.max(s, axis=0, keepdims=True)

    def update(jb, a, b, s_ref, mx_ref):
        r = pl.multiple_of(jb * tk, tk)
        softmax_update(a, b, s_ref[a // cw], mx_ref[:, a:b], vt_ref[:, pl.ds(r, tk)])

    n_full = i * (tq // tk)
    for a, b in groups:
        scores(0, a, b, s0_ref, mx0_ref)

    def pair(jb):
        for a, b in groups:
            scores(jb + 1, a, b, s1_ref, mx1_ref)
            update(jb, a, b, s0_ref, mx0_ref)
        for a, b in groups:
            scores(jb + 2, a, b, s0_ref, mx0_ref)
            update(jb + 1, a, b, s1_ref, mx1_ref)

    def unrolled_pairs(t, carry):
        for u in range(unroll):
            pair(2 * unroll * t + 2 * u)
        return carry

    def one_pair(t, carry):
        pair(n_full - n_full % (2 * unroll) + 2 * t)
        return carry

    lax.fori_loop(0, n_full // (2 * unroll), unrolled_pairs, 0)
    lax.fori_loop(0, (n_full % (2 * unroll)) // 2, one_pair, 0)

    def key_rows(d, a):
        q0 = a % tq
        return d * tk, min((d + 1) * tk, q0 + cw), q0

    slots = (s0_ref, s1_ref)
    for d in range(tq // tk):
        r = pl.multiple_of((n_full + d) * tk, tk)
        r_next = pl.multiple_of((n_full + d + 1) * tk, tk)
        for a, b in groups:
            k0, k1, _ = key_rows(d + 1, a)
            if d + 1 < tq // tk and k1 > k0:
                slots[(d + 1) % 2][a // cw, 0:k1 - k0, :] = jnp.dot(
                    kn_ref[pl.ds(r_next, k1 - k0), :], qt_ref[:, a:b], preferred_element_type=F32).astype(BF16)
            k0, k1, q0 = key_rows(d, a)
            if k1 <= k0:
                continue
            s = slots[d % 2][a // cw, 0:k1 - k0, :].astype(F32)
            if k1 - 1 > q0:
                key = k0 + lax.broadcasted_iota(jnp.int32, s.shape, 0)
                qry = q0 + lax.broadcasted_iota(jnp.int32, s.shape, 1)
                s = jnp.where(key <= qry, s, NEG_BIG)
            softmax_update(a, b, s, jnp.max(s, axis=0, keepdims=True), vt_ref[:, pl.ds(r, k1 - k0)])

    o1 = acc_ref[0:HEAD_W, 0:tq] * (1.0 / acc_ref[HEAD_W:HEAD_W + 1, 0:tq])
    o2 = acc_ref[0:HEAD_W, tq:2 * tq] * (1.0 / acc_ref[HEAD_W:HEAD_W + 1, tq:2 * tq])
    lam = (jnp.exp(jnp.sum(lq1_ref[...] * lk1_ref[...], axis=-1, keepdims=True))
           - jnp.exp(jnp.sum(lq2_ref[...] * lk2_ref[...], axis=-1, keepdims=True)) + lambda_init)
    o = o1 - lam * o2
    inv = lax.rsqrt(jnp.mean(o * o, axis=0, keepdims=True) + EPS) * (1.0 - lambda_init)
    o_ref[...] = (o * inv * sg_ref[...]).T.astype(BF16)


def _cast_specs(w, first, n_steps, n_inner):
    nl, r, c = w.shape[0] - first, w.shape[1], w.shape[2]
    per_layer = n_steps // nl
    assert per_layer * nl == n_steps
    for nb_c in (1, 2, 4, 8, 16):
        nb_r = per_layer // nb_c
        if per_layer % nb_c or c % nb_c or r % nb_r:
            continue
        br, bc = r // nb_r, c // nb_c
        if br % 16 == 0 and bc % HEAD_W == 0:
            break
    else:
        raise ValueError(f"no slab split for {w.shape}")

    def index(offset):
        def index_map(h, i):
            t = h * n_inner + i
            rem = t % per_layer
            return (t // per_layer + offset, rem // nb_c, rem % nb_c)
        return index_map

    return (pl.BlockSpec((None, br, bc), index(first)), pl.BlockSpec((None, br, bc), index(0)),
            jax.ShapeDtypeStruct((nl, r, c), BF16))


def _attn(z, qg, kg, lq1, lk1, lq2, lk2, sg, cast=(), *, lambda_init, tq=1024, tk=512, cw=256, unroll=4):
    s = z.shape[0]
    assert s % tq == 0 and tq % (2 * tk) == 0 and tq % cw == 0
    small = lambda w: pl.BlockSpec((1, w), lambda h, i: (0, 0))
    column = pl.BlockSpec((HEAD_W, 1), lambda h, i: (0, 0))
    n_inner = s // tq
    cast_specs = [_cast_specs(w, first, DA_HEADS * n_inner, n_inner) for w, first in cast]
    kern = functools.partial(_attn_kernel, n_cast=len(cast), tq=tq, tk=tk, cw=cw, unroll=unroll,
                             lambda_init=lambda_init)
    out = pl.pallas_call(
        kern,
        out_shape=[jax.ShapeDtypeStruct((s, DA_HEADS * HEAD_W), BF16)] + [c[2] for c in cast_specs],
        grid=(DA_HEADS, n_inner),
        in_specs=[
            pl.BlockSpec((tq, HEAD_W), lambda h, i: (i, QA_BLK + h)),
            pl.BlockSpec((s, HEAD_W), lambda h, i: (0, KA_BLK + h)),
            pl.BlockSpec((s, HEAD_W), lambda h, i: (0, VA_BLK + h)),
            column, small(HEAD_W),
            small(DA_HEAD_DIM), small(DA_HEAD_DIM), small(DA_HEAD_DIM), small(DA_HEAD_DIM),
            column,
        ] + [c[0] for c in cast_specs],
        out_specs=[pl.BlockSpec((tq, HEAD_W), lambda h, i: (i, h))] + [c[1] for c in cast_specs],
        scratch_shapes=[
            pltpu.VMEM((s, HEAD_W), BF16),
            pltpu.VMEM((ACC_ROWS, s), BF16),
            pltpu.VMEM((HEAD_W, 2 * tq), BF16),
            pltpu.VMEM((1, 2 * tq), F32),
            pltpu.VMEM((ACC_ROWS, 2 * tq), F32),
            pltpu.VMEM((2 * tq // cw, tk, cw), BF16),
            pltpu.VMEM((2 * tq // cw, tk, cw), BF16),
            pltpu.VMEM((1, 2 * tq), F32),
            pltpu.VMEM((1, 2 * tq), F32),
        ],
        compiler_params=_params(("arbitrary", "arbitrary"), 58),
        name="diff_attn",
    )(z, z, z, qg, kg, lq1, lk1, lq2, lk2, sg, *[w for w, _ in cast])
    return out[0], out[1:]


def _rope_kernel(pos_ref, invf_ref, cos_ref, sin_ref):
    ang = pos_ref[...].astype(F32) * invf_ref[...]
    lo = lax.broadcasted_iota(jnp.int32, (1, HEAD_W), 1) < HEAD_W // 2
    cos_ref[...] = jnp.cos(ang)
    sin_ref[...] = jnp.where(lo, -1.0, 1.0) * jnp.sin(ang)


def _rope_tables(pos_col, invf, *, tm=1024):
    s = pos_col.shape[0]
    tm = min(tm, s)
    return pl.pallas_call(
        _rope_kernel,
        out_shape=(jax.ShapeDtypeStruct((s, HEAD_W), F32), jax.ShapeDtypeStruct((s, HEAD_W), F32)),
        grid=(s // tm,),
        in_specs=[pl.BlockSpec((tm, 1), lambda i: (i, 0)), pl.BlockSpec((1, HEAD_W), lambda i: (0, 0))],
        out_specs=(pl.BlockSpec((tm, HEAD_W), lambda i: (i, 0)), pl.BlockSpec((tm, HEAD_W), lambda i: (i, 0))),
        compiler_params=_params(("arbitrary",), 32),
        name="rope_tables",
    )(pos_col, invf)


def _ret_kernel(q_ref, k_ref, v_ref, gr_ref, cos_ref, sin_ref, dec_ref, zeta_ref, xi_ref, cd_ref, sg_ref,
                o_ref, st_ref, *, heads):
    @pl.when(pl.program_id(1) == 0)
    def _():
        st_ref[...] = jnp.zeros(st_ref.shape, F32)

    cosf = cos_ref[...]
    sinf = sin_ref[...]
    for hh in range(heads):
        lanes = slice(hh * HEAD_W, (hh + 1) * HEAD_W)
        q = q_ref[:, lanes].astype(F32)
        k = k_ref[:, lanes].astype(F32)
        q = q * cosf + pltpu.roll(q, HEAD_W // 2, 1) * sinf
        k = (k * cosf + pltpu.roll(k, HEAD_W // 2, 1) * sinf) * RET_SCALE
        v = v_ref[:, lanes]
        st = st_ref[hh]

        sc = lax.dot_general(q.astype(BF16), k.astype(BF16), (((1,), (1,)), ((), ())),
                             preferred_element_type=F32) * dec_ref[hh]
        o = jnp.dot(sc.astype(BF16), v, preferred_element_type=F32)
        o = o + jnp.dot((q * xi_ref[hh]).astype(BF16), st.astype(BF16), preferred_element_type=F32)
        kz = (k * zeta_ref[hh]).astype(BF16)
        kv = lax.dot_general(kz, v, (((0,), (0,)), ((), ())), preferred_element_type=F32)
        st_ref[hh] = cd_ref[hh, 0:1, :] * st + kv

        g = gr_ref[:, lanes].astype(F32)
        o_ref[:, lanes] = (_rms(o, sg_ref[...]) * (g * jax.nn.sigmoid(g))).astype(BF16)


def _retention(z, cosf, sinf, dec, zeta, xi, cd, sg, *, chunk, heads=8):
    s = z.shape[0]
    w = heads * HEAD_W
    row = lambda blk: pl.BlockSpec((chunk, w), lambda h, i: (i, blk // heads + h))
    tab = pl.BlockSpec((chunk, HEAD_W), lambda h, i: (i, 0))
    per_head = lambda r, c: pl.BlockSpec((heads, r, c), lambda h, i: (h, 0, 0))
    return pl.pallas_call(
        functools.partial(_ret_kernel, heads=heads),
        out_shape=jax.ShapeDtypeStruct((s, RET_HEADS * HEAD_W), BF16),
        grid=(RET_HEADS // heads, s // chunk),
        in_specs=[
            row(QR_BLK), row(KR_BLK), row(VR_BLK), row(GR_BLK), tab, tab,
            per_head(chunk, chunk), per_head(chunk, HEAD_W), per_head(chunk, HEAD_W), per_head(8, HEAD_W),
            pl.BlockSpec((1, HEAD_W), lambda h, i: (0, 0)),
        ],
        out_specs=pl.BlockSpec((chunk, w), lambda h, i: (i, h)),
        scratch_shapes=[pltpu.VMEM((heads, HEAD_W, HEAD_W), F32)],
        compiler_params=_params(("arbitrary", "arbitrary"), 40),
        name="retention",
    )(z, z, z, z, cosf, sinf, dec, zeta, xi, cd, sg)


def _retention_tables(chunk):
    log_g = jnp.log(1.0 - 2.0 ** (-5.0 - jnp.arange(RET_HEADS, dtype=F32)))
    idx = jnp.arange(chunk, dtype=F32)
    rel = idx[:, None] - idx[None, :]
    dec = jnp.exp(log_g[:, None, None] * jnp.maximum(rel, 0.0)) * (rel >= 0)
    zeta = jnp.exp(log_g[:, None] * (chunk - 1 - idx))
    xi = jnp.exp(log_g[:, None] * (idx + 1.0))
    cd = jnp.exp(log_g * chunk)
    bcast = lambda t: jnp.broadcast_to(t[:, :, None], (RET_HEADS, chunk, HEAD_W))
    return dec, bcast(zeta), bcast(xi), jnp.broadcast_to(cd[:, None, None], (RET_HEADS, 8, HEAD_W))


def _merge_kernel(x_ref, ya_ref, yb_ref, ga0_ref, ga1_ref, gb0_ref, gb1_ref, wa_ref, wb_ref, wo_ref, o_ref):
    ta = jnp.dot(ya_ref[...], wa_ref[...], preferred_element_type=F32)
    tb = jnp.dot(yb_ref[...], wb_ref[...], preferred_element_type=F32)
    ga = jnp.concatenate([ga0_ref[...], ga1_ref[...]], axis=-1).astype(F32)
    gb = jnp.concatenate([gb0_ref[...], gb1_ref[...]], axis=-1).astype(F32)
    merged = jax.nn.sigmoid(ga) * ta + jax.nn.sigmoid(gb) * tb
    o_ref[...] = x_ref[...] + jnp.dot(merged.astype(BF16), wo_ref[...], preferred_element_type=F32)


def _merge(x, ya, yb, z, wa, wb, wo, layer, *, tm=256):
    s, d = x.shape
    gate = lambda blk: pl.BlockSpec((tm, GATE_BLK_W), lambda i: (i, blk))
    resident = lambda w: pl.BlockSpec((None,) + w.shape[1:], lambda i: (layer, 0, 0), pipeline_mode=pl.Buffered(1))
    return pl.pallas_call(
        _merge_kernel,
        out_shape=jax.ShapeDtypeStruct((s, d), F32),
        grid=(s // tm,),
        in_specs=[
            pl.BlockSpec((tm, d), lambda i: (i, 0)),
            pl.BlockSpec((tm, ya.shape[1]), lambda i: (i, 0)),
            pl.BlockSpec((tm, yb.shape[1]), lambda i: (i, 0)),
            gate(GA_BLK), gate(GA_BLK + 1), gate(GB_BLK), gate(GB_BLK + 1),
            resident(wa), resident(wb), resident(wo),
        ],
        out_specs=pl.BlockSpec((tm, d), lambda i: (i, 0)),
        compiler_params=_params(("arbitrary",), 48),
        name="merge_out",
    )(x, ya, yb, z, z, z, z, wa, wb, wo)


def _ple_kernel(x_ref, g_ref, p_ref, wg_ref, wp_ref, o_ref):
    x = x_ref[...]
    h = _rms(x, g_ref[...]).astype(BF16)
    gate = jax.nn.sigmoid(jnp.dot(h, wg_ref[...], preferred_element_type=F32))
    proj = jnp.dot(p_ref[...].astype(BF16), wp_ref[...], preferred_element_type=F32)
    o_ref[...] = x + gate * proj


def _ple(x, g, p, wg, wp, layer, *, tm=512):
    s, d = x.shape
    resident = lambda w: pl.BlockSpec((None,) + w.shape[1:], lambda i: (layer, 0, 0), pipeline_mode=pl.Buffered(1))
    return pl.pallas_call(
        _ple_kernel,
        out_shape=jax.ShapeDtypeStruct((s, d), F32),
        grid=(s // tm,),
        in_specs=[
            pl.BlockSpec((tm, d), lambda i: (i, 0)),
            pl.BlockSpec((1, d), lambda i: (0, 0)),
            pl.BlockSpec((None, tm, p.shape[2]), lambda i: (layer, i, 0)),
            resident(wg), resident(wp),
        ],
        out_specs=pl.BlockSpec((tm, d), lambda i: (i, 0)),
        compiler_params=_params(("arbitrary",), 48),
        name="ple",
    )(x, g, p, wg, wp)


def kernel(x, p, positions, ffn1_norm, ffn1_w_gate, ffn1_w_up, ffn1_w_down, mix_norm, w_in, da_q_norm, da_k_norm, da_lambda_q1, da_lambda_k1, da_lambda_q2, da_lambda_k2, da_sub_norm, ret_sub_norm, w_up_a, w_up_b, w_out, ffn2_norm, ffn2_w_gate, ffn2_w_up, ffn2_w_down, ple_norm, w_ple_gate, w_ple_proj):
    b, s, d = x.shape
    assert b == 1
    depth = w_in.shape[0]
    chunk = min(256, s)

    row = lambda t: t.reshape(1, -1)
    twice = lambda t: jnp.concatenate([t, t]).reshape(1, -1)
    bf = lambda t: t.astype(BF16)

    inv_freq = ROPE_BASE ** (-jnp.arange(0, HEAD_W, 2, dtype=F32) / HEAD_W)
    cosf, sinf = _rope_tables(positions.reshape(s, 1), twice(inv_freq))
    dec, zeta, xi, cd = _retention_tables(chunk)

    early = [bf(w[0:1]) for w in (ffn1_w_gate, ffn1_w_up, ffn1_w_down, w_in)]
    wpp = bf(w_ple_proj)
    late = [(w, 1) for w in (ffn1_w_gate, ffn1_w_up, ffn1_w_down, w_in)] if depth > 1 else []
    rest = [(w, 0) for w in (ffn2_w_gate, ffn2_w_up, ffn2_w_down, w_up_a, w_up_b, w_out, w_ple_gate)]
    p3 = p.reshape(depth, s, -1)

    xc = x.reshape(s, d)
    for i in range(depth):
        lambda_init = 0.8 - 0.6 * math.exp(-0.3 * i)
        f1g, f1u, f1d, win = early if i == 0 else cast_late
        first = 0 if i == 0 else i - 1
        xc = _ffn(xc, row(ffn1_norm[i]), f1g, f1u, f1d, first)
        z = _inproj(xc, row(mix_norm[i]), win, first)
        ya, cast_out = _attn(z, twice(da_q_norm[i]).reshape(-1, 1), twice(da_k_norm[i]), row(da_lambda_q1[i]),
                             row(da_lambda_k1[i]), row(da_lambda_q2[i]), row(da_lambda_k2[i]),
                             da_sub_norm[i].reshape(-1, 1), (late + rest) if i == 0 else (),
                             lambda_init=lambda_init)
        if i == 0:
            cast_late = cast_out[:len(late)]
            f2g, f2u, f2d, wua, wub, wo, wpg = cast_out[len(late):]
        yb = _retention(z, cosf, sinf, dec, zeta, xi, cd, row(ret_sub_norm[i]), chunk=chunk)
        xc = _merge(xc, ya, yb, z, wua, wub, wo, i)
        xc = _ffn(xc, row(ffn2_norm[i]), f2g, f2u, f2d, i)
        xc = _ple(xc, row(ple_norm[i]), p3, wpg, wpp, i)
    return xc.reshape(b, s, d)
```

```python
import functools
import math

import jax
import jax.numpy as jnp
from jax import lax
from jax.experimental import pallas as pl
from jax.experimental.pallas import tpu as pltpu

F32 = jnp.float32
BF16 = jnp.bfloat16

EPS = 1e-6
ROPE_BASE = 10000.0
DA_HEADS = 8
DA_HEAD_DIM = 64
RET_HEADS = 8
HEAD_W = 128
RET_SCALE = HEAD_W ** -0.5
DA_SCALE = DA_HEAD_DIM ** -0.5
NEG_BIG = -1e30
LOG2E = math.log2(math.e)
BF16_SUBLANES = 16
ACC_ROWS = HEAD_W + BF16_SUBLANES

QA_BLK, KA_BLK, VA_BLK, QR_BLK, KR_BLK, VR_BLK, GR_BLK = 0, 8, 16, 24, 32, 40, 48
GATE_BLK_W = 1024
GA_BLK, GB_BLK = 7, 9

MIB = 1024 * 1024


def _params(sem, vmem_mib, flags=None):
    return pltpu.CompilerParams(dimension_semantics=sem, vmem_limit_bytes=vmem_mib * MIB, flags=flags)


def _rms(x, g):
    return x * lax.rsqrt(jnp.mean(x * x, axis=-1, keepdims=True) + EPS) * g


def _ffn_kernel(x_ref, g_ref, wg_ref, wu_ref, wd_ref, *rest, n_cast):
    w_refs, o_ref, c_refs, h_ref = rest[:n_cast], rest[n_cast], rest[n_cast + 1:-1], rest[-1]
    for w_ref, c_ref in zip(w_refs, c_refs):
        c_ref[...] = w_ref[...].astype(BF16)
    j = pl.program_id(1)

    @pl.when(j == 0)
    def _():
        x = x_ref[...]
        h_ref[...] = _rms(x, g_ref[...]).astype(BF16)
        o_ref[...] = x

    h = h_ref[...]
    gate = jnp.dot(h, wg_ref[...], preferred_element_type=F32)
    up = jnp.dot(h, wu_ref[...], preferred_element_type=F32)
    act = (gate * jax.nn.sigmoid(gate) * up * 0.5).astype(BF16)
    o_ref[...] += jnp.dot(act, wd_ref[...], preferred_element_type=F32)


def _ffn(x, g, wg, wu, wd, layer, cast=(), *, tm=1024, tf=512):
    s, d = x.shape
    tm = min(tm, s)
    f = wg.shape[2]
    grid = (s // tm, f // tf)
    cast_specs = [_cast_specs(w, first, count, grid[0] * grid[1], grid[1]) for w, first, count in cast]
    out = pl.pallas_call(
        functools.partial(_ffn_kernel, n_cast=len(cast)),
        out_shape=[jax.ShapeDtypeStruct((s, d), F32)] + [c[2] for c in cast_specs],
        grid=grid,
        in_specs=[
            pl.BlockSpec((tm, d), lambda i, j: (i, 0)),
            pl.BlockSpec((1, d), lambda i, j: (0, 0)),
            pl.BlockSpec((None, d, tf), lambda i, j: (layer, 0, j)),
            pl.BlockSpec((None, d, tf), lambda i, j: (layer, 0, j)),
            pl.BlockSpec((None, tf, d), lambda i, j: (layer, j, 0)),
        ] + [c[0] for c in cast_specs],
        out_specs=[pl.BlockSpec((tm, d), lambda i, j: (i, 0))] + [c[1] for c in cast_specs],
        scratch_shapes=[pltpu.VMEM((tm, d), BF16)],
        compiler_params=_params(("arbitrary", "arbitrary"), 58),
        name="ffn",
    )(x, g, wg, wu, wd, *[c[0] for c in cast])
    return out[0], out[1:]


def _inproj_kernel(x_ref, g_ref, w_ref, z_ref, h_ref):
    @pl.when(pl.program_id(1) == 0)
    def _():
        h_ref[...] = _rms(x_ref[...], g_ref[...]).astype(BF16)

    z_ref[...] = jnp.dot(h_ref[...], w_ref[...], preferred_element_type=F32).astype(BF16)


def _inproj(x, g, w, layer, *, tm=1024, tn=1024):
    s, d = x.shape
    tm = min(tm, s)
    n = w.shape[2]
    return pl.pallas_call(
        _inproj_kernel,
        out_shape=jax.ShapeDtypeStruct((s, n), BF16),
        grid=(s // tm, n // tn),
        in_specs=[
            pl.BlockSpec((tm, d), lambda i, j: (i, 0)),
            pl.BlockSpec((1, d), lambda i, j: (0, 0)),
            pl.BlockSpec((None, d, tn), lambda i, j: (layer, 0, j)),
        ],
        out_specs=pl.BlockSpec((tm, tn), lambda i, j: (i, j)),
        scratch_shapes=[pltpu.VMEM((tm, d), BF16)],
        compiler_params=_params(("arbitrary", "arbitrary"), 48),
        name="in_proj",
    )(x, g, w)


def _half_norm(x, g, lo):
    xx = x * x
    s_lo = jnp.sum(jnp.where(lo, xx, 0.0), axis=-1, keepdims=True)
    s_hi = jnp.sum(jnp.where(lo, 0.0, xx), axis=-1, keepdims=True)
    inv = jnp.where(lo, lax.rsqrt(s_lo * (1.0 / DA_HEAD_DIM) + EPS),
                    lax.rsqrt(s_hi * (1.0 / DA_HEAD_DIM) + EPS))
    return x * inv * g


def _attn_kernel(*refs, n_cast, tq, tk, cw, unroll, lambda_init):
    q_ref, k_ref, v_ref, qg_ref, kg_ref, lq1_ref, lk1_ref, lq2_ref, lk2_ref, sg_ref = refs[:10]
    w_refs = refs[10:10 + n_cast]
    o_ref = refs[10 + n_cast]
    c_refs = refs[11 + n_cast:11 + 2 * n_cast]
    kn_ref, vt_ref, qt_ref, m_ref, acc_ref, s0_ref, s1_ref, mx0_ref, mx1_ref = refs[11 + 2 * n_cast:]

    for w_ref, c_ref in zip(w_refs, c_refs):
        c_ref[...] = w_ref[...].astype(BF16)

    i = pl.program_id(1)
    s_len = k_ref.shape[0]
    lo = lax.broadcasted_iota(jnp.int32, (1, HEAD_W), 1) < DA_HEAD_DIM
    prep_rows = min(512, s_len)

    @pl.when(i == 0)
    def _():
        vt_ref[HEAD_W:ACC_ROWS, :] = jnp.ones((ACC_ROWS - HEAD_W, s_len), BF16)

        def body(c, carry):
            r = pl.multiple_of(c * prep_rows, prep_rows)
            kk = k_ref[pl.ds(r, prep_rows), :].astype(F32)
            kn_ref[pl.ds(r, prep_rows), :] = _half_norm(kk, kg_ref[...], lo).astype(BF16)
            vv = v_ref[pl.ds(r, prep_rows), :].astype(F32)
            vt_ref[0:HEAD_W, pl.ds(r, prep_rows)] = vv.T.astype(BF16)
            return carry
        lax.fori_loop(0, s_len // prep_rows, body, 0)

    dh = DA_HEAD_DIM
    qt = q_ref[...].astype(F32).T
    qq = qt * qt
    qscale = DA_SCALE * LOG2E
    inv1 = lax.rsqrt(jnp.sum(qq[0:dh], axis=0, keepdims=True) * (1.0 / dh) + EPS) * qscale
    inv2 = lax.rsqrt(jnp.sum(qq[dh:HEAD_W], axis=0, keepdims=True) * (1.0 / dh) + EPS) * qscale
    zeros = jnp.zeros((dh, tq), BF16)
    qt_ref[0:dh, 0:tq] = (qt[0:dh] * inv1 * qg_ref[0:dh, :]).astype(BF16)
    qt_ref[dh:HEAD_W, 0:tq] = zeros
    qt_ref[0:dh, tq:2 * tq] = zeros
    qt_ref[dh:HEAD_W, tq:2 * tq] = (qt[dh:HEAD_W] * inv2 * qg_ref[dh:HEAD_W, :]).astype(BF16)
    m_ref[...] = jnp.full(m_ref.shape, NEG_BIG, F32)
    acc_ref[...] = jnp.zeros(acc_ref.shape, F32)

    groups = [(c * cw, (c + 1) * cw) for c in range(2 * tq // cw)]

    def softmax_update(a, b, s, s_max, vt):
        m_prev = m_ref[:, a:b]
        m_new = jnp.maximum(m_prev, s_max)
        alpha = jnp.exp2(m_prev - m_new)
        if s.dtype == BF16:
            p = jnp.exp2(s - m_new.astype(BF16))
        else:
            p = jnp.exp2((s - m_new).astype(BF16))
        acc_ref[:, a:b] = alpha * acc_ref[:, a:b] + jnp.dot(vt, p, preferred_element_type=F32)
        m_ref[:, a:b] = m_new

    def scores(jb, a, b, s_ref, mx_ref):
        r = pl.multiple_of(jb * tk, tk)
        s = jnp.dot(kn_ref[pl.ds(r, tk), :], qt_ref[:, a:b], preferred_element_type=F32)
        s_ref[a // cw] = s.astype(BF16)
        mx_ref[:, a:b] = jnp.max(s, axis=0, keepdims=True)

    def update(jb, a, b, s_ref, mx_ref):
        r = pl.multiple_of(jb * tk, tk)
        softmax_update(a, b, s_ref[a // cw], mx_ref[:, a:b], vt_ref[:, pl.ds(r, tk)])

    n_full = i * (tq // tk)
    for a, b in groups:
        scores(0, a, b, s0_ref, mx0_ref)

    def pair(jb):
        for a, b in groups:
            scores(jb + 1, a, b, s1_ref, mx1_ref)
            update(jb, a, b, s0_ref, mx0_ref)
        for a, b in groups:
            scores(jb + 2, a, b, s0_ref, mx0_ref)
            update(jb + 1, a, b, s1_ref, mx1_ref)

    def unrolled_pairs(t, carry):
        for u in range(unroll):
            pair(2 * unroll * t + 2 * u)
        return carry

    def one_pair(t, carry):
        pair(n_full - n_full % (2 * unroll) + 2 * t)
        return carry

    lax.fori_loop(0, n_full // (2 * unroll), unrolled_pairs, 0)
    lax.fori_loop(0, (n_full % (2 * unroll)) // 2, one_pair, 0)

    def key_rows(d, a):
        q0 = a % tq
        return d * tk, min((d + 1) * tk, q0 + cw), q0

    slots = (s0_ref, s1_ref)
    for d in range(tq // tk):
        r = pl.multiple_of((n_full + d) * tk, tk)
        r_next = pl.multiple_of((n_full + d + 1) * tk, tk)
        for a, b in groups:
            k0, k1, _ = key_rows(d + 1, a)
            if d + 1 < tq // tk and k1 > k0:
                slots[(d + 1) % 2][a // cw, 0:k1 - k0, :] = jnp.dot(
                    kn_ref[pl.ds(r_next, k1 - k0), :], qt_ref[:, a:b], preferred_element_type=F32).astype(BF16)
            k0, k1, q0 = key_rows(d, a)
            if k1 <= k0:
                continue
            s = slots[d % 2][a // cw, 0:k1 - k0, :].astype(F32)
            if k1 - 1 > q0:
                key = k0 + lax.broadcasted_iota(jnp.int32, s.shape, 0)
                qry = q0 + lax.broadcasted_iota(jnp.int32, s.shape, 1)
                s = jnp.where(key <= qry, s, NEG_BIG)
            softmax_update(a, b, s, jnp.max(s, axis=0, keepdims=True), vt_ref[:, pl.ds(r, k1 - k0)])

    o1 = acc_ref[0:HEAD_W, 0:tq] * (1.0 / acc_ref[HEAD_W:HEAD_W + 1, 0:tq])
    o2 = acc_ref[0:HEAD_W, tq:2 * tq] * (1.0 / acc_ref[HEAD_W:HEAD_W + 1, tq:2 * tq])
    lam = (jnp.exp(jnp.sum(lq1_ref[...] * lk1_ref[...], axis=-1, keepdims=True))
           - jnp.exp(jnp.sum(lq2_ref[...] * lk2_ref[...], axis=-1, keepdims=True)) + lambda_init)
    o = o1 - lam * o2
    inv = lax.rsqrt(jnp.mean(o * o, axis=0, keepdims=True) + EPS) * (1.0 - lambda_init)
    o_ref[...] = (o * inv * sg_ref[...]).T.astype(BF16)


def _cast_specs(w, first, count, n_steps, n_inner):
    r, c = w.shape[1], w.shape[2]
    per_layer = n_steps // count
    assert per_layer * count == n_steps and first + count <= w.shape[0]
    for nb_c in range(1, per_layer + 1):
        nb_r = per_layer // nb_c
        if per_layer % nb_c or c % nb_c or r % nb_r:
            continue
        br, bc = r // nb_r, c // nb_c
        if br % BF16_SUBLANES == 0 and bc % HEAD_W == 0:
            break
    else:
        raise ValueError(f"no slab split for {w.shape}")

    def index(offset):
        def index_map(h, i):
            t = h * n_inner + i
            rem = t % per_layer
            return (t // per_layer + offset, rem // nb_c, rem % nb_c)
        return index_map

    return (pl.BlockSpec((None, br, bc), index(first)), pl.BlockSpec((None, br, bc), index(0)),
            jax.ShapeDtypeStruct((count, r, c), BF16))


def _attn(z, qg, kg, lq1, lk1, lq2, lk2, sg, cast=(), *, lambda_init, tq=1024, tk=512, cw=256, unroll=2):
    s = z.shape[0]
    assert s % tq == 0 and tq % (2 * tk) == 0 and tq % cw == 0
    small = lambda w: pl.BlockSpec((1, w), lambda h, i: (0, 0))
    column = pl.BlockSpec((HEAD_W, 1), lambda h, i: (0, 0))
    n_inner = s // tq
    cast_specs = [_cast_specs(w, first, count, DA_HEADS * n_inner, n_inner) for w, first, count in cast]
    kern = functools.partial(_attn_kernel, n_cast=len(cast), tq=tq, tk=tk, cw=cw, unroll=unroll,
                             lambda_init=lambda_init)
    out = pl.pallas_call(
        kern,
        out_shape=[jax.ShapeDtypeStruct((s, DA_HEADS * HEAD_W), BF16)] + [c[2] for c in cast_specs],
        grid=(DA_HEADS, n_inner),
        in_specs=[
            pl.BlockSpec((tq, HEAD_W), lambda h, i: (i, QA_BLK + h)),
            pl.BlockSpec((s, HEAD_W), lambda h, i: (0, KA_BLK + h)),
            pl.BlockSpec((s, HEAD_W), lambda h, i: (0, VA_BLK + h)),
            column, small(HEAD_W),
            small(DA_HEAD_DIM), small(DA_HEAD_DIM), small(DA_HEAD_DIM), small(DA_HEAD_DIM),
            column,
        ] + [c[0] for c in cast_specs],
        out_specs=[pl.BlockSpec((tq, HEAD_W), lambda h, i: (i, h))] + [c[1] for c in cast_specs],
        scratch_shapes=[
            pltpu.VMEM((s, HEAD_W), BF16),
            pltpu.VMEM((ACC_ROWS, s), BF16),
            pltpu.VMEM((HEAD_W, 2 * tq), BF16),
            pltpu.VMEM((1, 2 * tq), F32),
            pltpu.VMEM((ACC_ROWS, 2 * tq), F32),
            pltpu.VMEM((2 * tq // cw, tk, cw), BF16),
            pltpu.VMEM((2 * tq // cw, tk, cw), BF16),
            pltpu.VMEM((1, 2 * tq), F32),
            pltpu.VMEM((1, 2 * tq), F32),
        ],
        compiler_params=_params(("arbitrary", "arbitrary"), 58),
        name="diff_attn",
    )(z, z, z, qg, kg, lq1, lk1, lq2, lk2, sg, *[c[0] for c in cast])
    return out[0], out[1:]


def _rope_kernel(pos_ref, invf_ref, cos_ref, sin_ref):
    ang = pos_ref[...].astype(F32) * invf_ref[...]
    lo = lax.broadcasted_iota(jnp.int32, (1, HEAD_W), 1) < HEAD_W // 2
    cos_ref[...] = jnp.cos(ang)
    sin_ref[...] = jnp.where(lo, -1.0, 1.0) * jnp.sin(ang)


def _rope_tables(pos_col, invf, *, tm=1024):
    s = pos_col.shape[0]
    tm = min(tm, s)
    return pl.pallas_call(
        _rope_kernel,
        out_shape=(jax.ShapeDtypeStruct((s, HEAD_W), F32), jax.ShapeDtypeStruct((s, HEAD_W), F32)),
        grid=(s // tm,),
        in_specs=[pl.BlockSpec((tm, 1), lambda i: (i, 0)), pl.BlockSpec((1, HEAD_W), lambda i: (0, 0))],
        out_specs=(pl.BlockSpec((tm, HEAD_W), lambda i: (i, 0)), pl.BlockSpec((tm, HEAD_W), lambda i: (i, 0))),
        compiler_params=_params(("arbitrary",), 32),
        name="rope_tables",
    )(pos_col, invf)


def _ret_kernel(q_ref, k_ref, v_ref, gr_ref, cos_ref, sin_ref, dec_ref, zeta_ref, xi_ref, cd_ref, sg_ref,
                o_ref, st_ref, *, heads):
    @pl.when(pl.program_id(1) == 0)
    def _():
        st_ref[...] = jnp.zeros(st_ref.shape, F32)

    cosf = cos_ref[...]
    sinf = sin_ref[...]
    for hh in range(heads):
        lanes = slice(hh * HEAD_W, (hh + 1) * HEAD_W)
        q = q_ref[:, lanes].astype(F32)
        k = k_ref[:, lanes].astype(F32)
        q = q * cosf + pltpu.roll(q, HEAD_W // 2, 1) * sinf
        k = (k * cosf + pltpu.roll(k, HEAD_W // 2, 1) * sinf) * RET_SCALE
        v = v_ref[:, lanes]
        st = st_ref[hh]

        sc = lax.dot_general(q.astype(BF16), k.astype(BF16), (((1,), (1,)), ((), ())),
                             preferred_element_type=F32) * dec_ref[hh]
        o = jnp.dot(sc.astype(BF16), v, preferred_element_type=F32)
        o = o + jnp.dot((q * xi_ref[hh]).astype(BF16), st.astype(BF16), preferred_element_type=F32)
        kz = (k * zeta_ref[hh]).astype(BF16)
        kv = lax.dot_general(kz, v, (((0,), (0,)), ((), ())), preferred_element_type=F32)
        st_ref[hh] = cd_ref[hh, 0:1, :] * st + kv

        g = gr_ref[:, lanes].astype(F32)
        o_ref[:, lanes] = (_rms(o, sg_ref[...]) * (g * jax.nn.sigmoid(g))).astype(BF16)


def _retention(z, cosf, sinf, dec, zeta, xi, cd, sg, *, chunk, heads=8):
    s = z.shape[0]
    w = heads * HEAD_W
    row = lambda blk: pl.BlockSpec((chunk, w), lambda h, i: (i, blk // heads + h))
    tab = pl.BlockSpec((chunk, HEAD_W), lambda h, i: (i, 0))
    per_head = lambda r, c: pl.BlockSpec((heads, r, c), lambda h, i: (h, 0, 0))
    return pl.pallas_call(
        functools.partial(_ret_kernel, heads=heads),
        out_shape=jax.ShapeDtypeStruct((s, RET_HEADS * HEAD_W), BF16),
        grid=(RET_HEADS // heads, s // chunk),
        in_specs=[
            row(QR_BLK), row(KR_BLK), row(VR_BLK), row(GR_BLK), tab, tab,
            per_head(chunk, chunk), per_head(chunk, HEAD_W), per_head(chunk, HEAD_W), per_head(8, HEAD_W),
            pl.BlockSpec((1, HEAD_W), lambda h, i: (0, 0)),
        ],
        out_specs=pl.BlockSpec((chunk, w), lambda h, i: (i, h)),
        scratch_shapes=[pltpu.VMEM((heads, HEAD_W, HEAD_W), F32)],
        compiler_params=_params(("arbitrary", "arbitrary"), 40),
        name="retention",
    )(z, z, z, z, cosf, sinf, dec, zeta, xi, cd, sg)


def _retention_tables(chunk):
    log_g = jnp.log(1.0 - 2.0 ** (-5.0 - jnp.arange(RET_HEADS, dtype=F32)))
    idx = jnp.arange(chunk, dtype=F32)
    rel = idx[:, None] - idx[None, :]
    dec = jnp.exp(log_g[:, None, None] * jnp.maximum(rel, 0.0)) * (rel >= 0)
    zeta = jnp.exp(log_g[:, None] * (chunk - 1 - idx))
    xi = jnp.exp(log_g[:, None] * (idx + 1.0))
    cd = jnp.exp(log_g * chunk)
    bcast = lambda t: jnp.broadcast_to(t[:, :, None], (RET_HEADS, chunk, HEAD_W))
    return dec, bcast(zeta), bcast(xi), jnp.broadcast_to(cd[:, None, None], (RET_HEADS, 8, HEAD_W))


def _merge_kernel(x_ref, ya_ref, yb_ref, ga0_ref, ga1_ref, gb0_ref, gb1_ref, wa_ref, wb_ref, wo_ref, o_ref):
    ta = jnp.dot(ya_ref[...], wa_ref[...], preferred_element_type=F32)
    tb = jnp.dot(yb_ref[...], wb_ref[...], preferred_element_type=F32)
    ga = jnp.concatenate([ga0_ref[...], ga1_ref[...]], axis=-1).astype(F32)
    gb = jnp.concatenate([gb0_ref[...], gb1_ref[...]], axis=-1).astype(F32)
    merged = jax.nn.sigmoid(ga) * ta + jax.nn.sigmoid(gb) * tb
    o_ref[...] = x_ref[...] + jnp.dot(merged.astype(BF16), wo_ref[...], preferred_element_type=F32)


def _merge(x, ya, yb, z, wa, wb, wo, layer, *, tm=256):
    s, d = x.shape
    gate = lambda blk: pl.BlockSpec((tm, GATE_BLK_W), lambda i: (i, blk))
    resident = lambda w: pl.BlockSpec((None,) + w.shape[1:], lambda i: (layer, 0, 0), pipeline_mode=pl.Buffered(1))
    return pl.pallas_call(
        _merge_kernel,
        out_shape=jax.ShapeDtypeStruct((s, d), F32),
        grid=(s // tm,),
        in_specs=[
            pl.BlockSpec((tm, d), lambda i: (i, 0)),
            pl.BlockSpec((tm, ya.shape[1]), lambda i: (i, 0)),
            pl.BlockSpec((tm, yb.shape[1]), lambda i: (i, 0)),
            gate(GA_BLK), gate(GA_BLK + 1), gate(GB_BLK), gate(GB_BLK + 1),
            resident(wa), resident(wb), resident(wo),
        ],
        out_specs=pl.BlockSpec((tm, d), lambda i: (i, 0)),
        compiler_params=_params(("arbitrary",), 48),
        name="merge_out",
    )(x, ya, yb, z, z, z, z, wa, wb, wo)


def _ple_kernel(x_ref, g_ref, p_ref, wg_ref, wp_ref, o_ref):
    x = x_ref[...]
    h = _rms(x, g_ref[...]).astype(BF16)
    gate = jax.nn.sigmoid(jnp.dot(h, wg_ref[...], preferred_element_type=F32))
    proj = jnp.dot(p_ref[...].astype(BF16), wp_ref[...], preferred_element_type=F32)
    o_ref[...] = x + gate * proj


def _ple(x, g, p, wg, wp, layer, *, tm=512):
    s, d = x.shape
    resident = lambda w: pl.BlockSpec((None,) + w.shape[1:], lambda i: (layer, 0, 0), pipeline_mode=pl.Buffered(1))
    return pl.pallas_call(
        _ple_kernel,
        out_shape=jax.ShapeDtypeStruct((s, d), F32),
        grid=(s // tm,),
        in_specs=[
            pl.BlockSpec((tm, d), lambda i: (i, 0)),
            pl.BlockSpec((1, d), lambda i: (0, 0)),
            pl.BlockSpec((None, tm, p.shape[2]), lambda i: (layer, i, 0)),
            resident(wg), resident(wp),
        ],
        out_specs=pl.BlockSpec((tm, d), lambda i: (i, 0)),
        compiler_params=_params(("arbitrary",), 48),
        name="ple",
    )(x, g, p, wg, wp)


def kernel(x, p, positions, ffn1_norm, ffn1_w_gate, ffn1_w_up, ffn1_w_down, mix_norm, w_in, da_q_norm, da_k_norm, da_lambda_q1, da_lambda_k1, da_lambda_q2, da_lambda_k2, da_sub_norm, ret_sub_norm, w_up_a, w_up_b, w_out, ffn2_norm, ffn2_w_gate, ffn2_w_up, ffn2_w_down, ple_norm, w_ple_gate, w_ple_proj):
    b, s, d = x.shape
    assert b == 1
    depth = w_in.shape[0]
    chunk = min(256, s)

    row = lambda t: t.reshape(1, -1)
    twice = lambda t: jnp.concatenate([t, t]).reshape(1, -1)
    bf = lambda t: t.astype(BF16)

    inv_freq = ROPE_BASE ** (-jnp.arange(0, HEAD_W, 2, dtype=F32) / HEAD_W)
    cosf, sinf = _rope_tables(positions.reshape(s, 1), twice(inv_freq))
    dec, zeta, xi, cd = _retention_tables(chunk)

    early = [bf(w[0:1]) for w in (ffn1_w_gate, ffn1_w_up, ffn1_w_down)]
    wpp = bf(w_ple_proj)
    late = [(w, 1, depth - 1) for w in (ffn1_w_gate, ffn1_w_up, ffn1_w_down, w_in)] if depth > 1 else []
    rest = [(w, 0, depth) for w in (ffn2_w_gate, ffn2_w_up, ffn2_w_down, w_up_a, w_up_b, w_out, w_ple_gate)]
    p3 = p.reshape(depth, s, -1)

    xc = x.reshape(s, d)
    for i in range(depth):
        lambda_init = 0.8 - 0.6 * math.exp(-0.3 * i)
        first = 0 if i == 0 else i - 1
        if i == 0:
            xc, (win,) = _ffn(xc, row(ffn1_norm[i]), *early, first, [(w_in, 0, 1)])
        else:
            f1g, f1u, f1d, win = cast_late
            xc, _ = _ffn(xc, row(ffn1_norm[i]), f1g, f1u, f1d, first)
        z = _inproj(xc, row(mix_norm[i]), win, first)
        ya, cast_out = _attn(z, twice(da_q_norm[i]).reshape(-1, 1), twice(da_k_norm[i]), row(da_lambda_q1[i]),
                             row(da_lambda_k1[i]), row(da_lambda_q2[i]), row(da_lambda_k2[i]),
                             da_sub_norm[i].reshape(-1, 1), (late + rest) if i == 0 else (),
                             lambda_init=lambda_init)
        if i == 0:
            cast_late = cast_out[:len(late)]
            f2g, f2u, f2d, wua, wub, wo, wpg = cast_out[len(late):]
        yb = _retention(z, cosf, sinf, dec, zeta, xi, cd, row(ret_sub_norm[i]), chunk=chunk)
        xc = _merge(xc, ya, yb, z, wua, wub, wo, i)
        xc, _ = _ffn(xc, row(ffn2_norm[i]), f2g, f2u, f2d, i)
        xc = _ple(xc, row(ple_norm[i]), p3, wpg, wpp, i)
    return xc.reshape(b, s, d)
```

```python
import functools
import math

import jax
import jax.numpy as jnp
from jax import lax
from jax.experimental import pallas as pl
from jax.experimental.pallas import tpu as pltpu

F32 = jnp.float32
BF16 = jnp.bfloat16

EPS = 1e-6
ROPE_BASE = 10000.0
DA_HEADS = 8
DA_HEAD_DIM = 64
RET_HEADS = 8
HEAD_W = 128
RET_SCALE = HEAD_W ** -0.5
DA_SCALE = DA_HEAD_DIM ** -0.5
NEG_BIG = -1e30
LOG2E = math.log2(math.e)
BF16_SUBLANES = 16
ACC_ROWS = HEAD_W + BF16_SUBLANES

QA_BLK, KA_BLK, VA_BLK, QR_BLK, KR_BLK, VR_BLK, GR_BLK = 0, 8, 16, 24, 32, 40, 48
GATE_BLK_W = 1024
GA_BLK, GB_BLK = 7, 9

MIB = 1024 * 1024


def _params(sem, vmem_mib, flags=None):
    return pltpu.CompilerParams(dimension_semantics=sem, vmem_limit_bytes=vmem_mib * MIB, flags=flags)


def _rms(x, g):
    return x * lax.rsqrt(jnp.mean(x * x, axis=-1, keepdims=True) + EPS) * g


def _ffn_kernel(x_ref, g_ref, wg_ref, wu_ref, wd_ref, *rest, n_cast):
    w_refs, o_ref, c_refs, h_ref = rest[:n_cast], rest[n_cast], rest[n_cast + 1:-1], rest[-1]
    for w_ref, c_ref in zip(w_refs, c_refs):
        c_ref[...] = w_ref[...].astype(BF16)
    j = pl.program_id(1)

    @pl.when(j == 0)
    def _():
        x = x_ref[...]
        h_ref[...] = _rms(x, g_ref[...]).astype(BF16)
        o_ref[...] = x

    h = h_ref[...]
    gate = jnp.dot(h, wg_ref[...], preferred_element_type=F32)
    up = jnp.dot(h, wu_ref[...], preferred_element_type=F32)
    act = (gate * jax.nn.sigmoid(gate) * up * 0.5).astype(BF16)
    o_ref[...] += jnp.dot(act, wd_ref[...], preferred_element_type=F32)


def _ffn(x, g, wg, wu, wd, layer, cast=(), *, tm=1024, tf=512):
    s, d = x.shape
    tm = min(tm, s)
    f = wg.shape[2]
    grid = (s // tm, f // tf)
    cast_specs = [_cast_specs(w, first, count, grid[0] * grid[1], grid[1]) for w, first, count in cast]
    out = pl.pallas_call(
        functools.partial(_ffn_kernel, n_cast=len(cast)),
        out_shape=[jax.ShapeDtypeStruct((s, d), F32)] + [c[2] for c in cast_specs],
        grid=grid,
        in_specs=[
            pl.BlockSpec((tm, d), lambda i, j: (i, 0)),
            pl.BlockSpec((1, d), lambda i, j: (0, 0)),
            pl.BlockSpec((None, d, tf), lambda i, j: (layer, 0, j)),
            pl.BlockSpec((None, d, tf), lambda i, j: (layer, 0, j)),
            pl.BlockSpec((None, tf, d), lambda i, j: (layer, j, 0)),
        ] + [c[0] for c in cast_specs],
        out_specs=[pl.BlockSpec((tm, d), lambda i, j: (i, 0))] + [c[1] for c in cast_specs],
        scratch_shapes=[pltpu.VMEM((tm, d), BF16)],
        compiler_params=_params(("arbitrary", "arbitrary"), 58),
        name="ffn",
    )(x, g, wg, wu, wd, *[c[0] for c in cast])
    return out[0], out[1:]


def _inproj_kernel(x_ref, g_ref, w_ref, z_ref, h_ref):
    @pl.when(pl.program_id(1) == 0)
    def _():
        h_ref[...] = _rms(x_ref[...], g_ref[...]).astype(BF16)

    z_ref[...] = jnp.dot(h_ref[...], w_ref[...], preferred_element_type=F32).astype(BF16)


def _inproj(x, g, w, layer, *, tm=1024, tn=1024):
    s, d = x.shape
    tm = min(tm, s)
    n = w.shape[2]
    return pl.pallas_call(
        _inproj_kernel,
        out_shape=jax.ShapeDtypeStruct((s, n), BF16),
        grid=(s // tm, n // tn),
        in_specs=[
            pl.BlockSpec((tm, d), lambda i, j: (i, 0)),
            pl.BlockSpec((1, d), lambda i, j: (0, 0)),
            pl.BlockSpec((None, d, tn), lambda i, j: (layer, 0, j)),
        ],
        out_specs=pl.BlockSpec((tm, tn), lambda i, j: (i, j)),
        scratch_shapes=[pltpu.VMEM((tm, d), BF16)],
        compiler_params=_params(("arbitrary", "arbitrary"), 48),
        name="in_proj",
    )(x, g, w)


def _half_norm(x, g, same_half):
    xx = x * x
    hi = xx.astype(BF16)
    lo = (xx - hi.astype(F32)).astype(BF16)
    ss = (jnp.dot(hi, same_half, preferred_element_type=F32)
          + jnp.dot(lo, same_half, preferred_element_type=F32))
    return x * lax.rsqrt(ss * (1.0 / DA_HEAD_DIM) + EPS) * g


def _attn_kernel(*refs, n_cast, tq, tk, cw, unroll, lambda_init):
    q_ref, k_ref, v_ref, qg_ref, kg_ref, lq1_ref, lk1_ref, lq2_ref, lk2_ref, sg_ref = refs[:10]
    w_refs = refs[10:10 + n_cast]
    o_ref = refs[10 + n_cast]
    c_refs = refs[11 + n_cast:11 + 2 * n_cast]
    kn_ref, vt_ref, qt_ref, m_ref, acc_ref, s0_ref, s1_ref, mx0_ref, mx1_ref = refs[11 + 2 * n_cast:]

    for w_ref, c_ref in zip(w_refs, c_refs):
        c_ref[...] = w_ref[...].astype(BF16)

    i = pl.program_id(1)
    s_len = k_ref.shape[0]
    prep_rows = min(512, s_len)

    @pl.when(i == 0)
    def _():
        vt_ref[HEAD_W:ACC_ROWS, :] = jnp.ones((ACC_ROWS - HEAD_W, s_len), BF16)
        half = lambda axis: lax.broadcasted_iota(jnp.int32, (HEAD_W, HEAD_W), axis) < DA_HEAD_DIM
        same_half = (half(0) == half(1)).astype(BF16)

        def body(c, carry):
            r = pl.multiple_of(c * prep_rows, prep_rows)
            kk = k_ref[pl.ds(r, prep_rows), :].astype(F32)
            kn_ref[pl.ds(r, prep_rows), :] = _half_norm(kk, kg_ref[...], same_half).astype(BF16)
            vv = v_ref[pl.ds(r, prep_rows), :].astype(F32)
            vt_ref[0:HEAD_W, pl.ds(r, prep_rows)] = vv.T.astype(BF16)
            return carry
        lax.fori_loop(0, s_len // prep_rows, body, 0)

    dh = DA_HEAD_DIM
    qt = q_ref[...].astype(F32).T
    qq = qt * qt
    qscale = DA_SCALE * LOG2E
    inv1 = lax.rsqrt(jnp.sum(qq[0:dh], axis=0, keepdims=True) * (1.0 / dh) + EPS) * qscale
    inv2 = lax.rsqrt(jnp.sum(qq[dh:HEAD_W], axis=0, keepdims=True) * (1.0 / dh) + EPS) * qscale
    zeros = jnp.zeros((dh, tq), BF16)
    qt_ref[0:dh, 0:tq] = (qt[0:dh] * inv1 * qg_ref[0:dh, :]).astype(BF16)
    qt_ref[dh:HEAD_W, 0:tq] = zeros
    qt_ref[0:dh, tq:2 * tq] = zeros
    qt_ref[dh:HEAD_W, tq:2 * tq] = (qt[dh:HEAD_W] * inv2 * qg_ref[dh:HEAD_W, :]).astype(BF16)
    m_ref[...] = jnp.full(m_ref.shape, NEG_BIG, F32)
    acc_ref[...] = jnp.zeros(acc_ref.shape, F32)

    groups = [(c * cw, (c + 1) * cw) for c in range(2 * tq // cw)]

    def softmax_update(a, b, s, s_max, vt):
        m_prev = m_ref[:, a:b]
        m_new = jnp.maximum(m_prev, s_max)
        alpha = jnp.exp2(m_prev - m_new)
        if s.dtype == BF16:
            p = jnp.exp2(s - m_new.astype(BF16))
        else:
            p = jnp.exp2((s - m_new).astype(BF16))
        acc_ref[:, a:b] = alpha * acc_ref[:, a:b] + jnp.dot(vt, p, preferred_element_type=F32)
        m_ref[:, a:b] = m_new

    def scores(jb, a, b, s_ref, mx_ref):
        r = pl.multiple_of(jb * tk, tk)
        s = jnp.dot(kn_ref[pl.ds(r, tk), :], qt_ref[:, a:b], preferred_element_type=F32)
        s_ref[a // cw] = s.astype(BF16)
        mx_ref[:, a:b] = jnp.max(s, axis=0, keepdims=True)

    def update(jb, a, b, s_ref, mx_ref):
        r = pl.multiple_of(jb * tk, tk)
        softmax_update(a, b, s_ref[a // cw], mx_ref[:, a:b], vt_ref[:, pl.ds(r, tk)])

    n_full = i * (tq // tk)
    for a, b in groups:
        scores(0, a, b, s0_ref, mx0_ref)

    def pair(jb):
        for a, b in groups:
            scores(jb + 1, a, b, s1_ref, mx1_ref)
            update(jb, a, b, s0_ref, mx0_ref)
        for a, b in groups:
            scores(jb + 2, a, b, s0_ref, mx0_ref)
            update(jb + 1, a, b, s1_ref, mx1_ref)

    def unrolled_pairs(t, carry):
        for u in range(unroll):
            pair(2 * unroll * t + 2 * u)
        return carry

    def one_pair(t, carry):
        pair(n_full - n_full % (2 * unroll) + 2 * t)
        return carry

    lax.fori_loop(0, n_full // (2 * unroll), unrolled_pairs, 0)
    lax.fori_loop(0, (n_full % (2 * unroll)) // 2, one_pair, 0)

    def key_rows(d, a):
        q0 = a % tq
        return d * tk, min((d + 1) * tk, q0 + cw), q0

    slots = (s0_ref, s1_ref)
    for d in range(tq // tk):
        r = pl.multiple_of((n_full + d) * tk, tk)
        r_next = pl.multiple_of((n_full + d + 1) * tk, tk)
        for a, b in groups:
            k0, k1, _ = key_rows(d + 1, a)
            if d + 1 < tq // tk and k1 > k0:
                slots[(d + 1) % 2][a // cw, 0:k1 - k0, :] = jnp.dot(
                    kn_ref[pl.ds(r_next, k1 - k0), :], qt_ref[:, a:b], preferred_element_type=F32).astype(BF16)
            k0, k1, q0 = key_rows(d, a)
            if k1 <= k0:
                continue
            s = slots[d % 2][a // cw, 0:k1 - k0, :].astype(F32)
            if k1 - 1 > q0:
                key = k0 + lax.broadcasted_iota(jnp.int32, s.shape, 0)
                qry = q0 + lax.broadcasted_iota(jnp.int32, s.shape, 1)
                s = jnp.where(key <= qry, s, NEG_BIG)
            softmax_update(a, b, s, jnp.max(s, axis=0, keepdims=True), vt_ref[:, pl.ds(r, k1 - k0)])

    o1 = acc_ref[0:HEAD_W, 0:tq] * (1.0 / acc_ref[HEAD_W:HEAD_W + 1, 0:tq])
    o2 = acc_ref[0:HEAD_W, tq:2 * tq] * (1.0 / acc_ref[HEAD_W:HEAD_W + 1, tq:2 * tq])
    lam = (jnp.exp(jnp.sum(lq1_ref[...] * lk1_ref[...], axis=-1, keepdims=True))
           - jnp.exp(jnp.sum(lq2_ref[...] * lk2_ref[...], axis=-1, keepdims=True)) + lambda_init)
    o = o1 - lam * o2
    inv = lax.rsqrt(jnp.mean(o * o, axis=0, keepdims=True) + EPS) * (1.0 - lambda_init)
    o_ref[...] = (o * inv * sg_ref[...]).T.astype(BF16)


def _cast_specs(w, first, count, n_steps, n_inner):
    r, c = w.shape[1], w.shape[2]
    per_layer = n_steps // count
    assert per_layer * count == n_steps and first + count <= w.shape[0]
    for nb_c in range(1, per_layer + 1):
        nb_r = per_layer // nb_c
        if per_layer % nb_c or c % nb_c or r % nb_r:
            continue
        br, bc = r // nb_r, c // nb_c
        if br % BF16_SUBLANES == 0 and bc % HEAD_W == 0:
            break
    else:
        raise ValueError(f"no slab split for {w.shape}")

    def index(offset):
        def index_map(h, i):
            t = h * n_inner + i
            rem = t % per_layer
            return (t // per_layer + offset, rem // nb_c, rem % nb_c)
        return index_map

    return (pl.BlockSpec((None, br, bc), index(first)), pl.BlockSpec((None, br, bc), index(0)),
            jax.ShapeDtypeStruct((count, r, c), BF16))


def _attn(z, qg, kg, lq1, lk1, lq2, lk2, sg, cast=(), *, lambda_init, tq=1024, tk=512, cw=256, unroll=4):
    s = z.shape[0]
    assert s % tq == 0 and tq % (2 * tk) == 0 and tq % cw == 0
    small = lambda w: pl.BlockSpec((1, w), lambda h, i: (0, 0))
    column = pl.BlockSpec((HEAD_W, 1), lambda h, i: (0, 0))
    n_inner = s // tq
    cast_specs = [_cast_specs(w, first, count, DA_HEADS * n_inner, n_inner) for w, first, count in cast]
    kern = functools.partial(_attn_kernel, n_cast=len(cast), tq=tq, tk=tk, cw=cw, unroll=unroll,
                             lambda_init=lambda_init)
    out = pl.pallas_call(
        kern,
        out_shape=[jax.ShapeDtypeStruct((s, DA_HEADS * HEAD_W), BF16)] + [c[2] for c in cast_specs],
        grid=(DA_HEADS, n_inner),
        in_specs=[
            pl.BlockSpec((tq, HEAD_W), lambda h, i: (i, QA_BLK + h)),
            pl.BlockSpec((s, HEAD_W), lambda h, i: (0, KA_BLK + h)),
            pl.BlockSpec((s, HEAD_W), lambda h, i: (0, VA_BLK + h)),
            column, small(HEAD_W),
            small(DA_HEAD_DIM), small(DA_HEAD_DIM), small(DA_HEAD_DIM), small(DA_HEAD_DIM),
            column,
        ] + [c[0] for c in cast_specs],
        out_specs=[pl.BlockSpec((tq, HEAD_W), lambda h, i: (i, h))] + [c[1] for c in cast_specs],
        scratch_shapes=[
            pltpu.VMEM((s, HEAD_W), BF16),
            pltpu.VMEM((ACC_ROWS, s), BF16),
            pltpu.VMEM((HEAD_W, 2 * tq), BF16),
            pltpu.VMEM((1, 2 * tq), F32),
            pltpu.VMEM((ACC_ROWS, 2 * tq), F32),
            pltpu.VMEM((2 * tq // cw, tk, cw), BF16),
            pltpu.VMEM((2 * tq // cw, tk, cw), BF16),
            pltpu.VMEM((1, 2 * tq), F32),
            pltpu.VMEM((1, 2 * tq), F32),
        ],
        compiler_params=_params(("arbitrary", "arbitrary"), 58),
        name="diff_attn",
    )(z, z, z, qg, kg, lq1, lk1, lq2, lk2, sg, *[c[0] for c in cast])
    return out[0], out[1:]


def _rope_kernel(pos_ref, invf_ref, cos_ref, sin_ref):
    ang = pos_ref[...].astype(F32) * invf_ref[...]
    lo = lax.broadcasted_iota(jnp.int32, (1, HEAD_W), 1) < HEAD_W // 2
    cos_ref[...] = jnp.cos(ang)
    sin_ref[...] = jnp.where(lo, -1.0, 1.0) * jnp.sin(ang)


def _rope_tables(pos_col, invf, *, tm=1024):
    s = pos_col.shape[0]
    tm = min(tm, s)
    return pl.pallas_call(
        _rope_kernel,
        out_shape=(jax.ShapeDtypeStruct((s, HEAD_W), F32), jax.ShapeDtypeStruct((s, HEAD_W), F32)),
        grid=(s // tm,),
        in_specs=[pl.BlockSpec((tm, 1), lambda i: (i, 0)), pl.BlockSpec((1, HEAD_W), lambda i: (0, 0))],
        out_specs=(pl.BlockSpec((tm, HEAD_W), lambda i: (i, 0)), pl.BlockSpec((tm, HEAD_W), lambda i: (i, 0))),
        compiler_params=_params(("arbitrary",), 32),
        name="rope_tables",
    )(pos_col, invf)


def _ret_kernel(q_ref, k_ref, v_ref, gr_ref, cos_ref, sin_ref, dec_ref, zeta_ref, xi_ref, cd_ref, sg_ref,
                o_ref, st_ref, *, heads):
    @pl.when(pl.program_id(1) == 0)
    def _():
        st_ref[...] = jnp.zeros(st_ref.shape, F32)

    cosf = cos_ref[...]
    sinf = sin_ref[...]
    for hh in range(heads):
        lanes = slice(hh * HEAD_W, (hh + 1) * HEAD_W)
        q = q_ref[:, lanes].astype(F32)
        k = k_ref[:, lanes].astype(F32)
        q = q * cosf + pltpu.roll(q, HEAD_W // 2, 1) * sinf
        k = (k * cosf + pltpu.roll(k, HEAD_W // 2, 1) * sinf) * RET_SCALE
        v = v_ref[:, lanes]
        st = st_ref[hh]

        sc = lax.dot_general(q.astype(BF16), k.astype(BF16), (((1,), (1,)), ((), ())),
                             preferred_element_type=F32) * dec_ref[hh]
        o = jnp.dot(sc.astype(BF16), v, preferred_element_type=F32)
        o = o + jnp.dot((q * xi_ref[hh]).astype(BF16), st.astype(BF16), preferred_element_type=F32)
        kz = (k * zeta_ref[hh]).astype(BF16)
        kv = lax.dot_general(kz, v, (((0,), (0,)), ((), ())), preferred_element_type=F32)
        st_ref[hh] = cd_ref[hh, 0:1, :] * st + kv

        g = gr_ref[:, lanes].astype(F32)
        o_ref[:, lanes] = (_rms(o, sg_ref[...]) * (g * jax.nn.sigmoid(g))).astype(BF16)


def _retention(z, cosf, sinf, dec, zeta, xi, cd, sg, *, chunk, heads=8):
    s = z.shape[0]
    w = heads * HEAD_W
    row = lambda blk: pl.BlockSpec((chunk, w), lambda h, i: (i, blk // heads + h))
    tab = pl.BlockSpec((chunk, HEAD_W), lambda h, i: (i, 0))
    per_head = lambda r, c: pl.BlockSpec((heads, r, c), lambda h, i: (h, 0, 0))
    return pl.pallas_call(
        functools.partial(_ret_kernel, heads=heads),
        out_shape=jax.ShapeDtypeStruct((s, RET_HEADS * HEAD_W), BF16),
        grid=(RET_HEADS // heads, s // chunk),
        in_specs=[
            row(QR_BLK), row(KR_BLK), row(VR_BLK), row(GR_BLK), tab, tab,
            per_head(chunk, chunk), per_head(chunk, HEAD_W), per_head(chunk, HEAD_W), per_head(8, HEAD_W),
            pl.BlockSpec((1, HEAD_W), lambda h, i: (0, 0)),
        ],
        out_specs=pl.BlockSpec((chunk, w), lambda h, i: (i, h)),
        scratch_shapes=[pltpu.VMEM((heads, HEAD_W, HEAD_W), F32)],
        compiler_params=_params(("arbitrary", "arbitrary"), 40),
        name="retention",
    )(z, z, z, z, cosf, sinf, dec, zeta, xi, cd, sg)


def _retention_tables(chunk):
    log_g = jnp.log(1.0 - 2.0 ** (-5.0 - jnp.arange(RET_HEADS, dtype=F32)))
    idx = jnp.arange(chunk, dtype=F32)
    rel = idx[:, None] - idx[None, :]
    dec = jnp.exp(log_g[:, None, None] * jnp.maximum(rel, 0.0)) * (rel >= 0)
    zeta = jnp.exp(log_g[:, None] * (chunk - 1 - idx))
    xi = jnp.exp(log_g[:, None] * (idx + 1.0))
    cd = jnp.exp(log_g * chunk)
    bcast = lambda t: jnp.broadcast_to(t[:, :, None], (RET_HEADS, chunk, HEAD_W))
    return dec, bcast(zeta), bcast(xi), jnp.broadcast_to(cd[:, None, None], (RET_HEADS, 8, HEAD_W))


def _merge_kernel(x_ref, ya_ref, yb_ref, ga0_ref, ga1_ref, gb0_ref, gb1_ref, wa_ref, wb_ref, wo_ref, o_ref):
    ta = jnp.dot(ya_ref[...], wa_ref[...], preferred_element_type=F32)
    tb = jnp.dot(yb_ref[...], wb_ref[...], preferred_element_type=F32)
    ga = jnp.concatenate([ga0_ref[...], ga1_ref[...]], axis=-1).astype(F32)
    gb = jnp.concatenate([gb0_ref[...], gb1_ref[...]], axis=-1).astype(F32)
    merged = jax.nn.sigmoid(ga) * ta + jax.nn.sigmoid(gb) * tb
    o_ref[...] = x_ref[...] + jnp.dot(merged.astype(BF16), wo_ref[...], preferred_element_type=F32)


def _merge(x, ya, yb, z, wa, wb, wo, layer, *, tm=256):
    s, d = x.shape
    gate = lambda blk: pl.BlockSpec((tm, GATE_BLK_W), lambda i: (i, blk))
    resident = lambda w: pl.BlockSpec((None,) + w.shape[1:], lambda i: (layer, 0, 0), pipeline_mode=pl.Buffered(1))
    return pl.pallas_call(
        _merge_kernel,
        out_shape=jax.ShapeDtypeStruct((s, d), F32),
        grid=(s // tm,),
        in_specs=[
            pl.BlockSpec((tm, d), lambda i: (i, 0)),
            pl.BlockSpec((tm, ya.shape[1]), lambda i: (i, 0)),
            pl.BlockSpec((tm, yb.shape[1]), lambda i: (i, 0)),
            gate(GA_BLK), gate(GA_BLK + 1), gate(GB_BLK), gate(GB_BLK + 1),
            resident(wa), resident(wb), resident(wo),
        ],
        out_specs=pl.BlockSpec((tm, d), lambda i: (i, 0)),
        compiler_params=_params(("arbitrary",), 48),
        name="merge_out",
    )(x, ya, yb, z, z, z, z, wa, wb, wo)


def _ple_kernel(x_ref, g_ref, p_ref, wg_ref, wp_ref, o_ref):
    x = x_ref[...]
    h = _rms(x, g_ref[...]).astype(BF16)
    gate = jax.nn.sigmoid(jnp.dot(h, wg_ref[...], preferred_element_type=F32))
    proj = jnp.dot(p_ref[...].astype(BF16), wp_ref[...], preferred_element_type=F32)
    o_ref[...] = x + gate * proj


def _ple(x, g, p, wg, wp, layer, *, tm=512):
    s, d = x.shape
    resident = lambda w: pl.BlockSpec((None,) + w.shape[1:], lambda i: (layer, 0, 0), pipeline_mode=pl.Buffered(1))
    return pl.pallas_call(
        _ple_kernel,
        out_shape=jax.ShapeDtypeStruct((s, d), F32),
        grid=(s // tm,),
        in_specs=[
            pl.BlockSpec((tm, d), lambda i: (i, 0)),
            pl.BlockSpec((1, d), lambda i: (0, 0)),
            pl.BlockSpec((None, tm, p.shape[2]), lambda i: (layer, i, 0)),
            resident(wg), resident(wp),
        ],
        out_specs=pl.BlockSpec((tm, d), lambda i: (i, 0)),
        compiler_params=_params(("arbitrary",), 48),
        name="ple",
    )(x, g, p, wg, wp)


def kernel(x, p, positions, ffn1_norm, ffn1_w_gate, ffn1_w_up, ffn1_w_down, mix_norm, w_in, da_q_norm, da_k_norm, da_lambda_q1, da_lambda_k1, da_lambda_q2, da_lambda_k2, da_sub_norm, ret_sub_norm, w_up_a, w_up_b, w_out, ffn2_norm, ffn2_w_gate, ffn2_w_up, ffn2_w_down, ple_norm, w_ple_gate, w_ple_proj):
    b, s, d = x.shape
    assert b == 1
    depth = w_in.shape[0]
    chunk = min(256, s)

    row = lambda t: t.reshape(1, -1)
    twice = lambda t: jnp.concatenate([t, t]).reshape(1, -1)
    bf = lambda t: t.astype(BF16)

    inv_freq = ROPE_BASE ** (-jnp.arange(0, HEAD_W, 2, dtype=F32) / HEAD_W)
    cosf, sinf = _rope_tables(positions.reshape(s, 1), twice(inv_freq))
    dec, zeta, xi, cd = _retention_tables(chunk)

    early = [bf(w[0:1]) for w in (ffn1_w_gate, ffn1_w_up, ffn1_w_down)]
    wpp = bf(w_ple_proj)
    late = [(w, 1, depth - 1) for w in (ffn1_w_gate, ffn1_w_up, ffn1_w_down, w_in)] if depth > 1 else []
    rest = [(w, 0, depth) for w in (ffn2_w_gate, ffn2_w_up, ffn2_w_down, w_up_a, w_up_b, w_out, w_ple_gate)]
    p3 = p.reshape(depth, s, -1)

    xc = x.reshape(s, d)
    for i in range(depth):
        lambda_init = 0.8 - 0.6 * math.exp(-0.3 * i)
        first = 0 if i == 0 else i - 1
        if i == 0:
            xc, (win,) = _ffn(xc, row(ffn1_norm[i]), *early, first, [(w_in, 0, 1)])
        else:
            f1g, f1u, f1d, win = cast_late
            xc, _ = _ffn(xc, row(ffn1_norm[i]), f1g, f1u, f1d, first)
        z = _inproj(xc, row(mix_norm[i]), win, first)
        ya, cast_out = _attn(z, twice(da_q_norm[i]).reshape(-1, 1), twice(da_k_norm[i]), row(da_lambda_q1[i]),
                             row(da_lambda_k1[i]), row(da_lambda_q2[i]), row(da_lambda_k2[i]),
                             da_sub_norm[i].reshape(-1, 1), (late + rest) if i == 0 else (),
                             lambda_init=lambda_init)
        if i == 0:
            cast_late = cast_out[:len(late)]
            f2g, f2u, f2d, wua, wub, wo, wpg = cast_out[len(late):]
        yb = _retention(z, cosf, sinf, dec, zeta, xi, cd, row(ret_sub_norm[i]), chunk=chunk)
        xc = _merge(xc, ya, yb, z, wua, wub, wo, i)
        xc, _ = _ffn(xc, row(ffn2_norm[i]), f2g, f2u, f2d, i)
        xc = _ple(xc, row(ple_norm[i]), p3, wpg, wpp, i)
    return xc.reshape(b, s, d)
```

```python
import functools
import math

import jax
import jax.numpy as jnp
from jax import lax
from jax.experimental import pallas as pl
from jax.experimental.pallas import tpu as pltpu

F32 = jnp.float32
BF16 = jnp.bfloat16

EPS = 1e-6
ROPE_BASE = 10000.0
DA_HEADS = 8
DA_HEAD_DIM = 64
RET_HEADS = 8
HEAD_W = 128
RET_SCALE = HEAD_W ** -0.5
DA_SCALE = DA_HEAD_DIM ** -0.5
NEG_BIG = -1e30
LOG2E = math.log2(math.e)
BF16_SUBLANES = 16
NORM_CHUNKS = 4
ACC_ROWS = HEAD_W + BF16_SUBLANES

QA_BLK, KA_BLK, VA_BLK, QR_BLK, KR_BLK, VR_BLK, GR_BLK = 0, 8, 16, 24, 32, 40, 48
GATE_BLK_W = 1024
GA_BLK, GB_BLK = 7, 9

MIB = 1024 * 1024


def _params(sem, vmem_mib, flags=None):
    return pltpu.CompilerParams(dimension_semantics=sem, vmem_limit_bytes=vmem_mib * MIB, flags=flags)


def _rms(x, g):
    return x * lax.rsqrt(jnp.mean(x * x, axis=-1, keepdims=True) + EPS) * g


def _ffn_kernel(x_ref, g_ref, wg_ref, wu_ref, wd_ref, *rest, n_cast):
    w_refs, o_ref, c_refs, h_ref = rest[:n_cast], rest[n_cast], rest[n_cast + 1:-1], rest[-1]
    for w_ref, c_ref in zip(w_refs, c_refs):
        c_ref[...] = w_ref[...].astype(BF16)
    j = pl.program_id(1)

    def swiglu_down(h):
        gate = jnp.dot(h, wg_ref[...], preferred_element_type=F32)
        up = jnp.dot(h, wu_ref[...], preferred_element_type=F32)
        act = (gate * jax.nn.sigmoid(gate) * up * 0.5).astype(BF16)
        return jnp.dot(act, wd_ref[...], preferred_element_type=F32)

    @pl.when(j == 0)
    def _():
        rows = x_ref.shape[0] // NORM_CHUNKS
        for c in range(NORM_CHUNKS):
            sl = slice(c * rows, (c + 1) * rows)
            x = x_ref[sl, :]
            h = _rms(x, g_ref[...]).astype(BF16)
            h_ref[sl, :] = h
            o_ref[sl, :] = x + swiglu_down(h)

    @pl.when(j > 0)
    def _():
        o_ref[...] += swiglu_down(h_ref[...])


def _ffn(x, g, wg, wu, wd, layer, cast=(), *, tm=1024, tf=512):
    s, d = x.shape
    tm = min(tm, s)
    f = wg.shape[2]
    grid = (s // tm, f // tf)
    cast_specs = [_cast_specs(w, first, count, grid[0] * grid[1], grid[1]) for w, first, count in cast]
    out = pl.pallas_call(
        functools.partial(_ffn_kernel, n_cast=len(cast)),
        out_shape=[jax.ShapeDtypeStruct((s, d), F32)] + [c[2] for c in cast_specs],
        grid=grid,
        in_specs=[
            pl.BlockSpec((tm, d), lambda i, j: (i, 0)),
            pl.BlockSpec((1, d), lambda i, j: (0, 0)),
            pl.BlockSpec((None, d, tf), lambda i, j: (layer, 0, j)),
            pl.BlockSpec((None, d, tf), lambda i, j: (layer, 0, j)),
            pl.BlockSpec((None, tf, d), lambda i, j: (layer, j, 0)),
        ] + [c[0] for c in cast_specs],
        out_specs=[pl.BlockSpec((tm, d), lambda i, j: (i, 0))] + [c[1] for c in cast_specs],
        scratch_shapes=[pltpu.VMEM((tm, d), BF16)],
        compiler_params=_params(("arbitrary", "arbitrary"), 58),
        name="ffn",
    )(x, g, wg, wu, wd, *[c[0] for c in cast])
    return out[0], out[1:]


def _inproj_kernel(x_ref, g_ref, w_ref, z_ref, h_ref):
    j = pl.program_id(1)

    @pl.when(j == 0)
    def _():
        rows = x_ref.shape[0] // NORM_CHUNKS
        for c in range(NORM_CHUNKS):
            sl = slice(c * rows, (c + 1) * rows)
            h = _rms(x_ref[sl, :], g_ref[...]).astype(BF16)
            h_ref[sl, :] = h
            z_ref[sl, :] = jnp.dot(h, w_ref[...], preferred_element_type=F32).astype(BF16)

    @pl.when(j > 0)
    def _():
        z_ref[...] = jnp.dot(h_ref[...], w_ref[...], preferred_element_type=F32).astype(BF16)


def _inproj(x, g, w, layer, *, tm=1024, tn=1024):
    s, d = x.shape
    tm = min(tm, s)
    n = w.shape[2]
    return pl.pallas_call(
        _inproj_kernel,
        out_shape=jax.ShapeDtypeStruct((s, n), BF16),
        grid=(s // tm, n // tn),
        in_specs=[
            pl.BlockSpec((tm, d), lambda i, j: (i, 0)),
            pl.BlockSpec((1, d), lambda i, j: (0, 0)),
            pl.BlockSpec((None, d, tn), lambda i, j: (layer, 0, j)),
        ],
        out_specs=pl.BlockSpec((tm, tn), lambda i, j: (i, j)),
        scratch_shapes=[pltpu.VMEM((tm, d), BF16)],
        compiler_params=_params(("arbitrary", "arbitrary"), 48),
        name="in_proj",
    )(x, g, w)


def _half_norm(x, g, same_half):
    xx = x * x
    hi = xx.astype(BF16)
    lo = (xx - hi.astype(F32)).astype(BF16)
    ss = (jnp.dot(hi, same_half, preferred_element_type=F32)
          + jnp.dot(lo, same_half, preferred_element_type=F32))
    return x * lax.rsqrt(ss * (1.0 / DA_HEAD_DIM) + EPS) * g


def _attn_kernel(*refs, n_cast, tq, tk, cw, unroll, lambda_init):
    q_ref, k_ref, v_ref, qg_ref, kg_ref, lq1_ref, lk1_ref, lq2_ref, lk2_ref, sg_ref = refs[:10]
    w_refs = refs[10:10 + n_cast]
    o_ref = refs[10 + n_cast]
    c_refs = refs[11 + n_cast:11 + 2 * n_cast]
    kn_ref, vt_ref, qt_ref, m_ref, acc_ref, s0_ref, s1_ref, mx0_ref, mx1_ref = refs[11 + 2 * n_cast:]

    for w_ref, c_ref in zip(w_refs, c_refs):
        c_ref[...] = w_ref[...].astype(BF16)

    i = pl.program_id(1)
    s_len = k_ref.shape[0]
    prep_rows = min(512, s_len)

    @pl.when(i == 0)
    def _():
        vt_ref[HEAD_W:ACC_ROWS, :] = jnp.ones((ACC_ROWS - HEAD_W, s_len), BF16)
        half = lambda axis: lax.broadcasted_iota(jnp.int32, (HEAD_W, HEAD_W), axis) < DA_HEAD_DIM
        same_half = (half(0) == half(1)).astype(BF16)

        def body(c, carry):
            r = pl.multiple_of(c * prep_rows, prep_rows)
            kk = k_ref[pl.ds(r, prep_rows), :].astype(F32)
            kn_ref[pl.ds(r, prep_rows), :] = _half_norm(kk, kg_ref[...], same_half).astype(BF16)
            vv = v_ref[pl.ds(r, prep_rows), :].astype(F32)
            vt_ref[0:HEAD_W, pl.ds(r, prep_rows)] = vv.T.astype(BF16)
            return carry
        lax.fori_loop(0, s_len // prep_rows, body, 0)

    dh = DA_HEAD_DIM
    qt = q_ref[...].astype(F32).T
    qq = qt * qt
    qscale = DA_SCALE * LOG2E
    inv1 = lax.rsqrt(jnp.sum(qq[0:dh], axis=0, keepdims=True) * (1.0 / dh) + EPS) * qscale
    inv2 = lax.rsqrt(jnp.sum(qq[dh:HEAD_W], axis=0, keepdims=True) * (1.0 / dh) + EPS) * qscale
    zeros = jnp.zeros((dh, tq), BF16)
    qt_ref[0:dh, 0:tq] = (qt[0:dh] * inv1 * qg_ref[0:dh, :]).astype(BF16)
    qt_ref[dh:HEAD_W, 0:tq] = zeros
    qt_ref[0:dh, tq:2 * tq] = zeros
    qt_ref[dh:HEAD_W, tq:2 * tq] = (qt[dh:HEAD_W] * inv2 * qg_ref[dh:HEAD_W, :]).astype(BF16)
    m_ref[...] = jnp.full(m_ref.shape, NEG_BIG, F32)
    acc_ref[...] = jnp.zeros(acc_ref.shape, F32)

    groups = [(c * cw, (c + 1) * cw) for c in range(2 * tq // cw)]

    def softmax_update(a, b, s, s_max, vt):
        m_prev = m_ref[:, a:b]
        m_new = jnp.maximum(m_prev, s_max)
        alpha = jnp.exp2(m_prev - m_new)
        if s.dtype == BF16:
            p = jnp.exp2(s - m_new.astype(BF16))
        else:
            p = jnp.exp2((s - m_new).astype(BF16))
        acc_ref[:, a:b] = alpha * acc_ref[:, a:b] + jnp.dot(vt, p, preferred_element_type=F32)
        m_ref[:, a:b] = m_new

    def scores(jb, a, b, s_ref, mx_ref):
        r = pl.multiple_of(jb * tk, tk)
        s = jnp.dot(kn_ref[pl.ds(r, tk), :], qt_ref[:, a:b], preferred_element_type=F32)
        s_ref[a // cw] = s.astype(BF16)
        mx_ref[:, a:b] = jnp.max(s, axis=0, keepdims=True)

    def update(jb, a, b, s_ref, mx_ref):
        r = pl.multiple_of(jb * tk, tk)
        softmax_update(a, b, s_ref[a // cw], mx_ref[:, a:b], vt_ref[:, pl.ds(r, tk)])

    n_full = i * (tq // tk)
    for a, b in groups:
        scores(0, a, b, s0_ref, mx0_ref)

    def pair(jb):
        for a, b in groups:
            scores(jb + 1, a, b, s1_ref, mx1_ref)
            update(jb, a, b, s0_ref, mx0_ref)
        for a, b in groups:
            scores(jb + 2, a, b, s0_ref, mx0_ref)
            update(jb + 1, a, b, s1_ref, mx1_ref)

    def unrolled_pairs(t, carry):
        for u in range(unroll):
            pair(2 * unroll * t + 2 * u)
        return carry

    def one_pair(t, carry):
        pair(n_full - n_full % (2 * unroll) + 2 * t)
        return carry

    lax.fori_loop(0, n_full // (2 * unroll), unrolled_pairs, 0)
    lax.fori_loop(0, (n_full % (2 * unroll)) // 2, one_pair, 0)

    def key_rows(d, a):
        q0 = a % tq
        return d * tk, min((d + 1) * tk, q0 + cw), q0

    slots = (s0_ref, s1_ref)
    for d in range(tq // tk):
        r = pl.multiple_of((n_full + d) * tk, tk)
        r_next = pl.multiple_of((n_full + d + 1) * tk, tk)
        for a, b in groups:
            k0, k1, _ = key_rows(d + 1, a)
            if d + 1 < tq // tk and k1 > k0:
                slots[(d + 1) % 2][a // cw, 0:k1 - k0, :] = jnp.dot(
                    kn_ref[pl.ds(r_next, k1 - k0), :], qt_ref[:, a:b], preferred_element_type=F32).astype(BF16)
            k0, k1, q0 = key_rows(d, a)
            if k1 <= k0:
                continue
            s = slots[d % 2][a // cw, 0:k1 - k0, :].astype(F32)
            if k1 - 1 > q0:
                key = k0 + lax.broadcasted_iota(jnp.int32, s.shape, 0)
                qry = q0 + lax.broadcasted_iota(jnp.int32, s.shape, 1)
                s = jnp.where(key <= qry, s, NEG_BIG)
            softmax_update(a, b, s, jnp.max(s, axis=0, keepdims=True), vt_ref[:, pl.ds(r, k1 - k0)])

    o1 = acc_ref[0:HEAD_W, 0:tq] * (1.0 / acc_ref[HEAD_W:HEAD_W + 1, 0:tq])
    o2 = acc_ref[0:HEAD_W, tq:2 * tq] * (1.0 / acc_ref[HEAD_W:HEAD_W + 1, tq:2 * tq])
    lam = (jnp.exp(jnp.sum(lq1_ref[...] * lk1_ref[...], axis=-1, keepdims=True))
           - jnp.exp(jnp.sum(lq2_ref[...] * lk2_ref[...], axis=-1, keepdims=True)) + lambda_init)
    o = o1 - lam * o2
    inv = lax.rsqrt(jnp.mean(o * o, axis=0, keepdims=True) + EPS) * (1.0 - lambda_init)
    o_ref[...] = (o * inv * sg_ref[...]).T.astype(BF16)


def _cast_specs(w, first, count, n_steps, n_inner):
    r, c = w.shape[1], w.shape[2]
    per_layer = n_steps // count
    assert per_layer * count == n_steps and first + count <= w.shape[0]
    for nb_c in range(1, per_layer + 1):
        nb_r = per_layer // nb_c
        if per_layer % nb_c or c % nb_c or r % nb_r:
            continue
        br, bc = r // nb_r, c // nb_c
        if br % BF16_SUBLANES == 0 and bc % HEAD_W == 0:
            break
    else:
        raise ValueError(f"no slab split for {w.shape}")

    def index(offset):
        def index_map(h, i):
            t = h * n_inner + i
            rem = t % per_layer
            return (t // per_layer + offset, rem // nb_c, rem % nb_c)
        return index_map

    return (pl.BlockSpec((None, br, bc), index(first)), pl.BlockSpec((None, br, bc), index(0)),
            jax.ShapeDtypeStruct((count, r, c), BF16))


def _attn(z, qg, kg, lq1, lk1, lq2, lk2, sg, cast=(), *, lambda_init, tq=1024, tk=512, cw=256, unroll=4):
    s = z.shape[0]
    assert s % tq == 0 and tq % (2 * tk) == 0 and tq % cw == 0
    small = lambda w: pl.BlockSpec((1, w), lambda h, i: (0, 0))
    column = pl.BlockSpec((HEAD_W, 1), lambda h, i: (0, 0))
    n_inner = s // tq
    cast_specs = [_cast_specs(w, first, count, DA_HEADS * n_inner, n_inner) for w, first, count in cast]
    kern = functools.partial(_attn_kernel, n_cast=len(cast), tq=tq, tk=tk, cw=cw, unroll=unroll,
                             lambda_init=lambda_init)
    out = pl.pallas_call(
        kern,
        out_shape=[jax.ShapeDtypeStruct((s, DA_HEADS * HEAD_W), BF16)] + [c[2] for c in cast_specs],
        grid=(DA_HEADS, n_inner),
        in_specs=[
            pl.BlockSpec((tq, HEAD_W), lambda h, i: (i, QA_BLK + h)),
            pl.BlockSpec((s, HEAD_W), lambda h, i: (0, KA_BLK + h)),
            pl.BlockSpec((s, HEAD_W), lambda h, i: (0, VA_BLK + h)),
            column, small(HEAD_W),
            small(DA_HEAD_DIM), small(DA_HEAD_DIM), small(DA_HEAD_DIM), small(DA_HEAD_DIM),
            column,
        ] + [c[0] for c in cast_specs],
        out_specs=[pl.BlockSpec((tq, HEAD_W), lambda h, i: (i, h))] + [c[1] for c in cast_specs],
        scratch_shapes=[
            pltpu.VMEM((s, HEAD_W), BF16),
            pltpu.VMEM((ACC_ROWS, s), BF16),
            pltpu.VMEM((HEAD_W, 2 * tq), BF16),
            pltpu.VMEM((1, 2 * tq), F32),
            pltpu.VMEM((ACC_ROWS, 2 * tq), F32),
            pltpu.VMEM((2 * tq // cw, tk, cw), BF16),
            pltpu.VMEM((2 * tq // cw, tk, cw), BF16),
            pltpu.VMEM((1, 2 * tq), F32),
            pltpu.VMEM((1, 2 * tq), F32),
        ],
        compiler_params=_params(("arbitrary", "arbitrary"), 58),
        name="diff_attn",
    )(z, z, z, qg, kg, lq1, lk1, lq2, lk2, sg, *[c[0] for c in cast])
    return out[0], out[1:]


def _rope_kernel(pos_ref, invf_ref, cos_ref, sin_ref):
    ang = pos_ref[...].astype(F32) * invf_ref[...]
    lo = lax.broadcasted_iota(jnp.int32, (1, HEAD_W), 1) < HEAD_W // 2
    cos_ref[...] = jnp.cos(ang)
    sin_ref[...] = jnp.where(lo, -1.0, 1.0) * jnp.sin(ang)


def _rope_tables(pos_col, invf, *, tm=1024):
    s = pos_col.shape[0]
    tm = min(tm, s)
    return pl.pallas_call(
        _rope_kernel,
        out_shape=(jax.ShapeDtypeStruct((s, HEAD_W), F32), jax.ShapeDtypeStruct((s, HEAD_W), F32)),
        grid=(s // tm,),
        in_specs=[pl.BlockSpec((tm, 1), lambda i: (i, 0)), pl.BlockSpec((1, HEAD_W), lambda i: (0, 0))],
        out_specs=(pl.BlockSpec((tm, HEAD_W), lambda i: (i, 0)), pl.BlockSpec((tm, HEAD_W), lambda i: (i, 0))),
        compiler_params=_params(("arbitrary",), 32),
        name="rope_tables",
    )(pos_col, invf)


def _ret_kernel(q_ref, k_ref, v_ref, gr_ref, cos_ref, sin_ref, dec_ref, zeta_ref, xi_ref, cd_ref, sg_ref,
                o_ref, st_ref, *, heads):
    @pl.when(pl.program_id(1) == 0)
    def _():
        st_ref[...] = jnp.zeros(st_ref.shape, F32)

    cosf = cos_ref[...]
    sinf = sin_ref[...]
    for hh in range(heads):
        lanes = slice(hh * HEAD_W, (hh + 1) * HEAD_W)
        q = q_ref[:, lanes].astype(F32)
        k = k_ref[:, lanes].astype(F32)
        q = q * cosf + pltpu.roll(q, HEAD_W // 2, 1) * sinf
        k = (k * cosf + pltpu.roll(k, HEAD_W // 2, 1) * sinf) * RET_SCALE
        v = v_ref[:, lanes]
        st = st_ref[hh]

        sc = lax.dot_general(q.astype(BF16), k.astype(BF16), (((1,), (1,)), ((), ())),
                             preferred_element_type=F32) * dec_ref[hh]
        o = jnp.dot(sc.astype(BF16), v, preferred_element_type=F32)
        o = o + jnp.dot((q * xi_ref[hh]).astype(BF16), st.astype(BF16), preferred_element_type=F32)
        kz = (k * zeta_ref[hh]).astype(BF16)
        kv = lax.dot_general(kz, v, (((0,), (0,)), ((), ())), preferred_element_type=F32)
        st_ref[hh] = cd_ref[hh, 0:1, :] * st + kv

        g = gr_ref[:, lanes].astype(F32)
        o_ref[:, lanes] = (_rms(o, sg_ref[...]) * (g * jax.nn.sigmoid(g))).astype(BF16)


def _retention(z, cosf, sinf, dec, zeta, xi, cd, sg, *, chunk, heads=8):
    s = z.shape[0]
    w = heads * HEAD_W
    row = lambda blk: pl.BlockSpec((chunk, w), lambda h, i: (i, blk // heads + h))
    tab = pl.BlockSpec((chunk, HEAD_W), lambda h, i: (i, 0))
    per_head = lambda r, c: pl.BlockSpec((heads, r, c), lambda h, i: (h, 0, 0))
    return pl.pallas_call(
        functools.partial(_ret_kernel, heads=heads),
        out_shape=jax.ShapeDtypeStruct((s, RET_HEADS * HEAD_W), BF16),
        grid=(RET_HEADS // heads, s // chunk),
        in_specs=[
            row(QR_BLK), row(KR_BLK), row(VR_BLK), row(GR_BLK), tab, tab,
            per_head(chunk, chunk), per_head(chunk, HEAD_W), per_head(chunk, HEAD_W), per_head(8, HEAD_W),
            pl.BlockSpec((1, HEAD_W), lambda h, i: (0, 0)),
        ],
        out_specs=pl.BlockSpec((chunk, w), lambda h, i: (i, h)),
        scratch_shapes=[pltpu.VMEM((heads, HEAD_W, HEAD_W), F32)],
        compiler_params=_params(("arbitrary", "arbitrary"), 40),
        name="retention",
    )(z, z, z, z, cosf, sinf, dec, zeta, xi, cd, sg)


def _retention_tables(chunk):
    log_g = jnp.log(1.0 - 2.0 ** (-5.0 - jnp.arange(RET_HEADS, dtype=F32)))
    idx = jnp.arange(chunk, dtype=F32)
    rel = idx[:, None] - idx[None, :]
    dec = jnp.exp(log_g[:, None, None] * jnp.maximum(rel, 0.0)) * (rel >= 0)
    zeta = jnp.exp(log_g[:, None] * (chunk - 1 - idx))
    xi = jnp.exp(log_g[:, None] * (idx + 1.0))
    cd = jnp.exp(log_g * chunk)
    bcast = lambda t: jnp.broadcast_to(t[:, :, None], (RET_HEADS, chunk, HEAD_W))
    return dec, bcast(zeta), bcast(xi), jnp.broadcast_to(cd[:, None, None], (RET_HEADS, 8, HEAD_W))


def _merge_kernel(x_ref, ya_ref, yb_ref, ga0_ref, ga1_ref, gb0_ref, gb1_ref, wa_ref, wb_ref, wo_ref, o_ref):
    ta = jnp.dot(ya_ref[...], wa_ref[...], preferred_element_type=F32)
    tb = jnp.dot(yb_ref[...], wb_ref[...], preferred_element_type=F32)
    ga = jnp.concatenate([ga0_ref[...], ga1_ref[...]], axis=-1).astype(F32)
    gb = jnp.concatenate([gb0_ref[...], gb1_ref[...]], axis=-1).astype(F32)
    merged = jax.nn.sigmoid(ga) * ta + jax.nn.sigmoid(gb) * tb
    o_ref[...] = x_ref[...] + jnp.dot(merged.astype(BF16), wo_ref[...], preferred_element_type=F32)


def _merge(x, ya, yb, z, wa, wb, wo, layer, *, tm=256):
    s, d = x.shape
    gate = lambda blk: pl.BlockSpec((tm, GATE_BLK_W), lambda i: (i, blk))
    resident = lambda w: pl.BlockSpec((None,) + w.shape[1:], lambda i: (layer, 0, 0), pipeline_mode=pl.Buffered(1))
    return pl.pallas_call(
        _merge_kernel,
        out_shape=jax.ShapeDtypeStruct((s, d), F32),
        grid=(s // tm,),
        in_specs=[
            pl.BlockSpec((tm, d), lambda i: (i, 0)),
            pl.BlockSpec((tm, ya.shape[1]), lambda i: (i, 0)),
            pl.BlockSpec((tm, yb.shape[1]), lambda i: (i, 0)),
            gate(GA_BLK), gate(GA_BLK + 1), gate(GB_BLK), gate(GB_BLK + 1),
            resident(wa), resident(wb), resident(wo),
        ],
        out_specs=pl.BlockSpec((tm, d), lambda i: (i, 0)),
        compiler_params=_params(("arbitrary",), 48),
        name="merge_out",
    )(x, ya, yb, z, z, z, z, wa, wb, wo)


def _ple_kernel(x_ref, g_ref, p_ref, wg_ref, wp_ref, o_ref):
    x = x_ref[...]
    h = _rms(x, g_ref[...]).astype(BF16)
    gate = jax.nn.sigmoid(jnp.dot(h, wg_ref[...], preferred_element_type=F32))
    proj = jnp.dot(p_ref[...].astype(BF16), wp_ref[...], preferred_element_type=F32)
    o_ref[...] = x + gate * proj


def _ple(x, g, p, wg, wp, layer, *, tm=512):
    s, d = x.shape
    resident = lambda w: pl.BlockSpec((None,) + w.shape[1:], lambda i: (layer, 0, 0), pipeline_mode=pl.Buffered(1))
    return pl.pallas_call(
        _ple_kernel,
        out_shape=jax.ShapeDtypeStruct((s, d), F32),
        grid=(s // tm,),
        in_specs=[
            pl.BlockSpec((tm, d), lambda i: (i, 0)),
            pl.BlockSpec((1, d), lambda i: (0, 0)),
            pl.BlockSpec((None, tm, p.shape[2]), lambda i: (layer, i, 0)),
            resident(wg), resident(wp),
        ],
        out_specs=pl.BlockSpec((tm, d), lambda i: (i, 0)),
        compiler_params=_params(("arbitrary",), 48),
        name="ple",
    )(x, g, p, wg, wp)


def kernel(x, p, positions, ffn1_norm, ffn1_w_gate, ffn1_w_up, ffn1_w_down, mix_norm, w_in, da_q_norm, da_k_norm, da_lambda_q1, da_lambda_k1, da_lambda_q2, da_lambda_k2, da_sub_norm, ret_sub_norm, w_up_a, w_up_b, w_out, ffn2_norm, ffn2_w_gate, ffn2_w_up, ffn2_w_down, ple_norm, w_ple_gate, w_ple_proj):
    b, s, d = x.shape
    assert b == 1
    depth = w_in.shape[0]
    chunk = min(256, s)

    row = lambda t: t.reshape(1, -1)
    twice = lambda t: jnp.concatenate([t, t]).reshape(1, -1)
    bf = lambda t: t.astype(BF16)

    inv_freq = ROPE_BASE ** (-jnp.arange(0, HEAD_W, 2, dtype=F32) / HEAD_W)
    cosf, sinf = _rope_tables(positions.reshape(s, 1), twice(inv_freq))
    dec, zeta, xi, cd = _retention_tables(chunk)

    early = [bf(w[0:1]) for w in (ffn1_w_gate, ffn1_w_up, ffn1_w_down)]
    wpp = bf(w_ple_proj)
    late = [(w, 1, depth - 1) for w in (ffn1_w_gate, ffn1_w_up, ffn1_w_down, w_in)] if depth > 1 else []
    rest = [(w, 0, depth) for w in (ffn2_w_gate, ffn2_w_up, ffn2_w_down, w_up_a, w_up_b, w_out, w_ple_gate)]
    p3 = p.reshape(depth, s, -1)

    xc = x.reshape(s, d)
    for i in range(depth):
        lambda_init = 0.8 - 0.6 * math.exp(-0.3 * i)
        first = 0 if i == 0 else i - 1
        if i == 0:
            xc, (win,) = _ffn(xc, row(ffn1_norm[i]), *early, first, [(w_in, 0, 1)])
        else:
            f1g, f1u, f1d, win = cast_late
            xc, _ = _ffn(xc, row(ffn1_norm[i]), f1g, f1u, f1d, first)
        z = _inproj(xc, row(mix_norm[i]), win, first)
        ya, cast_out = _attn(z, twice(da_q_norm[i]).reshape(-1, 1), twice(da_k_norm[i]), row(da_lambda_q1[i]),
                             row(da_lambda_k1[i]), row(da_lambda_q2[i]), row(da_lambda_k2[i]),
                             da_sub_norm[i].reshape(-1, 1), (late + rest) if i == 0 else (),
                             lambda_init=lambda_init)
        if i == 0:
            cast_late = cast_out[:len(late)]
            f2g, f2u, f2d, wua, wub, wo, wpg = cast_out[len(late):]
        yb = _retention(z, cosf, sinf, dec, zeta, xi, cd, row(ret_sub_norm[i]), chunk=chunk)
        xc = _merge(xc, ya, yb, z, wua, wub, wo, i)
        xc, _ = _ffn(xc, row(ffn2_norm[i]), f2g, f2u, f2d, i)
        xc = _ple(xc, row(ple_norm[i]), p3, wpg, wpp, i)
    return xc.reshape(b, s, d)
```

```python
import functools
import math

import jax
import jax.numpy as jnp
from jax import lax
from jax.experimental import pallas as pl
from jax.experimental.pallas import tpu as pltpu

F32 = jnp.float32
BF16 = jnp.bfloat16

EPS = 1e-6
ROPE_BASE = 10000.0
DA_HEADS = 8
DA_HEAD_DIM = 64
RET_HEADS = 8
HEAD_W = 128
RET_SCALE = HEAD_W ** -0.5
DA_SCALE = DA_HEAD_DIM ** -0.5
NEG_BIG = -1e30
LOG2E = math.log2(math.e)
BF16_SUBLANES = 16
NORM_CHUNKS = 4
ACC_ROWS = HEAD_W + BF16_SUBLANES

QA_BLK, KA_BLK, VA_BLK, QR_BLK, KR_BLK, VR_BLK, GR_BLK = 0, 8, 16, 24, 32, 40, 48
GATE_BLK_W = 1024
GA_BLK, GB_BLK = 7, 9

MIB = 1024 * 1024


def _params(sem, vmem_mib, flags=None):
    return pltpu.CompilerParams(dimension_semantics=sem, vmem_limit_bytes=vmem_mib * MIB, flags=flags)


def _rms(x, g):
    return x * lax.rsqrt(jnp.mean(x * x, axis=-1, keepdims=True) + EPS) * g


def _ffn_kernel(x_ref, g_ref, wg_ref, wu_ref, wd_ref, *rest, n_cast):
    w_refs, o_ref, c_refs, h_ref = rest[:n_cast], rest[n_cast], rest[n_cast + 1:-1], rest[-1]
    for w_ref, c_ref in zip(w_refs, c_refs):
        c_ref[...] = w_ref[...].astype(BF16)
    j = pl.program_id(1)

    def swiglu_down(h):
        gate = jnp.dot(h, wg_ref[...], preferred_element_type=F32)
        up = jnp.dot(h, wu_ref[...], preferred_element_type=F32)
        act = (gate * jax.nn.sigmoid(gate) * up * 0.5).astype(BF16)
        return jnp.dot(act, wd_ref[...], preferred_element_type=F32)

    @pl.when(j == 0)
    def _():
        rows = x_ref.shape[0] // NORM_CHUNKS
        for c in range(NORM_CHUNKS):
            sl = slice(c * rows, (c + 1) * rows)
            x = x_ref[sl, :]
            h = _rms(x, g_ref[...]).astype(BF16)
            h_ref[sl, :] = h
            o_ref[sl, :] = x + swiglu_down(h)

    @pl.when(j > 0)
    def _():
        o_ref[...] += swiglu_down(h_ref[...])


def _ffn(x, g, wg, wu, wd, layer, cast=(), *, tm=1024, tf=512):
    s, d = x.shape
    tm = min(tm, s)
    f = wg.shape[2]
    grid = (s // tm, f // tf)
    cast_specs = [_cast_specs(w, first, count, grid[0] * grid[1], grid[1]) for w, first, count in cast]
    out = pl.pallas_call(
        functools.partial(_ffn_kernel, n_cast=len(cast)),
        out_shape=[jax.ShapeDtypeStruct((s, d), F32)] + [c[2] for c in cast_specs],
        grid=grid,
        in_specs=[
            pl.BlockSpec((tm, d), lambda i, j: (i, 0)),
            pl.BlockSpec((1, d), lambda i, j: (0, 0)),
            pl.BlockSpec((None, d, tf), lambda i, j: (layer, 0, j)),
            pl.BlockSpec((None, d, tf), lambda i, j: (layer, 0, j)),
            pl.BlockSpec((None, tf, d), lambda i, j: (layer, j, 0)),
        ] + [c[0] for c in cast_specs],
        out_specs=[pl.BlockSpec((tm, d), lambda i, j: (i, 0))] + [c[1] for c in cast_specs],
        scratch_shapes=[pltpu.VMEM((tm, d), BF16)],
        compiler_params=_params(("arbitrary", "arbitrary"), 58),
        name="ffn",
    )(x, g, wg, wu, wd, *[c[0] for c in cast])
    return out[0], out[1:]


def _inproj_kernel(x_ref, g_ref, w_ref, z_ref, h_ref):
    j = pl.program_id(1)

    @pl.when(j == 0)
    def _():
        rows = x_ref.shape[0] // NORM_CHUNKS
        for c in range(NORM_CHUNKS):
            sl = slice(c * rows, (c + 1) * rows)
            h = _rms(x_ref[sl, :], g_ref[...]).astype(BF16)
            h_ref[sl, :] = h
            z_ref[sl, :] = jnp.dot(h, w_ref[...], preferred_element_type=F32).astype(BF16)

    @pl.when(j > 0)
    def _():
        z_ref[...] = jnp.dot(h_ref[...], w_ref[...], preferred_element_type=F32).astype(BF16)


def _inproj(x, g, w, layer, *, tm=512, tn=2816):
    s, d = x.shape
    tm = min(tm, s)
    n = w.shape[2]
    return pl.pallas_call(
        _inproj_kernel,
        out_shape=jax.ShapeDtypeStruct((s, n), BF16),
        grid=(s // tm, n // tn),
        in_specs=[
            pl.BlockSpec((tm, d), lambda i, j: (i, 0)),
            pl.BlockSpec((1, d), lambda i, j: (0, 0)),
            pl.BlockSpec((None, d, tn), lambda i, j: (layer, 0, j)),
        ],
        out_specs=pl.BlockSpec((tm, tn), lambda i, j: (i, j)),
        scratch_shapes=[pltpu.VMEM((tm, d), BF16)],
        compiler_params=_params(("arbitrary", "arbitrary"), 48),
        name="in_proj",
    )(x, g, w)


def _half_norm(x, g, same_half):
    xx = x * x
    hi = xx.astype(BF16)
    lo = (xx - hi.astype(F32)).astype(BF16)
    ss = (jnp.dot(hi, same_half, preferred_element_type=F32)
          + jnp.dot(lo, same_half, preferred_element_type=F32))
    return x * lax.rsqrt(ss * (1.0 / DA_HEAD_DIM) + EPS) * g


def _attn_kernel(*refs, n_cast, tq, tk, cw, unroll, lambda_init):
    q_ref, k_ref, v_ref, qg_ref, kg_ref, lq1_ref, lk1_ref, lq2_ref, lk2_ref, sg_ref = refs[:10]
    w_refs = refs[10:10 + n_cast]
    o_ref = refs[10 + n_cast]
    c_refs = refs[11 + n_cast:11 + 2 * n_cast]
    kn_ref, vt_ref, qt_ref, m_ref, acc_ref, s0_ref, s1_ref, mx0_ref, mx1_ref = refs[11 + 2 * n_cast:]

    i = pl.program_id(1)
    s_len = k_ref.shape[0]
    prep_rows = min(512, s_len)

    @pl.when(i == 0)
    def _():
        vt_ref[HEAD_W:ACC_ROWS, :] = jnp.ones((ACC_ROWS - HEAD_W, s_len), BF16)
        half = lambda axis: lax.broadcasted_iota(jnp.int32, (HEAD_W, HEAD_W), axis) < DA_HEAD_DIM
        same_half = (half(0) == half(1)).astype(BF16)

        def body(c, carry):
            r = pl.multiple_of(c * prep_rows, prep_rows)
            kk = k_ref[pl.ds(r, prep_rows), :].astype(F32)
            kn_ref[pl.ds(r, prep_rows), :] = _half_norm(kk, kg_ref[...], same_half).astype(BF16)
            vv = v_ref[pl.ds(r, prep_rows), :].astype(F32)
            vt_ref[0:HEAD_W, pl.ds(r, prep_rows)] = vv.T.astype(BF16)
            return carry
        lax.fori_loop(0, s_len // prep_rows, body, 0)

    dh = DA_HEAD_DIM
    qt = q_ref[...].astype(F32).T
    qq = qt * qt
    qscale = DA_SCALE * LOG2E
    inv1 = lax.rsqrt(jnp.sum(qq[0:dh], axis=0, keepdims=True) * (1.0 / dh) + EPS) * qscale
    inv2 = lax.rsqrt(jnp.sum(qq[dh:HEAD_W], axis=0, keepdims=True) * (1.0 / dh) + EPS) * qscale
    zeros = jnp.zeros((dh, tq), BF16)
    qt_ref[0:dh, 0:tq] = (qt[0:dh] * inv1 * qg_ref[0:dh, :]).astype(BF16)
    qt_ref[dh:HEAD_W, 0:tq] = zeros
    qt_ref[0:dh, tq:2 * tq] = zeros
    qt_ref[dh:HEAD_W, tq:2 * tq] = (qt[dh:HEAD_W] * inv2 * qg_ref[dh:HEAD_W, :]).astype(BF16)
    m_ref[...] = jnp.full(m_ref.shape, NEG_BIG, F32)
    acc_ref[...] = jnp.zeros(acc_ref.shape, F32)

    groups = [(c * cw, (c + 1) * cw) for c in range(2 * tq // cw)]

    def softmax_update(a, b, s, s_max, vt):
        m_prev = m_ref[:, a:b]
        m_new = jnp.maximum(m_prev, s_max)
        alpha = jnp.exp2(m_prev - m_new)
        if s.dtype == BF16:
            p = jnp.exp2(s - m_new.astype(BF16))
        else:
            p = jnp.exp2((s - m_new).astype(BF16))
        acc_ref[:, a:b] = alpha * acc_ref[:, a:b] + jnp.dot(vt, p, preferred_element_type=F32)
        m_ref[:, a:b] = m_new

    def scores(jb, a, b, s_ref, mx_ref):
        r = pl.multiple_of(jb * tk, tk)
        s = jnp.dot(kn_ref[pl.ds(r, tk), :], qt_ref[:, a:b], preferred_element_type=F32)
        s_ref[a // cw] = s.astype(BF16)
        mx_ref[:, a:b] = jnp.max(s, axis=0, keepdims=True)

    def update(jb, a, b, s_ref, mx_ref):
        r = pl.multiple_of(jb * tk, tk)
        softmax_update(a, b, s_ref[a // cw], mx_ref[:, a:b], vt_ref[:, pl.ds(r, tk)])

    n_full = i * (tq // tk)
    for a, b in groups:
        scores(0, a, b, s0_ref, mx0_ref)

    def pair(jb):
        for a, b in groups:
            scores(jb + 1, a, b, s1_ref, mx1_ref)
            update(jb, a, b, s0_ref, mx0_ref)
        for a, b in groups:
            scores(jb + 2, a, b, s0_ref, mx0_ref)
            update(jb + 1, a, b, s1_ref, mx1_ref)

    def unrolled_pairs(t, carry):
        for u in range(unroll):
            pair(2 * unroll * t + 2 * u)
        return carry

    def one_pair(t, carry):
        pair(n_full - n_full % (2 * unroll) + 2 * t)
        return carry

    lax.fori_loop(0, n_full // (2 * unroll), unrolled_pairs, 0)
    lax.fori_loop(0, (n_full % (2 * unroll)) // 2, one_pair, 0)

    for w_ref, c_ref in zip(w_refs, c_refs):
        c_ref[...] = w_ref[...].astype(BF16)

    def key_rows(d, a):
        q0 = a % tq
        return d * tk, min((d + 1) * tk, q0 + cw), q0

    slots = (s0_ref, s1_ref)
    for d in range(tq // tk):
        r = pl.multiple_of((n_full + d) * tk, tk)
        r_next = pl.multiple_of((n_full + d + 1) * tk, tk)
        for a, b in groups:
            k0, k1, _ = key_rows(d + 1, a)
            if d + 1 < tq // tk and k1 > k0:
                slots[(d + 1) % 2][a // cw, 0:k1 - k0, :] = jnp.dot(
                    kn_ref[pl.ds(r_next, k1 - k0), :], qt_ref[:, a:b], preferred_element_type=F32).astype(BF16)
            k0, k1, q0 = key_rows(d, a)
            if k1 <= k0:
                continue
            s = slots[d % 2][a // cw, 0:k1 - k0, :].astype(F32)
            if k1 - 1 > q0:
                key = k0 + lax.broadcasted_iota(jnp.int32, s.shape, 0)
                qry = q0 + lax.broadcasted_iota(jnp.int32, s.shape, 1)
                s = jnp.where(key <= qry, s, NEG_BIG)
            softmax_update(a, b, s, jnp.max(s, axis=0, keepdims=True), vt_ref[:, pl.ds(r, k1 - k0)])

    o1 = acc_ref[0:HEAD_W, 0:tq] * (1.0 / acc_ref[HEAD_W:HEAD_W + 1, 0:tq])
    o2 = acc_ref[0:HEAD_W, tq:2 * tq] * (1.0 / acc_ref[HEAD_W:HEAD_W + 1, tq:2 * tq])
    lam = (jnp.exp(jnp.sum(lq1_ref[...] * lk1_ref[...], axis=-1, keepdims=True))
           - jnp.exp(jnp.sum(lq2_ref[...] * lk2_ref[...], axis=-1, keepdims=True)) + lambda_init)
    o = o1 - lam * o2
    inv = lax.rsqrt(jnp.mean(o * o, axis=0, keepdims=True) + EPS) * (1.0 - lambda_init)
    o_ref[...] = (o * inv * sg_ref[...]).T.astype(BF16)


def _cast_specs(w, first, count, n_steps, n_inner):
    r, c = w.shape[1], w.shape[2]
    per_layer = n_steps // count
    assert per_layer * count == n_steps and first + count <= w.shape[0]
    for nb_c in range(1, per_layer + 1):
        nb_r = per_layer // nb_c
        if per_layer % nb_c or c % nb_c or r % nb_r:
            continue
        br, bc = r // nb_r, c // nb_c
        if br % BF16_SUBLANES == 0 and bc % HEAD_W == 0:
            break
    else:
        raise ValueError(f"no slab split for {w.shape}")

    def index(offset):
        def index_map(h, i):
            t = h * n_inner + i
            rem = t % per_layer
            return (t // per_layer + offset, rem // nb_c, rem % nb_c)
        return index_map

    return (pl.BlockSpec((None, br, bc), index(first)), pl.BlockSpec((None, br, bc), index(0)),
            jax.ShapeDtypeStruct((count, r, c), BF16))


def _attn(z, qg, kg, lq1, lk1, lq2, lk2, sg, cast=(), *, lambda_init, tq=1024, tk=512, cw=256, unroll=4):
    s = z.shape[0]
    assert s % tq == 0 and tq % (2 * tk) == 0 and tq % cw == 0
    small = lambda w: pl.BlockSpec((1, w), lambda h, i: (0, 0))
    column = pl.BlockSpec((HEAD_W, 1), lambda h, i: (0, 0))
    n_inner = s // tq
    cast_specs = [_cast_specs(w, first, count, DA_HEADS * n_inner, n_inner) for w, first, count in cast]
    kern = functools.partial(_attn_kernel, n_cast=len(cast), tq=tq, tk=tk, cw=cw, unroll=unroll,
                             lambda_init=lambda_init)
    out = pl.pallas_call(
        kern,
        out_shape=[jax.ShapeDtypeStruct((s, DA_HEADS * HEAD_W), BF16)] + [c[2] for c in cast_specs],
        grid=(DA_HEADS, n_inner),
        in_specs=[
            pl.BlockSpec((tq, HEAD_W), lambda h, i: (i, QA_BLK + h)),
            pl.BlockSpec((s, HEAD_W), lambda h, i: (0, KA_BLK + h)),
            pl.BlockSpec((s, HEAD_W), lambda h, i: (0, VA_BLK + h)),
            column, small(HEAD_W),
            small(DA_HEAD_DIM), small(DA_HEAD_DIM), small(DA_HEAD_DIM), small(DA_HEAD_DIM),
            column,
        ] + [c[0] for c in cast_specs],
        out_specs=[pl.BlockSpec((tq, HEAD_W), lambda h, i: (i, h))] + [c[1] for c in cast_specs],
        scratch_shapes=[
            pltpu.VMEM((s, HEAD_W), BF16),
            pltpu.VMEM((ACC_ROWS, s), BF16),
            pltpu.VMEM((HEAD_W, 2 * tq), BF16),
            pltpu.VMEM((1, 2 * tq), F32),
            pltpu.VMEM((ACC_ROWS, 2 * tq), F32),
            pltpu.VMEM((2 * tq // cw, tk, cw), BF16),
            pltpu.VMEM((2 * tq // cw, tk, cw), BF16),
            pltpu.VMEM((1, 2 * tq), F32),
            pltpu.VMEM((1, 2 * tq), F32),
        ],
        compiler_params=_params(("arbitrary", "arbitrary"), 58),
        name="diff_attn",
    )(z, z, z, qg, kg, lq1, lk1, lq2, lk2, sg, *[c[0] for c in cast])
    return out[0], out[1:]


def _rope_kernel(pos_ref, invf_ref, *rest, n_cast):
    w_refs, (cos_ref, sin_ref), c_refs = rest[:n_cast], rest[n_cast:n_cast + 2], rest[n_cast + 2:]
    for w_ref, c_ref in zip(w_refs, c_refs):
        c_ref[...] = w_ref[...].astype(BF16)
    ang = pos_ref[...].astype(F32) * invf_ref[...]
    lo = lax.broadcasted_iota(jnp.int32, (1, HEAD_W), 1) < HEAD_W // 2
    cos_ref[...] = jnp.cos(ang)
    sin_ref[...] = jnp.where(lo, -1.0, 1.0) * jnp.sin(ang)


def _rope_tables(pos_col, invf, cast=(), *, tm=1024):
    s = pos_col.shape[0]
    tm = min(tm, s)
    n = s // tm
    cast_specs = [_cast_specs(w, first, count, n, 1) for w, first, count in cast]
    table = pl.BlockSpec((tm, HEAD_W), lambda i, _: (i, 0))
    out = pl.pallas_call(
        functools.partial(_rope_kernel, n_cast=len(cast)),
        out_shape=[jax.ShapeDtypeStruct((s, HEAD_W), F32)] * 2 + [c[2] for c in cast_specs],
        grid=(n, 1),
        in_specs=[pl.BlockSpec((tm, 1), lambda i, _: (i, 0)), pl.BlockSpec((1, HEAD_W), lambda i, _: (0, 0))]
        + [c[0] for c in cast_specs],
        out_specs=[table, table] + [c[1] for c in cast_specs],
        compiler_params=_params(("arbitrary", "arbitrary"), 40),
        name="rope_tables",
    )(pos_col, invf, *[c[0] for c in cast])
    return out[:2], out[2:]


def _ret_kernel(q_ref, k_ref, v_ref, gr_ref, cos_ref, sin_ref, dec_ref, zeta_ref, xi_ref, cd_ref, sg_ref,
                o_ref, st_ref, *, heads):
    @pl.when(pl.program_id(1) == 0)
    def _():
        st_ref[...] = jnp.zeros(st_ref.shape, F32)

    cosf = cos_ref[...]
    sinf = sin_ref[...]
    for hh in range(heads):
        lanes = slice(hh * HEAD_W, (hh + 1) * HEAD_W)
        q = q_ref[:, lanes].astype(F32)
        k = k_ref[:, lanes].astype(F32)
        q = q * cosf + pltpu.roll(q, HEAD_W // 2, 1) * sinf
        k = (k * cosf + pltpu.roll(k, HEAD_W // 2, 1) * sinf) * RET_SCALE
        v = v_ref[:, lanes]
        st = st_ref[hh]

        sc = lax.dot_general(q.astype(BF16), k.astype(BF16), (((1,), (1,)), ((), ())),
                             preferred_element_type=F32) * dec_ref[hh]
        o = jnp.dot(sc.astype(BF16), v, preferred_element_type=F32)
        o = o + jnp.dot((q * xi_ref[hh]).astype(BF16), st.astype(BF16), preferred_element_type=F32)
        kz = (k * zeta_ref[hh]).astype(BF16)
        kv = lax.dot_general(kz, v, (((0,), (0,)), ((), ())), preferred_element_type=F32)
        st_ref[hh] = cd_ref[hh, 0:1, :] * st + kv

        g = gr_ref[:, lanes].astype(F32)
        o_ref[:, lanes] = (_rms(o, sg_ref[...]) * (g * jax.nn.sigmoid(g))).astype(BF16)


def _retention(z, cosf, sinf, dec, zeta, xi, cd, sg, *, chunk, heads=8):
    s = z.shape[0]
    w = heads * HEAD_W
    row = lambda blk: pl.BlockSpec((chunk, w), lambda h, i: (i, blk // heads + h))
    tab = pl.BlockSpec((chunk, HEAD_W), lambda h, i: (i, 0))
    per_head = lambda r, c: pl.BlockSpec((heads, r, c), lambda h, i: (h, 0, 0))
    return pl.pallas_call(
        functools.partial(_ret_kernel, heads=heads),
        out_shape=jax.ShapeDtypeStruct((s, RET_HEADS * HEAD_W), BF16),
        grid=(RET_HEADS // heads, s // chunk),
        in_specs=[
            row(QR_BLK), row(KR_BLK), row(VR_BLK), row(GR_BLK), tab, tab,
            per_head(chunk, chunk), per_head(chunk, HEAD_W), per_head(chunk, HEAD_W), per_head(8, HEAD_W),
            pl.BlockSpec((1, HEAD_W), lambda h, i: (0, 0)),
        ],
        out_specs=pl.BlockSpec((chunk, w), lambda h, i: (i, h)),
        scratch_shapes=[pltpu.VMEM((heads, HEAD_W, HEAD_W), F32)],
        compiler_params=_params(("arbitrary", "arbitrary"), 40),
        name="retention",
    )(z, z, z, z, cosf, sinf, dec, zeta, xi, cd, sg)


def _retention_tables(chunk):
    log_g = jnp.log(1.0 - 2.0 ** (-5.0 - jnp.arange(RET_HEADS, dtype=F32)))
    idx = jnp.arange(chunk, dtype=F32)
    rel = idx[:, None] - idx[None, :]
    dec = jnp.exp(log_g[:, None, None] * jnp.maximum(rel, 0.0)) * (rel >= 0)
    zeta = jnp.exp(log_g[:, None] * (chunk - 1 - idx))
    xi = jnp.exp(log_g[:, None] * (idx + 1.0))
    cd = jnp.exp(log_g * chunk)
    bcast = lambda t: jnp.broadcast_to(t[:, :, None], (RET_HEADS, chunk, HEAD_W))
    return dec, bcast(zeta), bcast(xi), jnp.broadcast_to(cd[:, None, None], (RET_HEADS, 8, HEAD_W))


def _merge_kernel(x_ref, ya_ref, yb_ref, ga0_ref, ga1_ref, gb0_ref, gb1_ref, wa_ref, wb_ref, wo_ref, o_ref):
    ta = jnp.dot(ya_ref[...], wa_ref[...], preferred_element_type=F32)
    tb = jnp.dot(yb_ref[...], wb_ref[...], preferred_element_type=F32)
    ga = jnp.concatenate([ga0_ref[...], ga1_ref[...]], axis=-1).astype(F32)
    gb = jnp.concatenate([gb0_ref[...], gb1_ref[...]], axis=-1).astype(F32)
    merged = jax.nn.sigmoid(ga) * ta + jax.nn.sigmoid(gb) * tb
    o_ref[...] = x_ref[...] + jnp.dot(merged.astype(BF16), wo_ref[...], preferred_element_type=F32)


def _merge(x, ya, yb, z, wa, wb, wo, layer, *, tm=256):
    s, d = x.shape
    gate = lambda blk: pl.BlockSpec((tm, GATE_BLK_W), lambda i: (i, blk))
    resident = lambda w: pl.BlockSpec((None,) + w.shape[1:], lambda i: (layer, 0, 0), pipeline_mode=pl.Buffered(1))
    return pl.pallas_call(
        _merge_kernel,
        out_shape=jax.ShapeDtypeStruct((s, d), F32),
        grid=(s // tm,),
        in_specs=[
            pl.BlockSpec((tm, d), lambda i: (i, 0)),
            pl.BlockSpec((tm, ya.shape[1]), lambda i: (i, 0)),
            pl.BlockSpec((tm, yb.shape[1]), lambda i: (i, 0)),
            gate(GA_BLK), gate(GA_BLK + 1), gate(GB_BLK), gate(GB_BLK + 1),
            resident(wa), resident(wb), resident(wo),
        ],
        out_specs=pl.BlockSpec((tm, d), lambda i: (i, 0)),
        compiler_params=_params(("arbitrary",), 48),
        name="merge_out",
    )(x, ya, yb, z, z, z, z, wa, wb, wo)


def _ple_kernel(x_ref, g_ref, p_ref, wg_ref, wp_ref, o_ref):
    x = x_ref[...]
    h = _rms(x, g_ref[...]).astype(BF16)
    gate = jax.nn.sigmoid(jnp.dot(h, wg_ref[...], preferred_element_type=F32))
    proj = jnp.dot(p_ref[...].astype(BF16), wp_ref[...], preferred_element_type=F32)
    o_ref[...] = x + gate * proj


def _ple(x, g, p, wg, wp, layer, *, tm=512):
    s, d = x.shape
    resident = lambda w: pl.BlockSpec((None,) + w.shape[1:], lambda i: (layer, 0, 0), pipeline_mode=pl.Buffered(1))
    return pl.pallas_call(
        _ple_kernel,
        out_shape=jax.ShapeDtypeStruct((s, d), F32),
        grid=(s // tm,),
        in_specs=[
            pl.BlockSpec((tm, d), lambda i: (i, 0)),
            pl.BlockSpec((1, d), lambda i: (0, 0)),
            pl.BlockSpec((None, tm, p.shape[2]), lambda i: (layer, i, 0)),
            resident(wg), resident(wp),
        ],
        out_specs=pl.BlockSpec((tm, d), lambda i: (i, 0)),
        compiler_params=_params(("arbitrary",), 48),
        name="ple",
    )(x, g, p, wg, wp)


def kernel(x, p, positions, ffn1_norm, ffn1_w_gate, ffn1_w_up, ffn1_w_down, mix_norm, w_in, da_q_norm, da_k_norm, da_lambda_q1, da_lambda_k1, da_lambda_q2, da_lambda_k2, da_sub_norm, ret_sub_norm, w_up_a, w_up_b, w_out, ffn2_norm, ffn2_w_gate, ffn2_w_up, ffn2_w_down, ple_norm, w_ple_gate, w_ple_proj):
    b, s, d = x.shape
    assert b == 1
    depth = w_in.shape[0]
    chunk = min(256, s)

    row = lambda t: t.reshape(1, -1)
    twice = lambda t: jnp.concatenate([t, t]).reshape(1, -1)
    bf = lambda t: t.astype(BF16)

    inv_freq = ROPE_BASE ** (-jnp.arange(0, HEAD_W, 2, dtype=F32) / HEAD_W)
    (cosf, sinf), early = _rope_tables(positions.reshape(s, 1), twice(inv_freq),
                                       [(w, 0, 1) for w in (ffn1_w_gate, ffn1_w_up, ffn1_w_down)])
    dec, zeta, xi, cd = _retention_tables(chunk)

    wpp = bf(w_ple_proj)
    late = [(w, 1, depth - 1) for w in (ffn1_w_gate, ffn1_w_up, ffn1_w_down, w_in)] if depth > 1 else []
    rest = [(w, 0, depth) for w in (ffn2_w_gate, ffn2_w_up, ffn2_w_down, w_up_a, w_up_b, w_out, w_ple_gate)]
    p3 = p.reshape(depth, s, -1)

    xc = x.reshape(s, d)
    for i in range(depth):
        lambda_init = 0.8 - 0.6 * math.exp(-0.3 * i)
        first = 0 if i == 0 else i - 1
        if i == 0:
            xc, (win,) = _ffn(xc, row(ffn1_norm[i]), *early, first, [(w_in, 0, 1)])
        else:
            f1g, f1u, f1d, win = cast_late
            xc, _ = _ffn(xc, row(ffn1_norm[i]), f1g, f1u, f1d, first)
        z = _inproj(xc, row(mix_norm[i]), win, first)
        ya, cast_out = _attn(z, twice(da_q_norm[i]).reshape(-1, 1), twice(da_k_norm[i]), row(da_lambda_q1[i]),
                             row(da_lambda_k1[i]), row(da_lambda_q2[i]), row(da_lambda_k2[i]),
                             da_sub_norm[i].reshape(-1, 1), (late + rest) if i == 0 else (),
                             lambda_init=lambda_init)
        if i == 0:
            cast_late = cast_out[:len(late)]
            f2g, f2u, f2d, wua, wub, wo, wpg = cast_out[len(late):]
        yb = _retention(z, cosf, sinf, dec, zeta, xi, cd, row(ret_sub_norm[i]), chunk=chunk)
        xc = _merge(xc, ya, yb, z, wua, wub, wo, i)
        xc, _ = _ffn(xc, row(ffn2_norm[i]), f2g, f2u, f2d, i)
        xc = _ple(xc, row(ple_norm[i]), p3, wpg, wpp, i)
    return xc.reshape(b, s, d)
```

```python
import functools
import math

import jax
import jax.numpy as jnp
from jax import lax
from jax.experimental import pallas as pl
from jax.experimental.pallas import tpu as pltpu

F32 = jnp.float32
BF16 = jnp.bfloat16

EPS = 1e-6
ROPE_BASE = 10000.0
DA_HEADS = 8
DA_HEAD_DIM = 64
RET_HEADS = 8
HEAD_W = 128
RET_SCALE = HEAD_W ** -0.5
DA_SCALE = DA_HEAD_DIM ** -0.5
NEG_BIG = -1e30
LOG2E = math.log2(math.e)
BF16_SUBLANES = 16
NORM_CHUNKS = 4
ACC_ROWS = HEAD_W + BF16_SUBLANES

QA_BLK, KA_BLK, VA_BLK, QR_BLK, KR_BLK, VR_BLK, GR_BLK = 0, 8, 16, 24, 32, 40, 48
GATE_BLK_W = 1024
GA_BLK, GB_BLK = 7, 9

MIB = 1024 * 1024


def _params(sem, vmem_mib, flags=None):
    return pltpu.CompilerParams(dimension_semantics=sem, vmem_limit_bytes=vmem_mib * MIB, flags=flags)


def _rms(x, g):
    return x * lax.rsqrt(jnp.mean(x * x, axis=-1, keepdims=True) + EPS) * g


def _ffn_kernel(x_ref, g_ref, wg_ref, wu_ref, wd_ref, *rest, n_cast):
    w_refs, o_ref, c_refs, h_ref = rest[:n_cast], rest[n_cast], rest[n_cast + 1:-1], rest[-1]
    for w_ref, c_ref in zip(w_refs, c_refs):
        c_ref[...] = w_ref[...].astype(BF16)
    j = pl.program_id(1)

    def swiglu_down(h):
        gate = jnp.dot(h, wg_ref[...], preferred_element_type=F32)
        up = jnp.dot(h, wu_ref[...], preferred_element_type=F32)
        act = (gate * jax.nn.sigmoid(gate) * up * 0.5).astype(BF16)
        return jnp.dot(act, wd_ref[...], preferred_element_type=F32)

    @pl.when(j == 0)
    def _():
        rows = x_ref.shape[0] // NORM_CHUNKS
        for c in range(NORM_CHUNKS):
            sl = slice(c * rows, (c + 1) * rows)
            x = x_ref[sl, :]
            h = _rms(x, g_ref[...]).astype(BF16)
            h_ref[sl, :] = h
            o_ref[sl, :] = x + swiglu_down(h)

    @pl.when(j > 0)
    def _():
        o_ref[...] += swiglu_down(h_ref[...])


def _ffn(x, g, wg, wu, wd, layer, cast=(), *, tm=1024, tf=512):
    s, d = x.shape
    tm = min(tm, s)
    f = wg.shape[2]
    grid = (s // tm, f // tf)
    cast_specs = [_cast_specs(w, first, count, grid[0] * grid[1], grid[1]) for w, first, count in cast]
    out = pl.pallas_call(
        functools.partial(_ffn_kernel, n_cast=len(cast)),
        out_shape=[jax.ShapeDtypeStruct((s, d), F32)] + [c[2] for c in cast_specs],
        grid=grid,
        in_specs=[
            pl.BlockSpec((tm, d), lambda i, j: (i, 0)),
            pl.BlockSpec((1, d), lambda i, j: (0, 0)),
            pl.BlockSpec((None, d, tf), lambda i, j: (layer, 0, j)),
            pl.BlockSpec((None, d, tf), lambda i, j: (layer, 0, j)),
            pl.BlockSpec((None, tf, d), lambda i, j: (layer, j, 0)),
        ] + [c[0] for c in cast_specs],
        out_specs=[pl.BlockSpec((tm, d), lambda i, j: (i, 0))] + [c[1] for c in cast_specs],
        scratch_shapes=[pltpu.VMEM((tm, d), BF16)],
        compiler_params=_params(("arbitrary", "arbitrary"), 58),
        name="ffn",
    )(x, g, wg, wu, wd, *[c[0] for c in cast])
    return out[0], out[1:]


def _inproj_kernel(x_ref, g_ref, w_ref, z_ref, h_ref):
    j = pl.program_id(1)

    @pl.when(j == 0)
    def _():
        rows = x_ref.shape[0] // NORM_CHUNKS
        for c in range(NORM_CHUNKS):
            sl = slice(c * rows, (c + 1) * rows)
            h = _rms(x_ref[sl, :], g_ref[...]).astype(BF16)
            h_ref[sl, :] = h
            z_ref[sl, :] = jnp.dot(h, w_ref[...], preferred_element_type=F32).astype(BF16)

    @pl.when(j > 0)
    def _():
        z_ref[...] = jnp.dot(h_ref[...], w_ref[...], preferred_element_type=F32).astype(BF16)


def _inproj(x, g, w, layer, *, tm=512, tn=2816):
    s, d = x.shape
    tm = min(tm, s)
    n = w.shape[2]
    return pl.pallas_call(
        _inproj_kernel,
        out_shape=jax.ShapeDtypeStruct((s, n), BF16),
        grid=(s // tm, n // tn),
        in_specs=[
            pl.BlockSpec((tm, d), lambda i, j: (i, 0)),
            pl.BlockSpec((1, d), lambda i, j: (0, 0)),
            pl.BlockSpec((None, d, tn), lambda i, j: (layer, 0, j)),
        ],
        out_specs=pl.BlockSpec((tm, tn), lambda i, j: (i, j)),
        scratch_shapes=[pltpu.VMEM((tm, d), BF16)],
        compiler_params=_params(("arbitrary", "arbitrary"), 48),
        name="in_proj",
    )(x, g, w)


def _half_norm(x, g, same_half):
    xx = x * x
    hi = xx.astype(BF16)
    lo = (xx - hi.astype(F32)).astype(BF16)
    ss = (jnp.dot(hi, same_half, preferred_element_type=F32)
          + jnp.dot(lo, same_half, preferred_element_type=F32))
    return x * lax.rsqrt(ss * (1.0 / DA_HEAD_DIM) + EPS) * g


def _attn_kernel(*refs, n_cast, tq, tk, cw, unroll, lambda_init):
    q_ref, k_ref, v_ref, qg_ref, kg_ref, lq1_ref, lk1_ref, lq2_ref, lk2_ref, sg_ref = refs[:10]
    w_refs = refs[10:10 + n_cast]
    o_ref = refs[10 + n_cast]
    c_refs = refs[11 + n_cast:11 + 2 * n_cast]
    kn_ref, vt_ref, qt_ref, m_ref, acc_ref, s0_ref, s1_ref, mx0_ref, mx1_ref, ref_ref = refs[11 + 2 * n_cast:]

    i = pl.program_id(1)
    s_len = k_ref.shape[0]
    prep_rows = min(512, s_len)

    @pl.when(i == 0)
    def _():
        vt_ref[HEAD_W:ACC_ROWS, :] = jnp.ones((ACC_ROWS - HEAD_W, s_len), BF16)
        half = lambda axis: lax.broadcasted_iota(jnp.int32, (HEAD_W, HEAD_W), axis) < DA_HEAD_DIM
        same_half = (half(0) == half(1)).astype(BF16)

        def body(c, carry):
            r = pl.multiple_of(c * prep_rows, prep_rows)
            kk = k_ref[pl.ds(r, prep_rows), :].astype(F32)
            kn_ref[pl.ds(r, prep_rows), :] = _half_norm(kk, kg_ref[...], same_half).astype(BF16)
            vv = v_ref[pl.ds(r, prep_rows), :].astype(F32)
            vt_ref[0:HEAD_W, pl.ds(r, prep_rows)] = vv.T.astype(BF16)
            return carry
        lax.fori_loop(0, s_len // prep_rows, body, 0)

    dh = DA_HEAD_DIM
    qt = q_ref[...].astype(F32).T
    qq = qt * qt
    qscale = DA_SCALE * LOG2E
    inv1 = lax.rsqrt(jnp.sum(qq[0:dh], axis=0, keepdims=True) * (1.0 / dh) + EPS) * qscale
    inv2 = lax.rsqrt(jnp.sum(qq[dh:HEAD_W], axis=0, keepdims=True) * (1.0 / dh) + EPS) * qscale
    zeros = jnp.zeros((dh, tq), BF16)
    qt_ref[0:dh, 0:tq] = (qt[0:dh] * inv1 * qg_ref[0:dh, :]).astype(BF16)
    qt_ref[dh:HEAD_W, 0:tq] = zeros
    qt_ref[0:dh, tq:2 * tq] = zeros
    qt_ref[dh:HEAD_W, tq:2 * tq] = (qt[dh:HEAD_W] * inv2 * qg_ref[dh:HEAD_W, :]).astype(BF16)
    m_ref[...] = jnp.full(m_ref.shape, NEG_BIG, F32)
    acc_ref[...] = jnp.zeros(acc_ref.shape, F32)

    groups = [(c * cw, (c + 1) * cw) for c in range(2 * tq // cw)]

    def softmax_update(a, b, s, s_max, vt):
        m_prev = m_ref[:, a:b]
        m_new = jnp.maximum(m_prev, s_max)
        alpha = jnp.exp2(m_prev - m_new)
        if s.dtype == BF16:
            p = jnp.exp2(s - (m_new - ref_ref[:, a:b]).astype(BF16))
        else:
            p = jnp.exp2((s - m_new).astype(BF16))
        acc_ref[:, a:b] = alpha * acc_ref[:, a:b] + jnp.dot(vt, p, preferred_element_type=F32)
        m_ref[:, a:b] = m_new

    def scores(jb, a, b, s_ref, mx_ref, first=False):
        r = pl.multiple_of(jb * tk, tk)
        s = jnp.dot(kn_ref[pl.ds(r, tk), :], qt_ref[:, a:b], preferred_element_type=F32)
        s_max = jnp.max(s, axis=0, keepdims=True)
        if first:
            ref_ref[:, a:b] = s_max
        s_ref[a // cw] = (s - (s_max if first else ref_ref[:, a:b])).astype(BF16)
        mx_ref[:, a:b] = s_max

    def update(jb, a, b, s_ref, mx_ref):
        r = pl.multiple_of(jb * tk, tk)
        softmax_update(a, b, s_ref[a // cw], mx_ref[:, a:b], vt_ref[:, pl.ds(r, tk)])

    n_full = i * (tq // tk)
    for a, b in groups:
        scores(0, a, b, s0_ref, mx0_ref, first=True)

    def pair(jb):
        for a, b in groups:
            scores(jb + 1, a, b, s1_ref, mx1_ref)
            update(jb, a, b, s0_ref, mx0_ref)
        for a, b in groups:
            scores(jb + 2, a, b, s0_ref, mx0_ref)
            update(jb + 1, a, b, s1_ref, mx1_ref)

    def unrolled_pairs(t, carry):
        for u in range(unroll):
            pair(2 * unroll * t + 2 * u)
        return carry

    def one_pair(t, carry):
        pair(n_full - n_full % (2 * unroll) + 2 * t)
        return carry

    lax.fori_loop(0, n_full // (2 * unroll), unrolled_pairs, 0)
    lax.fori_loop(0, (n_full % (2 * unroll)) // 2, one_pair, 0)

    for w_ref, c_ref in zip(w_refs, c_refs):
        c_ref[...] = w_ref[...].astype(BF16)

    def key_rows(d, a):
        q0 = a % tq
        return d * tk, min((d + 1) * tk, q0 + cw), q0

    slots = (s0_ref, s1_ref)
    for d in range(tq // tk):
        r = pl.multiple_of((n_full + d) * tk, tk)
        r_next = pl.multiple_of((n_full + d + 1) * tk, tk)
        for a, b in groups:
            k0, k1, _ = key_rows(d + 1, a)
            if d + 1 < tq // tk and k1 > k0:
                slots[(d + 1) % 2][a // cw, 0:k1 - k0, :] = (jnp.dot(
                    kn_ref[pl.ds(r_next, k1 - k0), :], qt_ref[:, a:b], preferred_element_type=F32)
                    - ref_ref[:, a:b]).astype(BF16)
            k0, k1, q0 = key_rows(d, a)
            if k1 <= k0:
                continue
            s = slots[d % 2][a // cw, 0:k1 - k0, :].astype(F32) + ref_ref[:, a:b]
            if k1 - 1 > q0:
                key = k0 + lax.broadcasted_iota(jnp.int32, s.shape, 0)
                qry = q0 + lax.broadcasted_iota(jnp.int32, s.shape, 1)
                s = jnp.where(key <= qry, s, NEG_BIG)
            softmax_update(a, b, s, jnp.max(s, axis=0, keepdims=True), vt_ref[:, pl.ds(r, k1 - k0)])

    o1 = acc_ref[0:HEAD_W, 0:tq] * (1.0 / acc_ref[HEAD_W:HEAD_W + 1, 0:tq])
    o2 = acc_ref[0:HEAD_W, tq:2 * tq] * (1.0 / acc_ref[HEAD_W:HEAD_W + 1, tq:2 * tq])
    lam = (jnp.exp(jnp.sum(lq1_ref[...] * lk1_ref[...], axis=-1, keepdims=True))
           - jnp.exp(jnp.sum(lq2_ref[...] * lk2_ref[...], axis=-1, keepdims=True)) + lambda_init)
    o = o1 - lam * o2
    inv = lax.rsqrt(jnp.mean(o * o, axis=0, keepdims=True) + EPS) * (1.0 - lambda_init)
    o_ref[...] = (o * inv * sg_ref[...]).T.astype(BF16)


def _cast_specs(w, first, count, n_steps, n_inner):
    r, c = w.shape[1], w.shape[2]
    per_layer = n_steps // count
    assert per_layer * count == n_steps and first + count <= w.shape[0]
    for nb_c in range(1, per_layer + 1):
        nb_r = per_layer // nb_c
        if per_layer % nb_c or c % nb_c or r % nb_r:
            continue
        br, bc = r // nb_r, c // nb_c
        if br % BF16_SUBLANES == 0 and bc % HEAD_W == 0:
            break
    else:
        raise ValueError(f"no slab split for {w.shape}")

    def index(offset):
        def index_map(h, i):
            t = h * n_inner + i
            rem = t % per_layer
            return (t // per_layer + offset, rem // nb_c, rem % nb_c)
        return index_map

    return (pl.BlockSpec((None, br, bc), index(first)), pl.BlockSpec((None, br, bc), index(0)),
            jax.ShapeDtypeStruct((count, r, c), BF16))


def _attn(z, qg, kg, lq1, lk1, lq2, lk2, sg, cast=(), *, lambda_init, tq=1024, tk=512, cw=256, unroll=4):
    s = z.shape[0]
    assert s % tq == 0 and tq % (2 * tk) == 0 and tq % cw == 0
    small = lambda w: pl.BlockSpec((1, w), lambda h, i: (0, 0))
    column = pl.BlockSpec((HEAD_W, 1), lambda h, i: (0, 0))
    n_inner = s // tq
    cast_specs = [_cast_specs(w, first, count, DA_HEADS * n_inner, n_inner) for w, first, count in cast]
    kern = functools.partial(_attn_kernel, n_cast=len(cast), tq=tq, tk=tk, cw=cw, unroll=unroll,
                             lambda_init=lambda_init)
    out = pl.pallas_call(
        kern,
        out_shape=[jax.ShapeDtypeStruct((s, DA_HEADS * HEAD_W), BF16)] + [c[2] for c in cast_specs],
        grid=(DA_HEADS, n_inner),
        in_specs=[
            pl.BlockSpec((tq, HEAD_W), lambda h, i: (i, QA_BLK + h)),
            pl.BlockSpec((s, HEAD_W), lambda h, i: (0, KA_BLK + h)),
            pl.BlockSpec((s, HEAD_W), lambda h, i: (0, VA_BLK + h)),
            column, small(HEAD_W),
            small(DA_HEAD_DIM), small(DA_HEAD_DIM), small(DA_HEAD_DIM), small(DA_HEAD_DIM),
            column,
        ] + [c[0] for c in cast_specs],
        out_specs=[pl.BlockSpec((tq, HEAD_W), lambda h, i: (i, h))] + [c[1] for c in cast_specs],
        scratch_shapes=[
            pltpu.VMEM((s, HEAD_W), BF16),
            pltpu.VMEM((ACC_ROWS, s), BF16),
            pltpu.VMEM((HEAD_W, 2 * tq), BF16),
            pltpu.VMEM((1, 2 * tq), F32),
            pltpu.VMEM((ACC_ROWS, 2 * tq), F32),
            pltpu.VMEM((2 * tq // cw, tk, cw), BF16),
            pltpu.VMEM((2 * tq // cw, tk, cw), BF16),
            pltpu.VMEM((1, 2 * tq), F32),
            pltpu.VMEM((1, 2 * tq), F32),
            pltpu.VMEM((1, 2 * tq), F32),
        ],
        compiler_params=_params(("arbitrary", "arbitrary"), 58),
        name="diff_attn",
    )(z, z, z, qg, kg, lq1, lk1, lq2, lk2, sg, *[c[0] for c in cast])
    return out[0], out[1:]


def _rope_kernel(pos_ref, invf_ref, *rest, n_cast):
    w_refs, (cos_ref, sin_ref), c_refs = rest[:n_cast], rest[n_cast:n_cast + 2], rest[n_cast + 2:]
    for w_ref, c_ref in zip(w_refs, c_refs):
        c_ref[...] = w_ref[...].astype(BF16)
    ang = pos_ref[...].astype(F32) * invf_ref[...]
    lo = lax.broadcasted_iota(jnp.int32, (1, HEAD_W), 1) < HEAD_W // 2
    cos_ref[...] = jnp.cos(ang)
    sin_ref[...] = jnp.where(lo, -1.0, 1.0) * jnp.sin(ang)


def _rope_tables(pos_col, invf, cast=(), *, tm=1024):
    s = pos_col.shape[0]
    tm = min(tm, s)
    n = s // tm
    cast_specs = [_cast_specs(w, first, count, n, 1) for w, first, count in cast]
    table = pl.BlockSpec((tm, HEAD_W), lambda i, _: (i, 0))
    out = pl.pallas_call(
        functools.partial(_rope_kernel, n_cast=len(cast)),
        out_shape=[jax.ShapeDtypeStruct((s, HEAD_W), F32)] * 2 + [c[2] for c in cast_specs],
        grid=(n, 1),
        in_specs=[pl.BlockSpec((tm, 1), lambda i, _: (i, 0)), pl.BlockSpec((1, HEAD_W), lambda i, _: (0, 0))]
        + [c[0] for c in cast_specs],
        out_specs=[table, table] + [c[1] for c in cast_specs],
        compiler_params=_params(("arbitrary", "arbitrary"), 40),
        name="rope_tables",
    )(pos_col, invf, *[c[0] for c in cast])
    return out[:2], out[2:]


def _ret_kernel(q_ref, k_ref, v_ref, gr_ref, cos_ref, sin_ref, dec_ref, zeta_ref, xi_ref, cd_ref, sg_ref,
                o_ref, st_ref, *, heads):
    @pl.when(pl.program_id(1) == 0)
    def _():
        st_ref[...] = jnp.zeros(st_ref.shape, F32)

    cosf = cos_ref[...]
    sinf = sin_ref[...]
    for hh in range(heads):
        lanes = slice(hh * HEAD_W, (hh + 1) * HEAD_W)
        q = q_ref[:, lanes].astype(F32)
        k = k_ref[:, lanes].astype(F32)
        q = q * cosf + pltpu.roll(q, HEAD_W // 2, 1) * sinf
        k = (k * cosf + pltpu.roll(k, HEAD_W // 2, 1) * sinf) * RET_SCALE
        v = v_ref[:, lanes]
        st = st_ref[hh]

        sc = lax.dot_general(q.astype(BF16), k.astype(BF16), (((1,), (1,)), ((), ())),
                             preferred_element_type=F32) * dec_ref[hh]
        o = jnp.dot(sc.astype(BF16), v, preferred_element_type=F32)
        o = o + jnp.dot((q * xi_ref[hh]).astype(BF16), st.astype(BF16), preferred_element_type=F32)
        kz = (k * zeta_ref[hh]).astype(BF16)
        kv = lax.dot_general(kz, v, (((0,), (0,)), ((), ())), preferred_element_type=F32)
        st_ref[hh] = cd_ref[hh, 0:1, :] * st + kv

        g = gr_ref[:, lanes].astype(F32)
        o_ref[:, lanes] = (_rms(o, sg_ref[...]) * (g * jax.nn.sigmoid(g))).astype(BF16)


def _retention(z, cosf, sinf, dec, zeta, xi, cd, sg, *, chunk, heads=8):
    s = z.shape[0]
    w = heads * HEAD_W
    row = lambda blk: pl.BlockSpec((chunk, w), lambda h, i: (i, blk // heads + h))
    tab = pl.BlockSpec((chunk, HEAD_W), lambda h, i: (i, 0))
    per_head = lambda r, c: pl.BlockSpec((heads, r, c), lambda h, i: (h, 0, 0))
    return pl.pallas_call(
        functools.partial(_ret_kernel, heads=heads),
        out_shape=jax.ShapeDtypeStruct((s, RET_HEADS * HEAD_W), BF16),
        grid=(RET_HEADS // heads, s // chunk),
        in_specs=[
            row(QR_BLK), row(KR_BLK), row(VR_BLK), row(GR_BLK), tab, tab,
            per_head(chunk, chunk), per_head(chunk, HEAD_W), per_head(chunk, HEAD_W), per_head(8, HEAD_W),
            pl.BlockSpec((1, HEAD_W), lambda h, i: (0, 0)),
        ],
        out_specs=pl.BlockSpec((chunk, w), lambda h, i: (i, h)),
        scratch_shapes=[pltpu.VMEM((heads, HEAD_W, HEAD_W), F32)],
        compiler_params=_params(("arbitrary", "arbitrary"), 40),
        name="retention",
    )(z, z, z, z, cosf, sinf, dec, zeta, xi, cd, sg)


def _retention_tables(chunk):
    log_g = jnp.log(1.0 - 2.0 ** (-5.0 - jnp.arange(RET_HEADS, dtype=F32)))
    idx = jnp.arange(chunk, dtype=F32)
    rel = idx[:, None] - idx[None, :]
    dec = jnp.exp(log_g[:, None, None] * jnp.maximum(rel, 0.0)) * (rel >= 0)
    zeta = jnp.exp(log_g[:, None] * (chunk - 1 - idx))
    xi = jnp.exp(log_g[:, None] * (idx + 1.0))
    cd = jnp.exp(log_g * chunk)
    bcast = lambda t: jnp.broadcast_to(t[:, :, None], (RET_HEADS, chunk, HEAD_W))
    return dec, bcast(zeta), bcast(xi), jnp.broadcast_to(cd[:, None, None], (RET_HEADS, 8, HEAD_W))


def _merge_kernel(x_ref, ya_ref, yb_ref, ga0_ref, ga1_ref, gb0_ref, gb1_ref, wa_ref, wb_ref, wo_ref, o_ref):
    ta = jnp.dot(ya_ref[...], wa_ref[...], preferred_element_type=F32)
    tb = jnp.dot(yb_ref[...], wb_ref[...], preferred_element_type=F32)
    ga = jnp.concatenate([ga0_ref[...], ga1_ref[...]], axis=-1).astype(F32)
    gb = jnp.concatenate([gb0_ref[...], gb1_ref[...]], axis=-1).astype(F32)
    merged = jax.nn.sigmoid(ga) * ta + jax.nn.sigmoid(gb) * tb
    o_ref[...] = x_ref[...] + jnp.dot(merged.astype(BF16), wo_ref[...], preferred_element_type=F32)


def _merge(x, ya, yb, z, wa, wb, wo, layer, *, tm=256):
    s, d = x.shape
    gate = lambda blk: pl.BlockSpec((tm, GATE_BLK_W), lambda i: (i, blk))
    resident = lambda w: pl.BlockSpec((None,) + w.shape[1:], lambda i: (layer, 0, 0), pipeline_mode=pl.Buffered(1))
    return pl.pallas_call(
        _merge_kernel,
        out_shape=jax.ShapeDtypeStruct((s, d), F32),
        grid=(s // tm,),
        in_specs=[
            pl.BlockSpec((tm, d), lambda i: (i, 0)),
            pl.BlockSpec((tm, ya.shape[1]), lambda i: (i, 0)),
            pl.BlockSpec((tm, yb.shape[1]), lambda i: (i, 0)),
            gate(GA_BLK), gate(GA_BLK + 1), gate(GB_BLK), gate(GB_BLK + 1),
            resident(wa), resident(wb), resident(wo),
        ],
        out_specs=pl.BlockSpec((tm, d), lambda i: (i, 0)),
        compiler_params=_params(("arbitrary",), 48),
        name="merge_out",
    )(x, ya, yb, z, z, z, z, wa, wb, wo)


def _ple_kernel(x_ref, g_ref, p_ref, wg_ref, wp_ref, o_ref):
    x = x_ref[...]
    h = _rms(x, g_ref[...]).astype(BF16)
    gate = jax.nn.sigmoid(jnp.dot(h, wg_ref[...], preferred_element_type=F32))
    proj = jnp.dot(p_ref[...].astype(BF16), wp_ref[...], preferred_element_type=F32)
    o_ref[...] = x + gate * proj


def _ple(x, g, p, wg, wp, layer, *, tm=512):
    s, d = x.shape
    resident = lambda w: pl.BlockSpec((None,) + w.shape[1:], lambda i: (layer, 0, 0), pipeline_mode=pl.Buffered(1))
    return pl.pallas_call(
        _ple_kernel,
        out_shape=jax.ShapeDtypeStruct((s, d), F32),
        grid=(s // tm,),
        in_specs=[
            pl.BlockSpec((tm, d), lambda i: (i, 0)),
            pl.BlockSpec((1, d), lambda i: (0, 0)),
            pl.BlockSpec((None, tm, p.shape[2]), lambda i: (layer, i, 0)),
            resident(wg), resident(wp),
        ],
        out_specs=pl.BlockSpec((tm, d), lambda i: (i, 0)),
        compiler_params=_params(("arbitrary",), 48),
        name="ple",
    )(x, g, p, wg, wp)


def kernel(x, p, positions, ffn1_norm, ffn1_w_gate, ffn1_w_up, ffn1_w_down, mix_norm, w_in, da_q_norm, da_k_norm, da_lambda_q1, da_lambda_k1, da_lambda_q2, da_lambda_k2, da_sub_norm, ret_sub_norm, w_up_a, w_up_b, w_out, ffn2_norm, ffn2_w_gate, ffn2_w_up, ffn2_w_down, ple_norm, w_ple_gate, w_ple_proj):
    b, s, d = x.shape
    assert b == 1
    depth = w_in.shape[0]
    chunk = min(256, s)

    row = lambda t: t.reshape(1, -1)
    twice = lambda t: jnp.concatenate([t, t]).reshape(1, -1)
    bf = lambda t: t.astype(BF16)

    inv_freq = ROPE_BASE ** (-jnp.arange(0, HEAD_W, 2, dtype=F32) / HEAD_W)
    (cosf, sinf), early = _rope_tables(positions.reshape(s, 1), twice(inv_freq),
                                       [(w, 0, 1) for w in (ffn1_w_gate, ffn1_w_up, ffn1_w_down)])
    dec, zeta, xi, cd = _retention_tables(chunk)

    wpp = bf(w_ple_proj)
    late = [(w, 1, depth - 1) for w in (ffn1_w_gate, ffn1_w_up, ffn1_w_down, w_in)] if depth > 1 else []
    rest = [(w, 0, depth) for w in (ffn2_w_gate, ffn2_w_up, ffn2_w_down, w_up_a, w_up_b, w_out, w_ple_gate)]
    p3 = p.reshape(depth, s, -1)

    xc = x.reshape(s, d)
    for i in range(depth):
        lambda_init = 0.8 - 0.6 * math.exp(-0.3 * i)
        first = 0 if i == 0 else i - 1
        if i == 0:
            xc, (win,) = _ffn(xc, row(ffn1_norm[i]), *early, first, [(w_in, 0, 1)])
        else:
            f1g, f1u, f1d, win = cast_late
            xc, _ = _ffn(xc, row(ffn1_norm[i]), f1g, f1u, f1d, first)
        z = _inproj(xc, row(mix_norm[i]), win, first)
        ya, cast_out = _attn(z, twice(da_q_norm[i]).reshape(-1, 1), twice(da_k_norm[i]), row(da_lambda_q1[i]),
                             row(da_lambda_k1[i]), row(da_lambda_q2[i]), row(da_lambda_k2[i]),
                             da_sub_norm[i].reshape(-1, 1), (late + rest) if i == 0 else (),
                             lambda_init=lambda_init)
        if i == 0:
            cast_late = cast_out[:len(late)]
            f2g, f2u, f2d, wua, wub, wo, wpg = cast_out[len(late):]
        yb = _retention(z, cosf, sinf, dec, zeta, xi, cd, row(ret_sub_norm[i]), chunk=chunk)
        xc = _merge(xc, ya, yb, z, wua, wub, wo, i)
        xc, _ = _ffn(xc, row(ffn2_norm[i]), f2g, f2u, f2d, i)
        xc = _ple(xc, row(ple_norm[i]), p3, wpg, wpp, i)
    return xc.reshape(b, s, d)
```

```python
import functools
import math

import jax
import jax.numpy as jnp
from jax import lax
from jax.experimental import pallas as pl
from jax.experimental.pallas import tpu as pltpu

F32 = jnp.float32
BF16 = jnp.bfloat16

EPS = 1e-6
ROPE_BASE = 10000.0
DA_HEADS = 8
DA_HEAD_DIM = 64
RET_HEADS = 8
HEAD_W = 128
RET_SCALE = HEAD_W ** -0.5
DA_SCALE = DA_HEAD_DIM ** -0.5
NEG_BIG = -1e30
LOG2E = math.log2(math.e)
BF16_SUBLANES = 16
NORM_CHUNKS = 4
ACC_ROWS = HEAD_W + BF16_SUBLANES

QA_BLK, KA_BLK, VA_BLK, QR_BLK, KR_BLK, VR_BLK, GR_BLK = 0, 8, 16, 24, 32, 40, 48
GATE_BLK_W = 1024
GA_BLK, GB_BLK = 7, 9

MIB = 1024 * 1024


def _params(sem, vmem_mib, flags=None):
    return pltpu.CompilerParams(dimension_semantics=sem, vmem_limit_bytes=vmem_mib * MIB, flags=flags)


def _rms(x, g):
    return x * lax.rsqrt(jnp.mean(x * x, axis=-1, keepdims=True) + EPS) * g


def _ffn_kernel(x_ref, g_ref, wg_ref, wu_ref, wd_ref, *rest, n_cast):
    w_refs, o_ref, c_refs, h_ref = rest[:n_cast], rest[n_cast], rest[n_cast + 1:-1], rest[-1]
    for w_ref, c_ref in zip(w_refs, c_refs):
        c_ref[...] = w_ref[...].astype(BF16)
    j = pl.program_id(1)

    def swiglu_down(h):
        gate = jnp.dot(h, wg_ref[...], preferred_element_type=F32)
        up = jnp.dot(h, wu_ref[...], preferred_element_type=F32)
        act = (gate * jax.nn.sigmoid(gate) * up * 0.5).astype(BF16)
        return jnp.dot(act, wd_ref[...], preferred_element_type=F32)

    @pl.when(j == 0)
    def _():
        rows = x_ref.shape[0] // NORM_CHUNKS
        for c in range(NORM_CHUNKS):
            sl = slice(c * rows, (c + 1) * rows)
            x = x_ref[sl, :]
            h = _rms(x, g_ref[...]).astype(BF16)
            h_ref[sl, :] = h
            o_ref[sl, :] = x + swiglu_down(h)

    @pl.when(j > 0)
    def _():
        o_ref[...] += swiglu_down(h_ref[...])


def _ffn(x, g, wg, wu, wd, layer, cast=(), *, tm=1024, tf=512):
    s, d = x.shape
    tm = min(tm, s)
    f = wg.shape[2]
    grid = (s // tm, f // tf)
    cast_specs = [_cast_specs(w, first, count, grid[0] * grid[1], grid[1]) for w, first, count in cast]
    out = pl.pallas_call(
        functools.partial(_ffn_kernel, n_cast=len(cast)),
        out_shape=[jax.ShapeDtypeStruct((s, d), F32)] + [c[2] for c in cast_specs],
        grid=grid,
        in_specs=[
            pl.BlockSpec((tm, d), lambda i, j: (i, 0)),
            pl.BlockSpec((1, d), lambda i, j: (0, 0)),
            pl.BlockSpec((None, d, tf), lambda i, j: (layer, 0, j)),
            pl.BlockSpec((None, d, tf), lambda i, j: (layer, 0, j)),
            pl.BlockSpec((None, tf, d), lambda i, j: (layer, j, 0)),
        ] + [c[0] for c in cast_specs],
        out_specs=[pl.BlockSpec((tm, d), lambda i, j: (i, 0))] + [c[1] for c in cast_specs],
        scratch_shapes=[pltpu.VMEM((tm, d), BF16)],
        compiler_params=_params(("arbitrary", "arbitrary"), 58),
        name="ffn",
    )(x, g, wg, wu, wd, *[c[0] for c in cast])
    return out[0], out[1:]


def _inproj_kernel(x_ref, g_ref, w_ref, z_ref, h_ref):
    j = pl.program_id(1)

    @pl.when(j == 0)
    def _():
        rows = x_ref.shape[0] // NORM_CHUNKS
        for c in range(NORM_CHUNKS):
            sl = slice(c * rows, (c + 1) * rows)
            h = _rms(x_ref[sl, :], g_ref[...]).astype(BF16)
            h_ref[sl, :] = h
            z_ref[sl, :] = jnp.dot(h, w_ref[...], preferred_element_type=F32).astype(BF16)

    @pl.when(j > 0)
    def _():
        z_ref[...] = jnp.dot(h_ref[...], w_ref[...], preferred_element_type=F32).astype(BF16)


def _inproj(x, g, w, layer, *, tm=1024, tn=2816):
    s, d = x.shape
    tm = min(tm, s)
    n = w.shape[2]
    return pl.pallas_call(
        _inproj_kernel,
        out_shape=jax.ShapeDtypeStruct((s, n), BF16),
        grid=(s // tm, n // tn),
        in_specs=[
            pl.BlockSpec((tm, d), lambda i, j: (i, 0)),
            pl.BlockSpec((1, d), lambda i, j: (0, 0)),
            pl.BlockSpec((None, d, tn), lambda i, j: (layer, 0, j)),
        ],
        out_specs=pl.BlockSpec((tm, tn), lambda i, j: (i, j)),
        scratch_shapes=[pltpu.VMEM((tm, d), BF16)],
        compiler_params=_params(("arbitrary", "arbitrary"), 60),
        name="in_proj",
    )(x, g, w)


def _half_norm(x, g, same_half):
    xx = x * x
    hi = xx.astype(BF16)
    lo = (xx - hi.astype(F32)).astype(BF16)
    ss = (jnp.dot(hi, same_half, preferred_element_type=F32)
          + jnp.dot(lo, same_half, preferred_element_type=F32))
    return x * lax.rsqrt(ss * (1.0 / DA_HEAD_DIM) + EPS) * g


def _attn_kernel(*refs, n_cast, tq, tk, cw, unroll, lambda_init):
    q_ref, k_ref, v_ref, qg_ref, kg_ref, lq1_ref, lk1_ref, lq2_ref, lk2_ref, sg_ref = refs[:10]
    w_refs = refs[10:10 + n_cast]
    o_ref = refs[10 + n_cast]
    c_refs = refs[11 + n_cast:11 + 2 * n_cast]
    kn_ref, vt_ref, qt_ref, m_ref, acc_ref, s0_ref, s1_ref, mx0_ref, mx1_ref, ref_ref = refs[11 + 2 * n_cast:]

    i = pl.program_id(1)
    s_len = k_ref.shape[0]
    prep_rows = min(1024, s_len)

    @pl.when(i == 0)
    def _():
        vt_ref[HEAD_W:ACC_ROWS, :] = jnp.ones((ACC_ROWS - HEAD_W, s_len), BF16)
        half = lambda axis: lax.broadcasted_iota(jnp.int32, (HEAD_W, HEAD_W), axis) < DA_HEAD_DIM
        same_half = (half(0) == half(1)).astype(BF16)

        def body(c, carry):
            r = pl.multiple_of(c * prep_rows, prep_rows)
            kk = k_ref[pl.ds(r, prep_rows), :].astype(F32)
            kn_ref[pl.ds(r, prep_rows), :] = _half_norm(kk, kg_ref[...], same_half).astype(BF16)
            vv = v_ref[pl.ds(r, prep_rows), :].astype(F32)
            vt_ref[0:HEAD_W, pl.ds(r, prep_rows)] = vv.T.astype(BF16)
            return carry
        lax.fori_loop(0, s_len // prep_rows, body, 0)

    dh = DA_HEAD_DIM
    qt = q_ref[...].astype(F32).T
    qq = qt * qt
    qscale = DA_SCALE * LOG2E
    inv1 = lax.rsqrt(jnp.sum(qq[0:dh], axis=0, keepdims=True) * (1.0 / dh) + EPS) * qscale
    inv2 = lax.rsqrt(jnp.sum(qq[dh:HEAD_W], axis=0, keepdims=True) * (1.0 / dh) + EPS) * qscale
    zeros = jnp.zeros((dh, tq), BF16)
    qt_ref[0:dh, 0:tq] = (qt[0:dh] * inv1 * qg_ref[0:dh, :]).astype(BF16)
    qt_ref[dh:HEAD_W, 0:tq] = zeros
    qt_ref[0:dh, tq:2 * tq] = zeros
    qt_ref[dh:HEAD_W, tq:2 * tq] = (qt[dh:HEAD_W] * inv2 * qg_ref[dh:HEAD_W, :]).astype(BF16)
    m_ref[...] = jnp.full(m_ref.shape, NEG_BIG, F32)
    acc_ref[...] = jnp.zeros(acc_ref.shape, F32)

    groups = [(c * cw, (c + 1) * cw) for c in range(2 * tq // cw)]

    def softmax_update(a, b, s, s_max, vt):
        m_prev = m_ref[:, a:b]
        m_new = jnp.maximum(m_prev, s_max)
        alpha = jnp.exp2(m_prev - m_new)
        if s.dtype == BF16:
            p = jnp.exp2(s - (m_new - ref_ref[:, a:b]).astype(BF16))
        else:
            p = jnp.exp2((s - m_new).astype(BF16))
        acc_ref[:, a:b] = alpha * acc_ref[:, a:b] + jnp.dot(vt, p, preferred_element_type=F32)
        m_ref[:, a:b] = m_new

    def scores(jb, a, b, s_ref, mx_ref):
        r = pl.multiple_of(jb * tk, tk)
        s = jnp.dot(kn_ref[pl.ds(r, tk), :], qt_ref[:, a:b], preferred_element_type=F32)
        s_ref[a // cw] = (s - ref_ref[:, a:b]).astype(BF16)
        mx_ref[:, a:b] = jnp.max(s, axis=0, keepdims=True)

    def first_scores(a, b):
        head = tk // 4
        s_lo = jnp.dot(kn_ref[0:head, :], qt_ref[:, a:b], preferred_element_type=F32)
        ref = jnp.max(s_lo, axis=0, keepdims=True)
        ref_ref[:, a:b] = ref
        s0_ref[a // cw, 0:head, :] = (s_lo - ref).astype(BF16)
        s_hi = jnp.dot(kn_ref[head:tk, :], qt_ref[:, a:b], preferred_element_type=F32)
        s0_ref[a // cw, head:tk, :] = (s_hi - ref).astype(BF16)
        mx0_ref[:, a:b] = jnp.maximum(ref, jnp.max(s_hi, axis=0, keepdims=True))

    def update(jb, a, b, s_ref, mx_ref):
        r = pl.multiple_of(jb * tk, tk)
        softmax_update(a, b, s_ref[a // cw], mx_ref[:, a:b], vt_ref[:, pl.ds(r, tk)])

    n_full = i * (tq // tk)
    for a, b in groups:
        first_scores(a, b)

    def pair(jb):
        for a, b in groups:
            scores(jb + 1, a, b, s1_ref, mx1_ref)
            update(jb, a, b, s0_ref, mx0_ref)
        for a, b in groups:
            scores(jb + 2, a, b, s0_ref, mx0_ref)
            update(jb + 1, a, b, s1_ref, mx1_ref)

    def unrolled_pairs(t, carry):
        for u in range(unroll):
            pair(2 * unroll * t + 2 * u)
        return carry

    def one_pair(t, carry):
        pair(n_full - n_full % (2 * unroll) + 2 * t)
        return carry

    lax.fori_loop(0, n_full // (2 * unroll), unrolled_pairs, 0)
    lax.fori_loop(0, (n_full % (2 * unroll)) // 2, one_pair, 0)

    for w_ref, c_ref in zip(w_refs, c_refs):
        c_ref[...] = w_ref[...].astype(BF16)

    def key_rows(d, a):
        q0 = a % tq
        return d * tk, min((d + 1) * tk, q0 + cw), q0

    slots = (s0_ref, s1_ref)
    for d in range(tq // tk):
        r = pl.multiple_of((n_full + d) * tk, tk)
        r_next = pl.multiple_of((n_full + d + 1) * tk, tk)
        for a, b in groups:
            k0, k1, _ = key_rows(d + 1, a)
            if d + 1 < tq // tk and k1 > k0:
                slots[(d + 1) % 2][a // cw, 0:k1 - k0, :] = (jnp.dot(
                    kn_ref[pl.ds(r_next, k1 - k0), :], qt_ref[:, a:b], preferred_element_type=F32)
                    - ref_ref[:, a:b]).astype(BF16)
            k0, k1, q0 = key_rows(d, a)
            if k1 <= k0:
                continue
            s = slots[d % 2][a // cw, 0:k1 - k0, :].astype(F32) + ref_ref[:, a:b]
            if k1 - 1 > q0:
                key = k0 + lax.broadcasted_iota(jnp.int32, s.shape, 0)
                qry = q0 + lax.broadcasted_iota(jnp.int32, s.shape, 1)
                s = jnp.where(key <= qry, s, NEG_BIG)
            softmax_update(a, b, s, jnp.max(s, axis=0, keepdims=True), vt_ref[:, pl.ds(r, k1 - k0)])

    o1 = acc_ref[0:HEAD_W, 0:tq] * (1.0 / acc_ref[HEAD_W:HEAD_W + 1, 0:tq])
    o2 = acc_ref[0:HEAD_W, tq:2 * tq] * (1.0 / acc_ref[HEAD_W:HEAD_W + 1, tq:2 * tq])
    lam = (jnp.exp(jnp.sum(lq1_ref[...] * lk1_ref[...], axis=-1, keepdims=True))
           - jnp.exp(jnp.sum(lq2_ref[...] * lk2_ref[...], axis=-1, keepdims=True)) + lambda_init)
    o = o1 - lam * o2
    inv = lax.rsqrt(jnp.mean(o * o, axis=0, keepdims=True) + EPS) * (1.0 - lambda_init)
    o_ref[...] = (o * inv * sg_ref[...]).T.astype(BF16)


def _cast_specs(w, first, count, n_steps, n_inner):
    r, c = w.shape[1], w.shape[2]
    per_layer = n_steps // count
    assert per_layer * count == n_steps and first + count <= w.shape[0]
    for nb_c in range(1, per_layer + 1):
        nb_r = per_layer // nb_c
        if per_layer % nb_c or c % nb_c or r % nb_r:
            continue
        br, bc = r // nb_r, c // nb_c
        if br % BF16_SUBLANES == 0 and bc % HEAD_W == 0:
            break
    else:
        raise ValueError(f"no slab split for {w.shape}")

    def index(offset):
        def index_map(h, i):
            t = h * n_inner + i
            rem = t % per_layer
            return (t // per_layer + offset, rem // nb_c, rem % nb_c)
        return index_map

    return (pl.BlockSpec((None, br, bc), index(first)), pl.BlockSpec((None, br, bc), index(0)),
            jax.ShapeDtypeStruct((count, r, c), BF16))


def _attn(z, qg, kg, lq1, lk1, lq2, lk2, sg, cast=(), *, lambda_init, tq=1024, tk=512, cw=256, unroll=4):
    s = z.shape[0]
    assert s % tq == 0 and tq % (2 * tk) == 0 and tq % cw == 0
    small = lambda w: pl.BlockSpec((1, w), lambda h, i: (0, 0))
    column = pl.BlockSpec((HEAD_W, 1), lambda h, i: (0, 0))
    n_inner = s // tq
    cast_specs = [_cast_specs(w, first, count, DA_HEADS * n_inner, n_inner) for w, first, count in cast]
    kern = functools.partial(_attn_kernel, n_cast=len(cast), tq=tq, tk=tk, cw=cw, unroll=unroll,
                             lambda_init=lambda_init)
    out = pl.pallas_call(
        kern,
        out_shape=[jax.ShapeDtypeStruct((s, DA_HEADS * HEAD_W), BF16)] + [c[2] for c in cast_specs],
        grid=(DA_HEADS, n_inner),
        in_specs=[
            pl.BlockSpec((tq, HEAD_W), lambda h, i: (i, QA_BLK + h)),
            pl.BlockSpec((s, HEAD_W), lambda h, i: (0, KA_BLK + h)),
            pl.BlockSpec((s, HEAD_W), lambda h, i: (0, VA_BLK + h)),
            column, small(HEAD_W),
            small(DA_HEAD_DIM), small(DA_HEAD_DIM), small(DA_HEAD_DIM), small(DA_HEAD_DIM),
            column,
        ] + [c[0] for c in cast_specs],
        out_specs=[pl.BlockSpec((tq, HEAD_W), lambda h, i: (i, h))] + [c[1] for c in cast_specs],
        scratch_shapes=[
            pltpu.VMEM((s, HEAD_W), BF16),
            pltpu.VMEM((ACC_ROWS, s), BF16),
            pltpu.VMEM((HEAD_W, 2 * tq), BF16),
            pltpu.VMEM((1, 2 * tq), F32),
            pltpu.VMEM((ACC_ROWS, 2 * tq), F32),
            pltpu.VMEM((2 * tq // cw, tk, cw), BF16),
            pltpu.VMEM((2 * tq // cw, tk, cw), BF16),
            pltpu.VMEM((1, 2 * tq), F32),
            pltpu.VMEM((1, 2 * tq), F32),
            pltpu.VMEM((1, 2 * tq), F32),
        ],
        compiler_params=_params(("arbitrary", "arbitrary"), 58),
        name="diff_attn",
    )(z, z, z, qg, kg, lq1, lk1, lq2, lk2, sg, *[c[0] for c in cast])
    return out[0], out[1:]


def _rope_kernel(pos_ref, invf_ref, *rest, n_cast):
    w_refs, (cos_ref, sin_ref), c_refs = rest[:n_cast], rest[n_cast:n_cast + 2], rest[n_cast + 2:]
    for w_ref, c_ref in zip(w_refs, c_refs):
        c_ref[...] = w_ref[...].astype(BF16)
    ang = pos_ref[...].astype(F32) * invf_ref[...]
    lo = lax.broadcasted_iota(jnp.int32, (1, HEAD_W), 1) < HEAD_W // 2
    cos_ref[...] = jnp.cos(ang)
    sin_ref[...] = jnp.where(lo, -1.0, 1.0) * jnp.sin(ang)


def _rope_tables(pos_col, invf, cast=(), *, tm=1024):
    s = pos_col.shape[0]
    tm = min(tm, s)
    n = s // tm
    cast_specs = [_cast_specs(w, first, count, n, 1) for w, first, count in cast]
    table = pl.BlockSpec((tm, HEAD_W), lambda i, _: (i, 0))
    out = pl.pallas_call(
        functools.partial(_rope_kernel, n_cast=len(cast)),
        out_shape=[jax.ShapeDtypeStruct((s, HEAD_W), F32)] * 2 + [c[2] for c in cast_specs],
        grid=(n, 1),
        in_specs=[pl.BlockSpec((tm, 1), lambda i, _: (i, 0)), pl.BlockSpec((1, HEAD_W), lambda i, _: (0, 0))]
        + [c[0] for c in cast_specs],
        out_specs=[table, table] + [c[1] for c in cast_specs],
        compiler_params=_params(("arbitrary", "arbitrary"), 40),
        name="rope_tables",
    )(pos_col, invf, *[c[0] for c in cast])
    return out[:2], out[2:]


def _ret_kernel(q_ref, k_ref, v_ref, gr_ref, cos_ref, sin_ref, dec_ref, zeta_ref, xi_ref, cd_ref, sg_ref,
                o_ref, st_ref, *, heads):
    @pl.when(pl.program_id(1) == 0)
    def _():
        st_ref[...] = jnp.zeros(st_ref.shape, F32)

    cosf = cos_ref[...]
    sinf = sin_ref[...]
    for hh in range(heads):
        lanes = slice(hh * HEAD_W, (hh + 1) * HEAD_W)
        q = q_ref[:, lanes].astype(F32)
        k = k_ref[:, lanes].astype(F32)
        q = q * cosf + pltpu.roll(q, HEAD_W // 2, 1) * sinf
        k = (k * cosf + pltpu.roll(k, HEAD_W // 2, 1) * sinf) * RET_SCALE
        v = v_ref[:, lanes]
        st = st_ref[hh]

        sc = lax.dot_general(q.astype(BF16), k.astype(BF16), (((1,), (1,)), ((), ())),
                             preferred_element_type=F32) * dec_ref[hh]
        o = jnp.dot(sc.astype(BF16), v, preferred_element_type=F32)
        o = o + jnp.dot((q * xi_ref[hh]).astype(BF16), st.astype(BF16), preferred_element_type=F32)
        kz = (k * zeta_ref[hh]).astype(BF16)
        kv = lax.dot_general(kz, v, (((0,), (0,)), ((), ())), preferred_element_type=F32)
        st_ref[hh] = cd_ref[hh, 0:1, :] * st + kv

        g = gr_ref[:, lanes].astype(F32)
        o_ref[:, lanes] = (_rms(o, sg_ref[...]) * (g * jax.nn.sigmoid(g))).astype(BF16)


def _retention(z, cosf, sinf, dec, zeta, xi, cd, sg, *, chunk, heads=8):
    s = z.shape[0]
    w = heads * HEAD_W
    row = lambda blk: pl.BlockSpec((chunk, w), lambda h, i: (i, blk // heads + h))
    tab = pl.BlockSpec((chunk, HEAD_W), lambda h, i: (i, 0))
    per_head = lambda r, c: pl.BlockSpec((heads, r, c), lambda h, i: (h, 0, 0))
    return pl.pallas_call(
        functools.partial(_ret_kernel, heads=heads),
        out_shape=jax.ShapeDtypeStruct((s, RET_HEADS * HEAD_W), BF16),
        grid=(RET_HEADS // heads, s // chunk),
        in_specs=[
            row(QR_BLK), row(KR_BLK), row(VR_BLK), row(GR_BLK), tab, tab,
            per_head(chunk, chunk), per_head(chunk, HEAD_W), per_head(chunk, HEAD_W), per_head(8, HEAD_W),
            pl.BlockSpec((1, HEAD_W), lambda h, i: (0, 0)),
        ],
        out_specs=pl.BlockSpec((chunk, w), lambda h, i: (i, h)),
        scratch_shapes=[pltpu.VMEM((heads, HEAD_W, HEAD_W), F32)],
        compiler_params=_params(("arbitrary", "arbitrary"), 40),
        name="retention",
    )(z, z, z, z, cosf, sinf, dec, zeta, xi, cd, sg)


def _retention_tables(chunk):
    log_g = jnp.log(1.0 - 2.0 ** (-5.0 - jnp.arange(RET_HEADS, dtype=F32)))
    idx = jnp.arange(chunk, dtype=F32)
    rel = idx[:, None] - idx[None, :]
    dec = jnp.exp(log_g[:, None, None] * jnp.maximum(rel, 0.0)) * (rel >= 0)
    zeta = jnp.exp(log_g[:, None] * (chunk - 1 - idx))
    xi = jnp.exp(log_g[:, None] * (idx + 1.0))
    cd = jnp.exp(log_g * chunk)
    bcast = lambda t: jnp.broadcast_to(t[:, :, None], (RET_HEADS, chunk, HEAD_W))
    return dec, bcast(zeta), bcast(xi), jnp.broadcast_to(cd[:, None, None], (RET_HEADS, 8, HEAD_W))


def _merge_kernel(x_ref, ya_ref, yb_ref, ga0_ref, ga1_ref, gb0_ref, gb1_ref, wa_ref, wb_ref, wo_ref, o_ref):
    ta = jnp.dot(ya_ref[...], wa_ref[...], preferred_element_type=F32)
    tb = jnp.dot(yb_ref[...], wb_ref[...], preferred_element_type=F32)
    ga = jnp.concatenate([ga0_ref[...], ga1_ref[...]], axis=-1).astype(F32)
    gb = jnp.concatenate([gb0_ref[...], gb1_ref[...]], axis=-1).astype(F32)
    merged = jax.nn.sigmoid(ga) * ta + jax.nn.sigmoid(gb) * tb
    o_ref[...] = x_ref[...] + jnp.dot(merged.astype(BF16), wo_ref[...], preferred_element_type=F32)


def _merge(x, ya, yb, z, wa, wb, wo, layer, *, tm=512):
    s, d = x.shape
    gate = lambda blk: pl.BlockSpec((tm, GATE_BLK_W), lambda i: (i, blk))
    resident = lambda w: pl.BlockSpec((None,) + w.shape[1:], lambda i: (layer, 0, 0), pipeline_mode=pl.Buffered(1))
    return pl.pallas_call(
        _merge_kernel,
        out_shape=jax.ShapeDtypeStruct((s, d), F32),
        grid=(s // tm,),
        in_specs=[
            pl.BlockSpec((tm, d), lambda i: (i, 0)),
            pl.BlockSpec((tm, ya.shape[1]), lambda i: (i, 0)),
            pl.BlockSpec((tm, yb.shape[1]), lambda i: (i, 0)),
            gate(GA_BLK), gate(GA_BLK + 1), gate(GB_BLK), gate(GB_BLK + 1),
            resident(wa), resident(wb), resident(wo),
        ],
        out_specs=pl.BlockSpec((tm, d), lambda i: (i, 0)),
        compiler_params=_params(("arbitrary",), 56),
        name="merge_out",
    )(x, ya, yb, z, z, z, z, wa, wb, wo)


def _ple_kernel(x_ref, g_ref, p_ref, wg_ref, wp_ref, o_ref):
    x = x_ref[...]
    h = _rms(x, g_ref[...]).astype(BF16)
    gate = jax.nn.sigmoid(jnp.dot(h, wg_ref[...], preferred_element_type=F32))
    proj = jnp.dot(p_ref[...].astype(BF16), wp_ref[...], preferred_element_type=F32)
    o_ref[...] = x + gate * proj


def _ple(x, g, p, wg, wp, layer, *, tm=512):
    s, d = x.shape
    resident = lambda w: pl.BlockSpec((None,) + w.shape[1:], lambda i: (layer, 0, 0), pipeline_mode=pl.Buffered(1))
    return pl.pallas_call(
        _ple_kernel,
        out_shape=jax.ShapeDtypeStruct((s, d), F32),
        grid=(s // tm,),
        in_specs=[
            pl.BlockSpec((tm, d), lambda i: (i, 0)),
            pl.BlockSpec((1, d), lambda i: (0, 0)),
            pl.BlockSpec((None, tm, p.shape[2]), lambda i: (layer, i, 0)),
            resident(wg), resident(wp),
        ],
        out_specs=pl.BlockSpec((tm, d), lambda i: (i, 0)),
        compiler_params=_params(("arbitrary",), 48),
        name="ple",
    )(x, g, p, wg, wp)


def kernel(x, p, positions, ffn1_norm, ffn1_w_gate, ffn1_w_up, ffn1_w_down, mix_norm, w_in, da_q_norm, da_k_norm, da_lambda_q1, da_lambda_k1, da_lambda_q2, da_lambda_k2, da_sub_norm, ret_sub_norm, w_up_a, w_up_b, w_out, ffn2_norm, ffn2_w_gate, ffn2_w_up, ffn2_w_down, ple_norm, w_ple_gate, w_ple_proj):
    b, s, d = x.shape
    assert b == 1
    depth = w_in.shape[0]
    chunk = min(256, s)

    row = lambda t: t.reshape(1, -1)
    twice = lambda t: jnp.concatenate([t, t]).reshape(1, -1)
    bf = lambda t: t.astype(BF16)

    inv_freq = ROPE_BASE ** (-jnp.arange(0, HEAD_W, 2, dtype=F32) / HEAD_W)
    (cosf, sinf), early = _rope_tables(positions.reshape(s, 1), twice(inv_freq),
                                       [(w, 0, 1) for w in (ffn1_w_gate, ffn1_w_up, ffn1_w_down)])
    dec, zeta, xi, cd = _retention_tables(chunk)

    wpp = bf(w_ple_proj)
    late = [(w, 1, depth - 1) for w in (ffn1_w_gate, ffn1_w_up, ffn1_w_down, w_in)] if depth > 1 else []
    rest = [(w, 0, depth) for w in (ffn2_w_gate, ffn2_w_up, ffn2_w_down, w_up_a, w_up_b, w_out, w_ple_gate)]
    p3 = p.reshape(depth, s, -1)

    xc = x.reshape(s, d)
    for i in range(depth):
        lambda_init = 0.8 - 0.6 * math.exp(-0.3 * i)
        first = 0 if i == 0 else i - 1
        if i == 0:
            xc, (win,) = _ffn(xc, row(ffn1_norm[i]), *early, first, [(w_in, 0, 1)])
        else:
            f1g, f1u, f1d, win = cast_late
            xc, _ = _ffn(xc, row(ffn1_norm[i]), f1g, f1u, f1d, first)
        z = _inproj(xc, row(mix_norm[i]), win, first)
        ya, cast_out = _attn(z, twice(da_q_norm[i]).reshape(-1, 1), twice(da_k_norm[i]), row(da_lambda_q1[i]),
                             row(da_lambda_k1[i]), row(da_lambda_q2[i]), row(da_lambda_k2[i]),
                             da_sub_norm[i].reshape(-1, 1), (late + rest) if i == 0 else (),
                             lambda_init=lambda_init)
        if i == 0:
            cast_late = cast_out[:len(late)]
            f2g, f2u, f2d, wua, wub, wo, wpg = cast_out[len(late):]
        yb = _retention(z, cosf, sinf, dec, zeta, xi, cd, row(ret_sub_norm[i]), chunk=chunk)
        xc = _merge(xc, ya, yb, z, wua, wub, wo, i)
        xc, _ = _ffn(xc, row(ffn2_norm[i]), f2g, f2u, f2d, i)
        xc = _ple(xc, row(ple_norm[i]), p3, wpg, wpp, i)
    return xc.reshape(b, s, d)
```

```python
import functools
import math

import jax
import jax.numpy as jnp
from jax import lax
from jax.experimental import pallas as pl
from jax.experimental.pallas import tpu as pltpu

F32 = jnp.float32
BF16 = jnp.bfloat16

EPS = 1e-6
ROPE_BASE = 10000.0
DA_HEADS = 8
DA_HEAD_DIM = 64
RET_HEADS = 8
HEAD_W = 128
RET_SCALE = HEAD_W ** -0.5
DA_SCALE = DA_HEAD_DIM ** -0.5
NEG_BIG = -1e30
LOG2E = math.log2(math.e)
BF16_SUBLANES = 16
NORM_CHUNKS = 4
ACC_ROWS = HEAD_W + BF16_SUBLANES

QA_BLK, KA_BLK, VA_BLK, QR_BLK, KR_BLK, VR_BLK, GR_BLK = 0, 8, 16, 24, 32, 40, 48
GATE_BLK_W = 1024
GA_BLK, GB_BLK = 7, 9

MIB = 1024 * 1024


def _params(sem, vmem_mib, flags=None):
    return pltpu.CompilerParams(dimension_semantics=sem, vmem_limit_bytes=vmem_mib * MIB, flags=flags)


def _rms(x, g):
    return x * lax.rsqrt(jnp.mean(x * x, axis=-1, keepdims=True) + EPS) * g


def _ffn_kernel(x_ref, g_ref, wg_ref, wu_ref, wd_ref, *rest, n_cast):
    w_refs, o_ref, c_refs, h_ref = rest[:n_cast], rest[n_cast], rest[n_cast + 1:-1], rest[-1]
    for w_ref, c_ref in zip(w_refs, c_refs):
        c_ref[...] = w_ref[...].astype(BF16)
    j = pl.program_id(1)

    def swiglu_down(h):
        gate = jnp.dot(h, wg_ref[...], preferred_element_type=F32)
        up = jnp.dot(h, wu_ref[...], preferred_element_type=F32)
        act = (gate * jax.nn.sigmoid(gate) * up * 0.5).astype(BF16)
        return jnp.dot(act, wd_ref[...], preferred_element_type=F32)

    @pl.when(j == 0)
    def _():
        rows = x_ref.shape[0] // NORM_CHUNKS
        for c in range(NORM_CHUNKS):
            sl = slice(c * rows, (c + 1) * rows)
            x = x_ref[sl, :]
            h = _rms(x, g_ref[...]).astype(BF16)
            h_ref[sl, :] = h
            o_ref[sl, :] = x + swiglu_down(h)

    @pl.when(j > 0)
    def _():
        o_ref[...] += swiglu_down(h_ref[...])


def _ffn(x, g, wg, wu, wd, layer, cast=(), *, tm=1024, tf=512):
    s, d = x.shape
    tm = min(tm, s)
    f = wg.shape[2]
    grid = (s // tm, f // tf)
    cast_specs = [_cast_specs(w, first, count, grid[0] * grid[1], grid[1]) for w, first, count in cast]
    out = pl.pallas_call(
        functools.partial(_ffn_kernel, n_cast=len(cast)),
        out_shape=[jax.ShapeDtypeStruct((s, d), F32)] + [c[2] for c in cast_specs],
        grid=grid,
        in_specs=[
            pl.BlockSpec((tm, d), lambda i, j: (i, 0)),
            pl.BlockSpec((1, d), lambda i, j: (0, 0)),
            pl.BlockSpec((None, d, tf), lambda i, j: (layer, 0, j)),
            pl.BlockSpec((None, d, tf), lambda i, j: (layer, 0, j)),
            pl.BlockSpec((None, tf, d), lambda i, j: (layer, j, 0)),
        ] + [c[0] for c in cast_specs],
        out_specs=[pl.BlockSpec((tm, d), lambda i, j: (i, 0))] + [c[1] for c in cast_specs],
        scratch_shapes=[pltpu.VMEM((tm, d), BF16)],
        compiler_params=_params(("arbitrary", "arbitrary"), 58),
        name="ffn",
    )(x, g, wg, wu, wd, *[c[0] for c in cast])
    return out[0], out[1:]


def _ffn_streamed_kernel(x_ref, g_ref, wg_hbm, wu_hbm, wd_hbm, o_ref, h_ref, *, layer, tf):
    x = x_ref[...]
    h_ref[...] = _rms(x, g_ref[...]).astype(BF16)
    o_ref[...] = x
    d, f = x_ref.shape[1], wg_hbm.shape[2]

    def tile(wg_ref, wu_ref, wd_ref):
        h = h_ref[...]
        gate = jnp.dot(h, wg_ref[...], preferred_element_type=F32)
        up = jnp.dot(h, wu_ref[...], preferred_element_type=F32)
        act = (gate * jax.nn.sigmoid(gate) * up * 0.5).astype(BF16)
        o_ref[...] += jnp.dot(act, wd_ref[...], preferred_element_type=F32)

    pltpu.emit_pipeline(
        tile, grid=(f // tf,),
        in_specs=[pl.BlockSpec((d, tf), lambda j: (0, j)), pl.BlockSpec((d, tf), lambda j: (0, j)),
                  pl.BlockSpec((tf, d), lambda j: (j, 0))],
    )(wg_hbm.at[layer], wu_hbm.at[layer], wd_hbm.at[layer])


def _ffn_streamed(x, g, wg, wu, wd, layer, *, tm=1024, tf=512):
    s, d = x.shape
    tm = min(tm, s)
    any_space = pl.BlockSpec(memory_space=pl.ANY)
    return pl.pallas_call(
        functools.partial(_ffn_streamed_kernel, layer=layer, tf=tf),
        out_shape=jax.ShapeDtypeStruct((s, d), F32),
        grid=(s // tm,),
        in_specs=[pl.BlockSpec((tm, d), lambda i: (i, 0)), pl.BlockSpec((1, d), lambda i: (0, 0)),
                  any_space, any_space, any_space],
        out_specs=pl.BlockSpec((tm, d), lambda i: (i, 0)),
        scratch_shapes=[pltpu.VMEM((tm, d), BF16)],
        compiler_params=_params(("arbitrary",), 58),
        name="ffn_streamed",
    )(x, g, wg, wu, wd)


def _inproj_kernel(x_ref, g_ref, w_ref, z_ref, h_ref):
    j = pl.program_id(1)

    @pl.when(j == 0)
    def _():
        rows = x_ref.shape[0] // NORM_CHUNKS
        for c in range(NORM_CHUNKS):
            sl = slice(c * rows, (c + 1) * rows)
            h = _rms(x_ref[sl, :], g_ref[...]).astype(BF16)
            h_ref[sl, :] = h
            z_ref[sl, :] = jnp.dot(h, w_ref[...], preferred_element_type=F32).astype(BF16)

    @pl.when(j > 0)
    def _():
        z_ref[...] = jnp.dot(h_ref[...], w_ref[...], preferred_element_type=F32).astype(BF16)


def _inproj(x, g, w, layer, *, tm=1024, tn=2816):
    s, d = x.shape
    tm = min(tm, s)
    n = w.shape[2]
    return pl.pallas_call(
        _inproj_kernel,
        out_shape=jax.ShapeDtypeStruct((s, n), BF16),
        grid=(s // tm, n // tn),
        in_specs=[
            pl.BlockSpec((tm, d), lambda i, j: (i, 0)),
            pl.BlockSpec((1, d), lambda i, j: (0, 0)),
            pl.BlockSpec((None, d, tn), lambda i, j: (layer, 0, j)),
        ],
        out_specs=pl.BlockSpec((tm, tn), lambda i, j: (i, j)),
        scratch_shapes=[pltpu.VMEM((tm, d), BF16)],
        compiler_params=_params(("arbitrary", "arbitrary"), 60),
        name="in_proj",
    )(x, g, w)


def _half_norm(x, g, same_half):
    xx = x * x
    hi = xx.astype(BF16)
    lo = (xx - hi.astype(F32)).astype(BF16)
    ss = (jnp.dot(hi, same_half, preferred_element_type=F32)
          + jnp.dot(lo, same_half, preferred_element_type=F32))
    return x * lax.rsqrt(ss * (1.0 / DA_HEAD_DIM) + EPS) * g


def _attn_kernel(*refs, n_cast, tq, tk, cw, unroll, lambda_init):
    q_ref, k_ref, v_ref, qg_ref, kg_ref, lq1_ref, lk1_ref, lq2_ref, lk2_ref, sg_ref = refs[:10]
    w_refs = refs[10:10 + n_cast]
    o_ref = refs[10 + n_cast]
    c_refs = refs[11 + n_cast:11 + 2 * n_cast]
    kn_ref, vt_ref, qt_ref, m_ref, acc_ref, s0_ref, s1_ref, mx0_ref, mx1_ref, ref_ref = refs[11 + 2 * n_cast:]

    i = pl.program_id(1)
    s_len = k_ref.shape[0]
    prep_rows = min(1024, s_len)

    @pl.when(i == 0)
    def _():
        vt_ref[HEAD_W:ACC_ROWS, :] = jnp.ones((ACC_ROWS - HEAD_W, s_len), BF16)
        half = lambda axis: lax.broadcasted_iota(jnp.int32, (HEAD_W, HEAD_W), axis) < DA_HEAD_DIM
        same_half = (half(0) == half(1)).astype(BF16)

        def body(c, carry):
            r = pl.multiple_of(c * prep_rows, prep_rows)
            kk = k_ref[pl.ds(r, prep_rows), :].astype(F32)
            kn_ref[pl.ds(r, prep_rows), :] = _half_norm(kk, kg_ref[...], same_half).astype(BF16)
            vv = v_ref[pl.ds(r, prep_rows), :].astype(F32)
            vt_ref[0:HEAD_W, pl.ds(r, prep_rows)] = vv.T.astype(BF16)
            return carry
        lax.fori_loop(0, s_len // prep_rows, body, 0)

    dh = DA_HEAD_DIM
    qt = q_ref[...].astype(F32).T
    qq = qt * qt
    qscale = DA_SCALE * LOG2E
    inv1 = lax.rsqrt(jnp.sum(qq[0:dh], axis=0, keepdims=True) * (1.0 / dh) + EPS) * qscale
    inv2 = lax.rsqrt(jnp.sum(qq[dh:HEAD_W], axis=0, keepdims=True) * (1.0 / dh) + EPS) * qscale
    zeros = jnp.zeros((dh, tq), BF16)
    qt_ref[0:dh, 0:tq] = (qt[0:dh] * inv1 * qg_ref[0:dh, :]).astype(BF16)
    qt_ref[dh:HEAD_W, 0:tq] = zeros
    qt_ref[0:dh, tq:2 * tq] = zeros
    qt_ref[dh:HEAD_W, tq:2 * tq] = (qt[dh:HEAD_W] * inv2 * qg_ref[dh:HEAD_W, :]).astype(BF16)
    m_ref[...] = jnp.full(m_ref.shape, NEG_BIG, F32)
    acc_ref[...] = jnp.zeros(acc_ref.shape, F32)

    groups = [(c * cw, (c + 1) * cw) for c in range(2 * tq // cw)]

    def softmax_update(a, b, s, s_max, vt):
        m_prev = m_ref[:, a:b]
        m_new = jnp.maximum(m_prev, s_max)
        alpha = jnp.exp2(m_prev - m_new)
        if s.dtype == BF16:
            p = jnp.exp2(s - (m_new - ref_ref[:, a:b]).astype(BF16))
        else:
            p = jnp.exp2((s - m_new).astype(BF16))
        acc_ref[:, a:b] = alpha * acc_ref[:, a:b] + jnp.dot(vt, p, preferred_element_type=F32)
        m_ref[:, a:b] = m_new

    def scores(jb, a, b, s_ref, mx_ref):
        r = pl.multiple_of(jb * tk, tk)
        s = jnp.dot(kn_ref[pl.ds(r, tk), :], qt_ref[:, a:b], preferred_element_type=F32)
        s_ref[a // cw] = (s - ref_ref[:, a:b]).astype(BF16)
        mx_ref[:, a:b] = jnp.max(s, axis=0, keepdims=True)

    def first_scores(a, b):
        head = tk // 4
        s_lo = jnp.dot(kn_ref[0:head, :], qt_ref[:, a:b], preferred_element_type=F32)
        ref = jnp.max(s_lo, axis=0, keepdims=True)
        ref_ref[:, a:b] = ref
        s0_ref[a // cw, 0:head, :] = (s_lo - ref).astype(BF16)
        s_hi = jnp.dot(kn_ref[head:tk, :], qt_ref[:, a:b], preferred_element_type=F32)
        s0_ref[a // cw, head:tk, :] = (s_hi - ref).astype(BF16)
        mx0_ref[:, a:b] = jnp.maximum(ref, jnp.max(s_hi, axis=0, keepdims=True))

    def update(jb, a, b, s_ref, mx_ref):
        r = pl.multiple_of(jb * tk, tk)
        softmax_update(a, b, s_ref[a // cw], mx_ref[:, a:b], vt_ref[:, pl.ds(r, tk)])

    n_full = i * (tq // tk)
    for a, b in groups:
        first_scores(a, b)

    def pair(jb):
        for a, b in groups:
            scores(jb + 1, a, b, s1_ref, mx1_ref)
            update(jb, a, b, s0_ref, mx0_ref)
        for a, b in groups:
            scores(jb + 2, a, b, s0_ref, mx0_ref)
            update(jb + 1, a, b, s1_ref, mx1_ref)

    def unrolled_pairs(t, carry):
        for u in range(unroll):
            pair(2 * unroll * t + 2 * u)
        return carry

    def one_pair(t, carry):
        pair(n_full - n_full % (2 * unroll) + 2 * t)
        return carry

    lax.fori_loop(0, n_full // (2 * unroll), unrolled_pairs, 0)
    lax.fori_loop(0, (n_full % (2 * unroll)) // 2, one_pair, 0)

    for w_ref, c_ref in zip(w_refs, c_refs):
        c_ref[...] = w_ref[...].astype(BF16)

    def key_rows(d, a):
        q0 = a % tq
        return d * tk, min((d + 1) * tk, q0 + cw), q0

    slots = (s0_ref, s1_ref)
    for d in range(tq // tk):
        r = pl.multiple_of((n_full + d) * tk, tk)
        r_next = pl.multiple_of((n_full + d + 1) * tk, tk)
        for a, b in groups:
            k0, k1, _ = key_rows(d + 1, a)
            if d + 1 < tq // tk and k1 > k0:
                slots[(d + 1) % 2][a // cw, 0:k1 - k0, :] = (jnp.dot(
                    kn_ref[pl.ds(r_next, k1 - k0), :], qt_ref[:, a:b], preferred_element_type=F32)
                    - ref_ref[:, a:b]).astype(BF16)
            k0, k1, q0 = key_rows(d, a)
            if k1 <= k0:
                continue
            s = slots[d % 2][a // cw, 0:k1 - k0, :].astype(F32) + ref_ref[:, a:b]
            if k1 - 1 > q0:
                key = k0 + lax.broadcasted_iota(jnp.int32, s.shape, 0)
                qry = q0 + lax.broadcasted_iota(jnp.int32, s.shape, 1)
                s = jnp.where(key <= qry, s, NEG_BIG)
            softmax_update(a, b, s, jnp.max(s, axis=0, keepdims=True), vt_ref[:, pl.ds(r, k1 - k0)])

    o1 = acc_ref[0:HEAD_W, 0:tq] * (1.0 / acc_ref[HEAD_W:HEAD_W + 1, 0:tq])
    o2 = acc_ref[0:HEAD_W, tq:2 * tq] * (1.0 / acc_ref[HEAD_W:HEAD_W + 1, tq:2 * tq])
    lam = (jnp.exp(jnp.sum(lq1_ref[...] * lk1_ref[...], axis=-1, keepdims=True))
           - jnp.exp(jnp.sum(lq2_ref[...] * lk2_ref[...], axis=-1, keepdims=True)) + lambda_init)
    o = o1 - lam * o2
    inv = lax.rsqrt(jnp.mean(o * o, axis=0, keepdims=True) + EPS) * (1.0 - lambda_init)
    o_ref[...] = (o * inv * sg_ref[...]).T.astype(BF16)


def _cast_specs(w, first, count, n_steps, n_inner):
    r, c = w.shape[1], w.shape[2]
    per_layer = n_steps // count
    assert per_layer * count == n_steps and first + count <= w.shape[0]
    for nb_c in range(1, per_layer + 1):
        nb_r = per_layer // nb_c
        if per_layer % nb_c or c % nb_c or r % nb_r:
            continue
        br, bc = r // nb_r, c // nb_c
        if br % BF16_SUBLANES == 0 and bc % HEAD_W == 0:
            break
    else:
        raise ValueError(f"no slab split for {w.shape}")

    def index(offset):
        def index_map(h, i):
            t = h * n_inner + i
            rem = t % per_layer
            return (t // per_layer + offset, rem // nb_c, rem % nb_c)
        return index_map

    return (pl.BlockSpec((None, br, bc), index(first)), pl.BlockSpec((None, br, bc), index(0)),
            jax.ShapeDtypeStruct((count, r, c), BF16))


def _attn(z, qg, kg, lq1, lk1, lq2, lk2, sg, cast=(), *, lambda_init, tq=1024, tk=512, cw=256, unroll=4):
    s = z.shape[0]
    assert s % tq == 0 and tq % (2 * tk) == 0 and tq % cw == 0
    small = lambda w: pl.BlockSpec((1, w), lambda h, i: (0, 0))
    column = pl.BlockSpec((HEAD_W, 1), lambda h, i: (0, 0))
    n_inner = s // tq
    cast_specs = [_cast_specs(w, first, count, DA_HEADS * n_inner, n_inner) for w, first, count in cast]
    kern = functools.partial(_attn_kernel, n_cast=len(cast), tq=tq, tk=tk, cw=cw, unroll=unroll,
                             lambda_init=lambda_init)
    out = pl.pallas_call(
        kern,
        out_shape=[jax.ShapeDtypeStruct((s, DA_HEADS * HEAD_W), BF16)] + [c[2] for c in cast_specs],
        grid=(DA_HEADS, n_inner),
        in_specs=[
            pl.BlockSpec((tq, HEAD_W), lambda h, i: (i, QA_BLK + h)),
            pl.BlockSpec((s, HEAD_W), lambda h, i: (0, KA_BLK + h)),
            pl.BlockSpec((s, HEAD_W), lambda h, i: (0, VA_BLK + h)),
            column, small(HEAD_W),
            small(DA_HEAD_DIM), small(DA_HEAD_DIM), small(DA_HEAD_DIM), small(DA_HEAD_DIM),
            column,
        ] + [c[0] for c in cast_specs],
        out_specs=[pl.BlockSpec((tq, HEAD_W), lambda h, i: (i, h))] + [c[1] for c in cast_specs],
        scratch_shapes=[
            pltpu.VMEM((s, HEAD_W), BF16),
            pltpu.VMEM((ACC_ROWS, s), BF16),
            pltpu.VMEM((HEAD_W, 2 * tq), BF16),
            pltpu.VMEM((1, 2 * tq), F32),
            pltpu.VMEM((ACC_ROWS, 2 * tq), F32),
            pltpu.VMEM((2 * tq // cw, tk, cw), BF16),
            pltpu.VMEM((2 * tq // cw, tk, cw), BF16),
            pltpu.VMEM((1, 2 * tq), F32),
            pltpu.VMEM((1, 2 * tq), F32),
            pltpu.VMEM((1, 2 * tq), F32),
        ],
        compiler_params=_params(("arbitrary", "arbitrary"), 58),
        name="diff_attn",
    )(z, z, z, qg, kg, lq1, lk1, lq2, lk2, sg, *[c[0] for c in cast])
    return out[0], out[1:]


def _rope_kernel(pos_ref, invf_ref, *rest, n_cast):
    w_refs, (cos_ref, sin_ref), c_refs = rest[:n_cast], rest[n_cast:n_cast + 2], rest[n_cast + 2:]
    for w_ref, c_ref in zip(w_refs, c_refs):
        c_ref[...] = w_ref[...].astype(BF16)
    ang = pos_ref[...].astype(F32) * invf_ref[...]
    lo = lax.broadcasted_iota(jnp.int32, (1, HEAD_W), 1) < HEAD_W // 2
    cos_ref[...] = jnp.cos(ang)
    sin_ref[...] = jnp.where(lo, -1.0, 1.0) * jnp.sin(ang)


def _rope_tables(pos_col, invf, cast=(), *, tm=1024):
    s = pos_col.shape[0]
    tm = min(tm, s)
    n = s // tm
    cast_specs = [_cast_specs(w, first, count, n, 1) for w, first, count in cast]
    table = pl.BlockSpec((tm, HEAD_W), lambda i, _: (i, 0))
    out = pl.pallas_call(
        functools.partial(_rope_kernel, n_cast=len(cast)),
        out_shape=[jax.ShapeDtypeStruct((s, HEAD_W), F32)] * 2 + [c[2] for c in cast_specs],
        grid=(n, 1),
        in_specs=[pl.BlockSpec((tm, 1), lambda i, _: (i, 0)), pl.BlockSpec((1, HEAD_W), lambda i, _: (0, 0))]
        + [c[0] for c in cast_specs],
        out_specs=[table, table] + [c[1] for c in cast_specs],
        compiler_params=_params(("arbitrary", "arbitrary"), 40),
        name="rope_tables",
    )(pos_col, invf, *[c[0] for c in cast])
    return out[:2], out[2:]


def _ret_kernel(q_ref, k_ref, v_ref, gr_ref, cos_ref, sin_ref, dec_ref, zeta_ref, xi_ref, cd_ref, sg_ref,
                o_ref, st_ref, *, heads):
    @pl.when(pl.program_id(1) == 0)
    def _():
        st_ref[...] = jnp.zeros(st_ref.shape, F32)

    cosf = cos_ref[...]
    sinf = sin_ref[...]
    for hh in range(heads):
        lanes = slice(hh * HEAD_W, (hh + 1) * HEAD_W)
        q = q_ref[:, lanes].astype(F32)
        k = k_ref[:, lanes].astype(F32)
        q = q * cosf + pltpu.roll(q, HEAD_W // 2, 1) * sinf
        k = (k * cosf + pltpu.roll(k, HEAD_W // 2, 1) * sinf) * RET_SCALE
        v = v_ref[:, lanes]
        st = st_ref[hh]

        sc = lax.dot_general(q.astype(BF16), k.astype(BF16), (((1,), (1,)), ((), ())),
                             preferred_element_type=F32) * dec_ref[hh]
        o = jnp.dot(sc.astype(BF16), v, preferred_element_type=F32)
        o = o + jnp.dot((q * xi_ref[hh]).astype(BF16), st.astype(BF16), preferred_element_type=F32)
        kz = (k * zeta_ref[hh]).astype(BF16)
        kv = lax.dot_general(kz, v, (((0,), (0,)), ((), ())), preferred_element_type=F32)
        st_ref[hh] = cd_ref[hh, 0:1, :] * st + kv

        g = gr_ref[:, lanes].astype(F32)
        o_ref[:, lanes] = (_rms(o, sg_ref[...]) * (g * jax.nn.sigmoid(g))).astype(BF16)


def _retention(z, cosf, sinf, dec, zeta, xi, cd, sg, *, chunk, heads=8):
    s = z.shape[0]
    w = heads * HEAD_W
    row = lambda blk: pl.BlockSpec((chunk, w), lambda h, i: (i, blk // heads + h))
    tab = pl.BlockSpec((chunk, HEAD_W), lambda h, i: (i, 0))
    per_head = lambda r, c: pl.BlockSpec((heads, r, c), lambda h, i: (h, 0, 0))
    return pl.pallas_call(
        functools.partial(_ret_kernel, heads=heads),
        out_shape=jax.ShapeDtypeStruct((s, RET_HEADS * HEAD_W), BF16),
        grid=(RET_HEADS // heads, s // chunk),
        in_specs=[
            row(QR_BLK), row(KR_BLK), row(VR_BLK), row(GR_BLK), tab, tab,
            per_head(chunk, chunk), per_head(chunk, HEAD_W), per_head(chunk, HEAD_W), per_head(8, HEAD_W),
            pl.BlockSpec((1, HEAD_W), lambda h, i: (0, 0)),
        ],
        out_specs=pl.BlockSpec((chunk, w), lambda h, i: (i, h)),
        scratch_shapes=[pltpu.VMEM((heads, HEAD_W, HEAD_W), F32)],
        compiler_params=_params(("arbitrary", "arbitrary"), 40),
        name="retention",
    )(z, z, z, z, cosf, sinf, dec, zeta, xi, cd, sg)


def _retention_tables(chunk):
    log_g = jnp.log(1.0 - 2.0 ** (-5.0 - jnp.arange(RET_HEADS, dtype=F32)))
    idx = jnp.arange(chunk, dtype=F32)
    rel = idx[:, None] - idx[None, :]
    dec = jnp.exp(log_g[:, None, None] * jnp.maximum(rel, 0.0)) * (rel >= 0)
    zeta = jnp.exp(log_g[:, None] * (chunk - 1 - idx))
    xi = jnp.exp(log_g[:, None] * (idx + 1.0))
    cd = jnp.exp(log_g * chunk)
    bcast = lambda t: jnp.broadcast_to(t[:, :, None], (RET_HEADS, chunk, HEAD_W))
    return dec, bcast(zeta), bcast(xi), jnp.broadcast_to(cd[:, None, None], (RET_HEADS, 8, HEAD_W))


def _merge_kernel(x_ref, ya_ref, yb_ref, ga0_ref, ga1_ref, gb0_ref, gb1_ref, wa_ref, wb_ref, wo_ref, o_ref):
    ta = jnp.dot(ya_ref[...], wa_ref[...], preferred_element_type=F32)
    tb = jnp.dot(yb_ref[...], wb_ref[...], preferred_element_type=F32)
    ga = jnp.concatenate([ga0_ref[...], ga1_ref[...]], axis=-1).astype(F32)
    gb = jnp.concatenate([gb0_ref[...], gb1_ref[...]], axis=-1).astype(F32)
    merged = jax.nn.sigmoid(ga) * ta + jax.nn.sigmoid(gb) * tb
    o_ref[...] = x_ref[...] + jnp.dot(merged.astype(BF16), wo_ref[...], preferred_element_type=F32)


def _merge(x, ya, yb, z, wa, wb, wo, layer, *, tm=512):
    s, d = x.shape
    gate = lambda blk: pl.BlockSpec((tm, GATE_BLK_W), lambda i: (i, blk))
    resident = lambda w: pl.BlockSpec((None,) + w.shape[1:], lambda i: (layer, 0, 0), pipeline_mode=pl.Buffered(1))
    return pl.pallas_call(
        _merge_kernel,
        out_shape=jax.ShapeDtypeStruct((s, d), F32),
        grid=(s // tm,),
        in_specs=[
            pl.BlockSpec((tm, d), lambda i: (i, 0)),
            pl.BlockSpec((tm, ya.shape[1]), lambda i: (i, 0)),
            pl.BlockSpec((tm, yb.shape[1]), lambda i: (i, 0)),
            gate(GA_BLK), gate(GA_BLK + 1), gate(GB_BLK), gate(GB_BLK + 1),
            resident(wa), resident(wb), resident(wo),
        ],
        out_specs=pl.BlockSpec((tm, d), lambda i: (i, 0)),
        compiler_params=_params(("arbitrary",), 56),
        name="merge_out",
    )(x, ya, yb, z, z, z, z, wa, wb, wo)


def _ple_kernel(x_ref, g_ref, p_ref, wg_ref, wp_ref, o_ref):
    x = x_ref[...]
    h = _rms(x, g_ref[...]).astype(BF16)
    gate = jax.nn.sigmoid(jnp.dot(h, wg_ref[...], preferred_element_type=F32))
    proj = jnp.dot(p_ref[...].astype(BF16), wp_ref[...], preferred_element_type=F32)
    o_ref[...] = x + gate * proj


def _ple(x, g, p, wg, wp, layer, *, tm=512):
    s, d = x.shape
    resident = lambda w: pl.BlockSpec((None,) + w.shape[1:], lambda i: (layer, 0, 0), pipeline_mode=pl.Buffered(1))
    return pl.pallas_call(
        _ple_kernel,
        out_shape=jax.ShapeDtypeStruct((s, d), F32),
        grid=(s // tm,),
        in_specs=[
            pl.BlockSpec((tm, d), lambda i: (i, 0)),
            pl.BlockSpec((1, d), lambda i: (0, 0)),
            pl.BlockSpec((None, tm, p.shape[2]), lambda i: (layer, i, 0)),
            resident(wg), resident(wp),
        ],
        out_specs=pl.BlockSpec((tm, d), lambda i: (i, 0)),
        compiler_params=_params(("arbitrary",), 48),
        name="ple",
    )(x, g, p, wg, wp)


def kernel(x, p, positions, ffn1_norm, ffn1_w_gate, ffn1_w_up, ffn1_w_down, mix_norm, w_in, da_q_norm, da_k_norm, da_lambda_q1, da_lambda_k1, da_lambda_q2, da_lambda_k2, da_sub_norm, ret_sub_norm, w_up_a, w_up_b, w_out, ffn2_norm, ffn2_w_gate, ffn2_w_up, ffn2_w_down, ple_norm, w_ple_gate, w_ple_proj):
    b, s, d = x.shape
    assert b == 1
    depth = w_in.shape[0]
    chunk = min(256, s)

    row = lambda t: t.reshape(1, -1)
    twice = lambda t: jnp.concatenate([t, t]).reshape(1, -1)
    bf = lambda t: t.astype(BF16)

    inv_freq = ROPE_BASE ** (-jnp.arange(0, HEAD_W, 2, dtype=F32) / HEAD_W)
    (cosf, sinf), early = _rope_tables(positions.reshape(s, 1), twice(inv_freq),
                                       [(w, 0, 1) for w in (ffn1_w_gate, ffn1_w_up, ffn1_w_down)])
    dec, zeta, xi, cd = _retention_tables(chunk)

    wpp = bf(w_ple_proj)
    late = [(w, 1, depth - 1) for w in (ffn1_w_gate, ffn1_w_up, ffn1_w_down, w_in)] if depth > 1 else []
    rest = [(w, 0, depth) for w in (ffn2_w_gate, ffn2_w_up, ffn2_w_down, w_up_a, w_up_b, w_out, w_ple_gate)]
    p3 = p.reshape(depth, s, -1)

    xc = x.reshape(s, d)
    for i in range(depth):
        lambda_init = 0.8 - 0.6 * math.exp(-0.3 * i)
        first = 0 if i == 0 else i - 1
        if i == 0:
            xc, (win,) = _ffn(xc, row(ffn1_norm[i]), *early, first, [(w_in, 0, 1)])
        else:
            f1g, f1u, f1d, win = cast_late
            xc = _ffn_streamed(xc, row(ffn1_norm[i]), f1g, f1u, f1d, first)
        z = _inproj(xc, row(mix_norm[i]), win, first)
        ya, cast_out = _attn(z, twice(da_q_norm[i]).reshape(-1, 1), twice(da_k_norm[i]), row(da_lambda_q1[i]),
                             row(da_lambda_k1[i]), row(da_lambda_q2[i]), row(da_lambda_k2[i]),
                             da_sub_norm[i].reshape(-1, 1), (late + rest) if i == 0 else (),
                             lambda_init=lambda_init)
        if i == 0:
            cast_late = cast_out[:len(late)]
            f2g, f2u, f2d, wua, wub, wo, wpg = cast_out[len(late):]
        yb = _retention(z, cosf, sinf, dec, zeta, xi, cd, row(ret_sub_norm[i]), chunk=chunk)
        xc = _merge(xc, ya, yb, z, wua, wub, wo, i)
        xc = _ffn_streamed(xc, row(ffn2_norm[i]), f2g, f2u, f2d, i)
        xc = _ple(xc, row(ple_norm[i]), p3, wpg, wpp, i)
    return xc.reshape(b, s, d)
```
